```python
import jax, jax.numpy as jnp
from jax import lax
import numpy as np

D_MODEL = 4096
BATCH = 1
SEQ = 8192
DEPTH = 1
DEC_BATCH = 16
DEC_SEQ = 16
PAST_LEN = 1024

CHUNK = 64
Q_BLOCK = 128
SPARSE_Q_BLOCK = 64
EPS = 1e-6
NEG = -1e30

MLA_HEADS = D_MODEL // 128
MLA_Q_LORA = D_MODEL // 4
MLA_KV_LORA = 512
MLA_NOPE = 128
MLA_ROPE = 64
MLA_V = 128
MLA_THETA = 10000.0
MLA_SCALE = (MLA_NOPE + MLA_ROPE) ** -0.5

DSA_HEADS = D_MODEL // 128
DSA_KV_HEADS = DSA_HEADS // 4
DSA_HEAD_DIM = 128
DSA_ROT = DSA_HEAD_DIM // 4
DSA_SCALE = DSA_HEAD_DIM ** -0.5
IDX_HEADS = D_MODEL // 128
IDX_DIM = 128
IDX_ROT = IDX_DIM // 4
IDX_W_SCALE = IDX_HEADS ** -0.5
TOPK_MAX = 256
ROPE_THETA = 500000.0

D_FF = 4 * D_MODEL
N_ADA = 6

IN_SPLITS = (MLA_Q_LORA, MLA_KV_LORA, MLA_ROPE,
             DSA_HEADS * DSA_HEAD_DIM, DSA_KV_HEADS * DSA_HEAD_DIM, DSA_KV_HEADS * DSA_HEAD_DIM,
             IDX_HEADS * IDX_DIM, IDX_DIM, IDX_HEADS,
             D_MODEL, D_MODEL)
D_IN = sum(IN_SPLITS)

kernel_name = 'hybrid_mla_dsa_streaming_encoder_step'


def rmsnorm(x, g):
    xf = x.astype(jnp.float32)
    y = xf * lax.rsqrt(jnp.mean(xf * xf, axis=-1, keepdims=True) + EPS)
    return (y * g.astype(jnp.float32)).astype(x.dtype)


def rope(x, pos, theta, rot_dim):
    half = rot_dim // 2
    inv = theta ** (-jnp.arange(half, dtype=jnp.float32) / half)
    ang = pos.astype(jnp.float32)[:, None] * inv[None, :]
    ang = ang.reshape(ang.shape[0], *([1] * (x.ndim - 3)), half)
    cos = jnp.cos(ang).astype(x.dtype)
    sin = jnp.sin(ang).astype(x.dtype)
    x1, x2, rest = x[..., :half], x[..., half:rot_dim], x[..., rot_dim:]
    return jnp.concatenate([x1 * cos - x2 * sin, x2 * cos + x1 * sin, rest], axis=-1)


def chunk_mask(q_pos, k_pos):
    return (k_pos[None, :] // CHUNK) <= (q_pos[:, None] // CHUNK)


def split_cols(z):
    offsets = [int(o) for o in np.cumsum(IN_SPLITS)[:-1]]
    return jnp.split(z, offsets, axis=-1)


def sweep_blocks(fn, block, q_pos, *qs):
    T = q_pos.shape[0]
    if T <= block:
        return fn(q_pos, *qs)
    nb = T // block
    qb = tuple(jnp.moveaxis(q.reshape(q.shape[0], nb, block, *q.shape[2:]), 1, 0) for q in qs)
    pb = q_pos.reshape(nb, block)
    out = lax.map(lambda a: fn(*a), (pb,) + qb)
    out = jnp.moveaxis(out, 0, 1)
    return out.reshape(out.shape[0], T, *out.shape[3:])


def mla_attend(q_pos, q_lat, q_rope, ckv_all, krope_all, k_pos):
    s = (jnp.einsum('bthc,blc->bhtl', q_lat, ckv_all)
         + jnp.einsum('bthr,blr->bhtl', q_rope, krope_all)).astype(jnp.float32) * MLA_SCALE
    s = jnp.where(chunk_mask(q_pos, k_pos)[None, None], s, NEG)
    p = jax.nn.softmax(s, axis=-1).astype(ckv_all.dtype)
    return jnp.einsum('bhtl,blc->bthc', p, ckv_all)


def dsa_attend(q_pos, q, qi, wi, k_all, v_all, ki_all, k_pos, topk):
    f32 = jnp.float32
    adm = chunk_mask(q_pos, k_pos)
    rel = jax.nn.relu(jnp.einsum('bthd,bld->bthl', qi, ki_all).astype(f32))
    score = jnp.einsum('bth,bthl->btl', wi.astype(f32) * IDX_W_SCALE, rel)
    score = jnp.where(adm[None], score, NEG)
    _, sel = lax.top_k(score, topk)
    valid = (k_pos[sel] // CHUNK) <= (q_pos[None, :, None] // CHUNK)
    kg = jax.vmap(lambda kb, sb: kb[sb])(k_all, sel)
    vg = jax.vmap(lambda vb, sb: vb[sb])(v_all, sel)
    B, T, H, D = q.shape
    qg = q.reshape(B, T, DSA_KV_HEADS, H // DSA_KV_HEADS, D)
    s = jnp.einsum('btgrd,btkgd->btgrk', qg, kg).astype(f32) * DSA_SCALE
    s = jnp.where(valid[:, :, None, None, :], s, NEG)
    p = jax.nn.softmax(s, axis=-1).astype(v_all.dtype)
    o = jnp.einsum('btgrk,btkgd->btgrd', p, vg)
    return o.reshape(B, T, H * D)


def encoder_layer(x, c, past, q_pos, k_pos, topk,
                  w_ada, b_ada, g_norm1, w_in, g_q_lora, w_uq, g_kv_lora, w_uk, w_uv,
                  w_out, g_norm2, w_up, w_down):
    B, T, _ = x.shape
    ada = jnp.einsum('bd,de->be', jax.nn.silu(c), w_ada) + b_ada
    sh1, sc1, gt1, sh2, sc2, gt2 = jnp.split(ada[:, None, :], N_ADA, axis=-1)

    h = rmsnorm(x, g_norm1) * (1.0 + sc1) + sh1
    z = jnp.einsum('btd,de->bte', h, w_in)
    (q_lat_in, ckv, krope, q_b, k_b, v_b, qi, ki, wi, gate_a, gate_b) = split_cols(z)

    q_a = jnp.einsum('btc,ce->bte', rmsnorm(q_lat_in, g_q_lora), w_uq)
    q_a = q_a.reshape(B, T, MLA_HEADS, MLA_NOPE + MLA_ROPE)
    q_nope = q_a[..., :MLA_NOPE]
    q_rope = rope(q_a[..., MLA_NOPE:], q_pos, MLA_THETA, MLA_ROPE)
    ckv = rmsnorm(ckv, g_kv_lora)
    krope = rope(krope, q_pos, MLA_THETA, MLA_ROPE)
    q_lat = jnp.einsum('bthn,chn->bthc', q_nope, w_uk)

    q_b = rope(q_b.reshape(B, T, DSA_HEADS, DSA_HEAD_DIM), q_pos, ROPE_THETA, DSA_ROT)
    k_b = rope(k_b.reshape(B, T, DSA_KV_HEADS, DSA_HEAD_DIM), q_pos, ROPE_THETA, DSA_ROT)
    v_b = v_b.reshape(B, T, DSA_KV_HEADS, DSA_HEAD_DIM)
    qi = rope(qi.reshape(B, T, IDX_HEADS, IDX_DIM), q_pos, ROPE_THETA, IDX_ROT)
    ki = rope(ki, q_pos, ROPE_THETA, IDX_ROT)

    if past is None:
        ckv_all, krope_all, k_all, v_all, ki_all = ckv, krope, k_b, v_b, ki
    else:
        p_ckv, p_krope, p_k, p_v, p_ki = past
        ckv_all = jnp.concatenate([p_ckv, ckv], axis=1)
        krope_all = jnp.concatenate([p_krope, krope], axis=1)
        k_all = jnp.concatenate([p_k, k_b], axis=1)
        v_all = jnp.concatenate([p_v, v_b], axis=1)
        ki_all = jnp.concatenate([p_ki, ki], axis=1)

    o_lat = sweep_blocks(lambda p, ql, qr: mla_attend(p, ql, qr, ckv_all, krope_all, k_pos),
                         Q_BLOCK, q_pos, q_lat, q_rope)
    out_a = jnp.einsum('bthc,chv->bthv', o_lat, w_uv).reshape(B, T, MLA_HEADS * MLA_V)
    out_b = sweep_blocks(lambda p, qq, qqi, wwi: dsa_attend(p, qq, qqi, wwi, k_all, v_all, ki_all, k_pos, topk),
                         SPARSE_Q_BLOCK, q_pos, q_b, qi, wi)

    merged = jax.nn.sigmoid(gate_a) * out_a + jax.nn.sigmoid(gate_b) * out_b
    x = x + gt1 * jnp.einsum('btd,de->bte', merged, w_out)

    h2 = rmsnorm(x, g_norm2) * (1.0 + sc2) + sh2
    u = jax.nn.relu(jnp.einsum('btd,df->btf', h2, w_up))
    x = x + gt2 * jnp.einsum('btf,fd->btd', u * u, w_down)
    return x, (ckv, krope, k_b, v_b, ki)


def setup_inputs(seed: int = 0) -> dict:
    key = jax.random.key(seed)
    ks = jax.random.split(key, 23)
    f32 = jnp.float32

    def nrm(k, shape, scale=1.0):
        return jax.random.normal(k, shape, f32) * scale

    def gain(k, shape):
        return 1.0 + 0.02 * jax.random.normal(k, shape, f32)

    return {
        'x_prompt': nrm(ks[0], (BATCH, SEQ, D_MODEL)),
        'x_sample': nrm(ks[1], (DEC_BATCH, DEC_SEQ, D_MODEL)),
        'c_prompt': nrm(ks[2], (BATCH, D_MODEL)),
        'c_sample': nrm(ks[3], (DEC_BATCH, D_MODEL)),
        'cache_mla_ckv': nrm(ks[4], (DEPTH, DEC_BATCH, PAST_LEN, MLA_KV_LORA)),
        'cache_mla_krope': nrm(ks[5], (DEPTH, DEC_BATCH, PAST_LEN, MLA_ROPE)),
        'cache_dsa_k': nrm(ks[6], (DEPTH, DEC_BATCH, PAST_LEN, DSA_KV_HEADS, DSA_HEAD_DIM)),
        'cache_dsa_v': nrm(ks[7], (DEPTH, DEC_BATCH, PAST_LEN, DSA_KV_HEADS, DSA_HEAD_DIM)),
        'cache_idx_k': nrm(ks[8], (DEPTH, DEC_BATCH, PAST_LEN, IDX_DIM)),
        'w_ada': nrm(ks[9], (DEPTH, D_MODEL, N_ADA * D_MODEL), 0.5 * D_MODEL ** -0.5),
        'b_ada': nrm(ks[10], (DEPTH, N_ADA * D_MODEL), 0.02),
        'g_norm1': gain(ks[11], (DEPTH, D_MODEL)),
        'w_in': nrm(ks[12], (DEPTH, D_MODEL, D_IN), D_MODEL ** -0.5),
        'g_q_lora': gain(ks[13], (DEPTH, MLA_Q_LORA)),
        'w_uq': nrm(ks[14], (DEPTH, MLA_Q_LORA, MLA_HEADS * (MLA_NOPE + MLA_ROPE)), MLA_Q_LORA ** -0.5),
        'g_kv_lora': gain(ks[15], (DEPTH, MLA_KV_LORA)),
        'w_uk': nrm(ks[16], (DEPTH, MLA_KV_LORA, MLA_HEADS, MLA_NOPE), MLA_KV_LORA ** -0.5),
        'w_uv': nrm(ks[17], (DEPTH, MLA_KV_LORA, MLA_HEADS, MLA_V), MLA_KV_LORA ** -0.5),
        'w_out': nrm(ks[18], (DEPTH, D_MODEL, D_MODEL), D_MODEL ** -0.5),
        'g_norm2': gain(ks[19], (DEPTH, D_MODEL)),
        'w_up': nrm(ks[20], (DEPTH, D_MODEL, D_FF), D_MODEL ** -0.5),
        'w_down': nrm(ks[21], (DEPTH, D_FF, D_MODEL), D_FF ** -0.5),
        'g_final': gain(ks[22], (D_MODEL,)),
    }


def reference(x_prompt, x_sample, c_prompt, c_sample,
              cache_mla_ckv, cache_mla_krope, cache_dsa_k, cache_dsa_v, cache_idx_k,
              w_ada, b_ada, g_norm1, w_in, g_q_lora, w_uq, g_kv_lora, w_uk, w_uv,
              w_out, g_norm2, w_up, w_down, g_final):
    topk_prompt = min(TOPK_MAX, SEQ // 4)
    topk_sample = min(TOPK_MAX, (PAST_LEN + DEC_SEQ) // 4)
    pos_prompt = jnp.arange(SEQ, dtype=jnp.int32)
    pos_sample_q = PAST_LEN + jnp.arange(DEC_SEQ, dtype=jnp.int32)
    pos_sample_k = jnp.arange(PAST_LEN + DEC_SEQ, dtype=jnp.int32)

    xp, xs = x_prompt, x_sample
    rows_p, rows_s = [], []
    for l in range(DEPTH):
        lw = (w_ada[l], b_ada[l], g_norm1[l], w_in[l], g_q_lora[l], w_uq[l], g_kv_lora[l],
              w_uk[l], w_uv[l], w_out[l], g_norm2[l], w_up[l], w_down[l])
        xp, rp = encoder_layer(xp, c_prompt, None, pos_prompt, pos_prompt, topk_prompt, *lw)
        past = (cache_mla_ckv[l], cache_mla_krope[l], cache_dsa_k[l], cache_dsa_v[l], cache_idx_k[l])
        xs, rs = encoder_layer(xs, c_sample, past, pos_sample_q, pos_sample_k, topk_sample, *lw)
        rows_p.append(rp)
        rows_s.append(rs)

    y_prompt = rmsnorm(xp, g_final)
    y_sample = rmsnorm(xs, g_final)
    new_ckv_prompt = jnp.stack([r[0] for r in rows_p])
    new_krope_prompt = jnp.stack([r[1] for r in rows_p])
    new_k_prompt = jnp.stack([r[2] for r in rows_p])
    new_v_prompt = jnp.stack([r[3] for r in rows_p])
    new_idxk_prompt = jnp.stack([r[4] for r in rows_p])
    new_ckv_sample = jnp.stack([r[0] for r in rows_s])
    new_krope_sample = jnp.stack([r[1] for r in rows_s])
    new_k_sample = jnp.stack([r[2] for r in rows_s])
    new_v_sample = jnp.stack([r[3] for r in rows_s])
    new_idxk_sample = jnp.stack([r[4] for r in rows_s])
    return (y_prompt, y_sample,
            new_ckv_prompt, new_krope_prompt, new_k_prompt, new_v_prompt, new_idxk_prompt,
            new_ckv_sample, new_krope_sample, new_k_sample, new_v_sample, new_idxk_sample)
```

```python
import functools

import numpy as np
import jax
import jax.numpy as jnp
from jax import lax
from jax.experimental import pallas as pl
from jax.experimental.pallas import tpu as pltpu

F32 = jnp.float32
BF16 = jnp.bfloat16
I32 = jnp.int32

LANES = 128
CHUNK = 64
CHUNK_LOG2 = 6
EPS = 1e-6
NEG = -1e30
INT_MIN = -2 ** 31
MLA_THETA = 10000.0
ROPE_THETA = 500000.0
HEAD_DIM = 128
DSA_ROT = HEAD_DIM // 4
IDX_ROT = HEAD_DIM // 4
TOPK_MAX = 256
N_ADA = 6
VMEM_LIMIT = 56 * 1024 * 1024
TQ_MLA, TQ_DSA = 512, 128
TK_PROMPT, TK_SAMPLE = 512, 384


def _cparams(sem):
    return pltpu.CompilerParams(dimension_semantics=sem, vmem_limit_bytes=VMEM_LIMIT)


def _pick(n, pref):
    if n <= pref:
        return n
    t = pref
    while n % t:
        t //= 2
    return t


def _rope_table(pos, theta, rot_dim):
    half = rot_dim // 2
    inv = theta ** (-jnp.arange(half, dtype=F32) / half)
    ang = pos.astype(F32)[:, None] * inv[None, :]
    cos, sin = jnp.cos(ang), jnp.sin(ang)
    n = pos.shape[0]
    ones = jnp.ones((n, LANES - rot_dim), F32)
    zeros = jnp.zeros((n, LANES - rot_dim), F32)
    zh = jnp.zeros((n, half), F32)
    a = jnp.concatenate([cos, cos, ones], axis=1)
    b = jnp.concatenate([-sin, zh, zeros], axis=1)
    c = jnp.concatenate([zh, sin, zeros], axis=1)
    return jnp.concatenate([a, b, c], axis=1)


def _rope_slab(x, tab, half):
    a = tab[:, 0:LANES]
    b = tab[:, LANES:2 * LANES]
    c = tab[:, 2 * LANES:3 * LANES]
    return x * a + pltpu.roll(x, LANES - half, 1) * b + pltpu.roll(x, half, 1) * c


def _mm_body(*refs, nk, n_extra, n_out, epilogue):
    a_ref, w_ref = refs[0], refs[1]
    extra = refs[2:2 + n_extra]
    outs = refs[2 + n_extra:2 + n_extra + n_out]
    a = a_ref[...].astype(BF16)
    w = w_ref[...].astype(BF16)
    part = jnp.dot(a, w, preferred_element_type=F32)
    if nk == 1:
        epilogue(part, extra, outs)
        return
    acc_ref = refs[2 + n_extra + n_out]
    k = pl.program_id(2)

    @pl.when(k == 0)
    def _():
        acc_ref[...] = part

    @pl.when(k > 0)
    def _():
        acc_ref[...] += part

    @pl.when(k == nk - 1)
    def _():
        epilogue(acc_ref[...], extra, outs)


def _matmul(a, w, *, tm, tn, tk, extras, outs, epilogue, name, order="nm"):
    m, kdim = a.shape
    n = w.shape[1]
    gi, gj, gk = m // tm, n // tn, kdim // tk
    if order == "nm":
        grid = (gj, gi, gk)
        wrap = lambda f: (lambda pj, pi, pk: f(pi, pj, pk))
    else:
        grid = (gi, gj, gk)
        wrap = lambda f: (lambda pi, pj, pk: f(pi, pj, pk))
    in_specs = [
        pl.BlockSpec((tm, tk), wrap(lambda i, j, k: (i, k))),
        pl.BlockSpec((tk, tn), wrap(lambda i, j, k: (k, j))),
    ]
    args = [a, w]
    for arr, blk, fn in extras:
        in_specs.append(pl.BlockSpec(blk, wrap(lambda i, j, k, fn=fn: fn(i, j))))
        args.append(arr)
    out_specs = [pl.BlockSpec(blk, wrap(lambda i, j, k, fn=fn: fn(i, j))) for _, blk, fn in outs]
    out_shape = [s for s, _, _ in outs]
    scratch = [pltpu.VMEM((tm, tn), F32)] if gk > 1 else []
    body = functools.partial(_mm_body, nk=gk, n_extra=len(extras), n_out=len(outs), epilogue=epilogue)
    res = pl.pallas_call(
        body, grid=grid, in_specs=in_specs, out_specs=out_specs, out_shape=out_shape,
        scratch_shapes=scratch, name=name,
        compiler_params=_cparams(("parallel", "parallel", "arbitrary")),
    )(*args)
    return res


def _ep_plain(acc, extra, outs):
    outs[0][...] = acc.astype(outs[0].dtype)


def _ep_rmsnorm(acc, extra, outs):
    g = extra[0][...]
    y = acc * lax.rsqrt(jnp.mean(acc * acc, axis=-1, keepdims=True) + EPS) * g
    outs[0][...] = y.astype(outs[0].dtype)


def _ep_sigmoid(acc, extra, outs):
    outs[0][...] = jax.nn.sigmoid(acc).astype(outs[0].dtype)


def _ep_relu2(acc, extra, outs):
    r = jnp.maximum(acc, 0.0)
    outs[0][...] = (r * r).astype(outs[0].dtype)


def _ep_residual(acc, extra, outs):
    x = extra[0][...]
    gate = extra[1][...]
    outs[0][...] = x + gate * acc


def _ep_rope(acc, extra, outs, *, half, period, roped, scale):
    tab = extra[0][...]
    o = outs[0]
    for s in range(acc.shape[1] // LANES):
        x = acc[:, s * LANES:(s + 1) * LANES]
        if (s % period) in roped:
            x = _rope_slab(x, tab, half)
        if scale != 1.0:
            x = x * scale
        o[:, s * LANES:(s + 1) * LANES] = x.astype(o.dtype)


def _ep_kfull(acc, extra, outs):
    kr = extra[0][...]
    o = outs[0]
    for h in range(acc.shape[1] // LANES):
        o[:, (2 * h) * LANES:(2 * h + 1) * LANES] = acc[:, h * LANES:(h + 1) * LANES].astype(o.dtype)
        o[:, (2 * h + 1) * LANES:(2 * h + 2) * LANES] = kr


TM, TN, TK = 512, 1024, 4096


def _proj(name, a, w, ep, out_dtype, *, tm=TM, tn=TN, tk=TK, extras=(), out_cols=None, order="nm"):
    m, k = a.shape
    n = w.shape[1]
    tm, tn, tk = _pick(m, tm), _pick(n, tn), _pick(k, tk)
    oc = n if out_cols is None else out_cols
    outs = [(jax.ShapeDtypeStruct((m, oc), out_dtype), (tm, tn * oc // n), lambda i, j: (i, j))]
    return _matmul(a, w, tm=tm, tn=tn, tk=tk, extras=list(extras), outs=outs, epilogue=ep, name=name,
                   order=order)[0]


def _ada_body(c_ref, w_ref, b_ref, o_ref):
    c = c_ref[...]
    a = (c * jax.nn.sigmoid(c)).astype(BF16)
    o_ref[...] = jnp.dot(a, w_ref[...].astype(BF16), preferred_element_type=F32) + b_ref[...]


def _ada(c_all, w_ada, b_ada):
    r, d = c_all.shape
    n = w_ada.shape[1]
    tn = _pick(n, 512)
    return pl.pallas_call(
        _ada_body, grid=(n // tn,),
        in_specs=[pl.BlockSpec((r, d), lambda j: (0, 0)),
                  pl.BlockSpec((d, tn), lambda j: (0, j)),
                  pl.BlockSpec((1, tn), lambda j: (0, j))],
        out_specs=pl.BlockSpec((r, tn), lambda j: (0, j)),
        out_shape=jax.ShapeDtypeStruct((r, n), F32), name="adaln",
        compiler_params=_cparams(("parallel",)),
    )(c_all, w_ada, b_ada.reshape(1, n))


def _norm_body(*refs, modulate):
    x = refs[0][...]
    g = refs[1][...]
    y = x * lax.rsqrt(jnp.mean(x * x, axis=-1, keepdims=True) + EPS) * g
    if modulate:
        y = y * (1.0 + refs[2][...]) + refs[3][...]
        o = refs[4]
    else:
        o = refs[2]
    o[...] = y.astype(o.dtype)


def _norm(x, g, sc, sh, out_dtype):
    m, d = x.shape
    tm = _pick(m, 256)
    in_specs = [pl.BlockSpec((tm, d), lambda i: (i, 0)), pl.BlockSpec((1, d), lambda i: (0, 0))]
    args = [x, g.reshape(1, d)]
    if sc is not None:
        per_row = sc.shape[0] == m
        blk = (tm, d) if per_row else (1, d)
        fn = (lambda i: (i, 0)) if per_row else (lambda i: (0, 0))
        in_specs += [pl.BlockSpec(blk, fn), pl.BlockSpec(blk, fn)]
        args += [sc, sh]
    return pl.pallas_call(
        functools.partial(_norm_body, modulate=sc is not None), grid=(m // tm,),
        in_specs=in_specs, out_specs=pl.BlockSpec((tm, d), lambda i: (i, 0)),
        out_shape=jax.ShapeDtypeStruct((m, d), out_dtype), name="rmsnorm",
        compiler_params=_cparams(("parallel",)),
    )(*args)


def _finish_body(z_ref, g_ref, tm_ref, td_ref,
                 ckv_o, ckvb_o, kr_o, krb_o, k_o, kb_o, v_o, vb_o, ki_o, kib_o, wi_o,
                 *, c_lora, r_mla, n_kv, idx_scale):
    tab_m = tm_ref[...]
    tab_d = td_ref[...]
    off = 0
    ckv = z_ref[:, off:off + c_lora]
    ckv = ckv * lax.rsqrt(jnp.mean(ckv * ckv, axis=-1, keepdims=True) + EPS) * g_ref[...]
    ckv_o[...] = ckv
    ckvb_o[...] = ckv.astype(BF16)
    off += c_lora
    kr = _rope_slab(z_ref[:, off:off + LANES], tab_m, r_mla // 2)
    kr_o[...] = kr[:, 0:r_mla]
    krb_o[...] = kr.astype(BF16)
    off += LANES
    for h in range(n_kv):
        kh = _rope_slab(z_ref[:, off + h * LANES:off + (h + 1) * LANES], tab_d, DSA_ROT // 2)
        k_o[:, h * LANES:(h + 1) * LANES] = kh
        kb_o[:, h * LANES:(h + 1) * LANES] = kh.astype(BF16)
    off += n_kv * LANES
    v = z_ref[:, off:off + n_kv * LANES]
    v_o[...] = v
    vb_o[...] = v.astype(BF16)
    off += n_kv * LANES
    ki = _rope_slab(z_ref[:, off:off + LANES], tab_d, IDX_ROT // 2)
    ki_o[...] = ki
    kib_o[...] = ki.astype(BF16)
    off += LANES
    wi_o[...] = z_ref[:, off:off + LANES] * idx_scale


def _finish(z, g_kv, tab_m, tab_d, *, c_lora, r_mla, n_kv, idx_scale):
    m, zc = z.shape
    tm = _pick(m, 256)
    kvw = n_kv * LANES
    row = lambda i: (i, 0)
    shapes = [(c_lora, F32), (c_lora, BF16), (r_mla, F32), (LANES, BF16), (kvw, F32), (kvw, BF16),
              (kvw, F32), (kvw, BF16), (LANES, F32), (LANES, BF16), (LANES, F32)]
    return pl.pallas_call(
        functools.partial(_finish_body, c_lora=c_lora, r_mla=r_mla, n_kv=n_kv, idx_scale=idx_scale),
        grid=(m // tm,),
        in_specs=[pl.BlockSpec((tm, zc), row), pl.BlockSpec((1, c_lora), lambda i: (0, 0)),
                  pl.BlockSpec((tm, 3 * LANES), row), pl.BlockSpec((tm, 3 * LANES), row)],
        out_specs=[pl.BlockSpec((tm, w), row) for w, _ in shapes],
        out_shape=[jax.ShapeDtypeStruct((m, w), dt) for w, dt in shapes], name="finish_small",
        compiler_params=_cparams(("parallel",)),
    )(z, g_kv.reshape(1, c_lora), tab_m, tab_d)


def _row_limits(q_first, tq, l_valid):
    rows = q_first + lax.broadcasted_iota(I32, (tq, 1), 0)
    chunk_end = lambda p: (lax.shift_right_logical(p, CHUNK_LOG2) + 1) * CHUNK
    lim = jnp.minimum(chunk_end(rows), l_valid)
    lim_min = jnp.minimum(chunk_end(q_first), l_valid)
    lim_max = jnp.minimum(chunk_end(q_first + tq - 1), l_valid)
    return lim, lim_min, lim_max


def _softmax_step(s, v, m_sc, l_sc, acc_sc):
    m_prev = m_sc[...]
    m_new = jnp.maximum(m_prev, jnp.max(s, axis=1, keepdims=True))
    p = jnp.exp(s - m_new)
    alpha = jnp.exp(m_prev - m_new)
    l_sc[...] = alpha * l_sc[...] + jnp.sum(p, axis=1, keepdims=True)
    acc_sc[...] = alpha * acc_sc[...] + jnp.dot(p.astype(BF16), v, preferred_element_type=F32)
    m_sc[...] = m_new


_NT = (((1,), (1,)), ((), ()))


def _mla_body(q_ref, k_ref, v_ref, o_ref, m_sc, l_sc, acc_sc, *, tq, tk, q0, l_valid):
    q_first = q0 + pl.program_id(2) * tq
    lim, lim_min, lim_max = _row_limits(q_first, tq, l_valid)
    n_full = lim_min // tk
    n_all = (lim_max + tk - 1) // tk
    q = q_ref[...]
    col = lax.broadcasted_iota(I32, (1, tk), 1)
    m_sc[...] = jnp.full(m_sc.shape, NEG, F32)
    l_sc[...] = jnp.zeros(l_sc.shape, F32)
    acc_sc[...] = jnp.zeros(acc_sc.shape, F32)

    def step(c, masked):
        start = pl.multiple_of(c * tk, tk)
        s = lax.dot_general(q, k_ref[pl.ds(start, tk), :], _NT, preferred_element_type=F32)
        if masked:
            s = jnp.where(col + start < lim, s, NEG)
        _softmax_step(s, v_ref[pl.ds(start, tk), :], m_sc, l_sc, acc_sc)

    def full_step(c, carry):
        step(c, False)
        return carry

    def masked_step(c, carry):
        step(c, True)
        return carry

    lax.fori_loop(0, n_full, full_step, 0)
    lax.fori_loop(n_full, n_all, masked_step, 0)
    o_ref[...] = (acc_sc[...] * (1.0 / l_sc[...])).astype(o_ref.dtype)


def _mla_attention(q_full, k_full, v, *, n_heads, tq, tk, q0, l_valid):
    b, t, _ = q_full.shape
    lp = k_full.shape[1]
    return pl.pallas_call(
        functools.partial(_mla_body, tq=tq, tk=tk, q0=q0, l_valid=l_valid),
        grid=(b, n_heads, t // tq),
        in_specs=[pl.BlockSpec((None, tq, 2 * LANES), lambda bi, h, qi: (bi, qi, h)),
                  pl.BlockSpec((None, lp, 2 * LANES), lambda bi, h, qi: (bi, 0, h)),
                  pl.BlockSpec((None, lp, LANES), lambda bi, h, qi: (bi, 0, h))],
        out_specs=pl.BlockSpec((None, tq, LANES), lambda bi, h, qi: (bi, qi, h)),
        out_shape=jax.ShapeDtypeStruct((b, t, n_heads * LANES), BF16),
        scratch_shapes=[pltpu.VMEM((tq, 1), F32), pltpu.VMEM((tq, 1), F32), pltpu.VMEM((tq, LANES), F32)],
        name="mla_attention",
        compiler_params=_cparams(("parallel", "parallel", "arbitrary")),
    )(q_full, k_full, v)


def _dsa_body(qb_ref, qi_ref, wi_ref, ki_ref, k_ref, v_ref, o_ref,
              keys_sc, thr_sc, m_sc, l_sc, acc_sc, *, tq, tk, q0, l_valid, n_idx, rep, topk):
    q_first = q0 + pl.program_id(1) * tq
    lim, _, lim_max = _row_limits(q_first, tq, l_valid)
    n_all = (lim_max + tk - 1) // tk
    col = lax.broadcasted_iota(I32, (1, tk), 1)

    @pl.when(pl.program_id(2) == 0)
    def _():
        w = wi_ref[...]

        def score_chunk(c, carry):
            start = pl.multiple_of(c * tk, tk)
            kic = ki_ref[pl.ds(start, tk), :]
            s = jnp.zeros((tq, tk), F32)
            for h in range(n_idx):
                y = lax.dot_general(qi_ref[:, h * LANES:(h + 1) * LANES], kic, _NT,
                                    preferred_element_type=F32)
                s = s + w[:, h:h + 1] * jnp.maximum(y, 0.0)
            bits = lax.bitcast_convert_type(s, I32)
            key = jnp.where(bits >= 0, bits, bits ^ jnp.int32(0x7FFFFFFF))
            keys_sc[c] = jnp.where(col + start < lim, key, jnp.int32(INT_MIN))
            return carry

        lax.fori_loop(0, n_all, score_chunk, 0)

        def bit_pass(i, t):
            cand = t + lax.shift_left(jnp.int32(1), jnp.int32(31) - i)

            def count_chunk(c, acc):
                ind = jnp.where(keys_sc[c] >= cand, 1.0, 0.0)
                part = ind[:, 0:LANES]
                for j in range(1, tk // LANES):
                    part = part + ind[:, j * LANES:(j + 1) * LANES]
                return acc + part

            acc = lax.fori_loop(0, n_all, count_chunk, jnp.zeros((tq, LANES), F32))
            cnt = jnp.sum(acc, axis=1, keepdims=True)
            return jnp.where(cnt >= topk, cand, t)

        t = lax.fori_loop(0, 32, bit_pass, jnp.full((tq, 1), INT_MIN, I32))
        thr_sc[...] = jnp.maximum(t, jnp.int32(INT_MIN + 1))

    thr = thr_sc[...]
    qs = jnp.concatenate([qb_ref[:, r * LANES:(r + 1) * LANES] for r in range(rep)], axis=0)
    m_sc[...] = jnp.full(m_sc.shape, NEG, F32)
    l_sc[...] = jnp.zeros(l_sc.shape, F32)
    acc_sc[...] = jnp.zeros(acc_sc.shape, F32)

    def attend_chunk(c, carry):
        start = pl.multiple_of(c * tk, tk)
        s = lax.dot_general(qs, k_ref[pl.ds(start, tk), :], _NT, preferred_element_type=F32)
        sel = keys_sc[c] >= thr
        s = jnp.concatenate([jnp.where(sel, s[r * tq:(r + 1) * tq], NEG) for r in range(rep)], axis=0)
        _softmax_step(s, v_ref[pl.ds(start, tk), :], m_sc, l_sc, acc_sc)
        return carry

    lax.fori_loop(0, n_all, attend_chunk, 0)
    o = acc_sc[...] * (1.0 / l_sc[...])
    for r in range(rep):
        o_ref[:, r * LANES:(r + 1) * LANES] = o[r * tq:(r + 1) * tq].astype(o_ref.dtype)


def _dsa_attention(qb, qi, wi, ki, k, v, *, n_kv, rep, n_idx, tq, tk, q0, l_valid, topk):
    b, t, _ = qb.shape
    lp = k.shape[1]
    return pl.pallas_call(
        functools.partial(_dsa_body, tq=tq, tk=tk, q0=q0, l_valid=l_valid, n_idx=n_idx, rep=rep, topk=topk),
        grid=(b, t // tq, n_kv),
        in_specs=[pl.BlockSpec((None, tq, rep * LANES), lambda bi, qb_, g: (bi, qb_, g)),
                  pl.BlockSpec((None, tq, n_idx * LANES), lambda bi, qb_, g: (bi, qb_, 0)),
                  pl.BlockSpec((None, tq, LANES), lambda bi, qb_, g: (bi, qb_, 0)),
                  pl.BlockSpec((None, lp, LANES), lambda bi, qb_, g: (bi, 0, 0)),
                  pl.BlockSpec((None, lp, LANES), lambda bi, qb_, g: (bi, 0, g)),
                  pl.BlockSpec((None, lp, LANES), lambda bi, qb_, g: (bi, 0, g))],
        out_specs=pl.BlockSpec((None, tq, rep * LANES), lambda bi, qb_, g: (bi, qb_, g)),
        out_shape=jax.ShapeDtypeStruct((b, t, n_kv * rep * LANES), BF16),
        scratch_shapes=[pltpu.VMEM((lp // tk, tq, tk), I32), pltpu.VMEM((tq, 1), I32),
                        pltpu.VMEM((rep * tq, 1), F32), pltpu.VMEM((rep * tq, 1), F32),
                        pltpu.VMEM((rep * tq, LANES), F32)],
        name="dsa_attention",
        compiler_params=_cparams(("parallel", "arbitrary", "arbitrary")),
    )(qb, qi, wi, ki, k, v)


def _merge_body(ga_ref, gb_ref, oa_ref, ob_ref, o_ref):
    y = (ga_ref[...].astype(F32) * oa_ref[...].astype(F32)
         + gb_ref[...].astype(F32) * ob_ref[...].astype(F32))
    o_ref[...] = y.astype(o_ref.dtype)


def _merge(gates, out_a, out_b):
    m, d = out_a.shape
    tm = _pick(m, 256)
    return pl.pallas_call(
        _merge_body, grid=(m // tm,),
        in_specs=[pl.BlockSpec((tm, d), lambda i: (i, 0)), pl.BlockSpec((tm, d), lambda i: (i, 1)),
                  pl.BlockSpec((tm, d), lambda i: (i, 0)), pl.BlockSpec((tm, d), lambda i: (i, 0))],
        out_specs=pl.BlockSpec((tm, d), lambda i: (i, 0)),
        out_shape=jax.ShapeDtypeStruct((m, d), BF16), name="merge_gates",
        compiler_params=_cparams(("parallel",)),
    )(gates, gates, out_a, out_b)


def _pad_cols(w, width):
    return jnp.pad(w, ((0, 0), (0, width - w.shape[1])))


def _prep_weights(w_in, w_uq, w_uk, w_uv, w_out, w_up, w_down, dims):
    d, q_lora, c_lora, r_mla, n_heads, n_kv, n_idx = dims
    splits = (q_lora, c_lora, r_mla, n_heads * HEAD_DIM, n_kv * HEAD_DIM, n_kv * HEAD_DIM,
              n_idx * HEAD_DIM, HEAD_DIM, n_idx, d, d)
    offs = np.concatenate([[0], np.cumsum(splits)])
    sec = [w_in[:, int(offs[i]):int(offs[i + 1])] for i in range(len(splits))]
    small = jnp.concatenate([sec[1], _pad_cols(sec[2], LANES), sec[4], sec[5], sec[7],
                             _pad_cols(sec[8], LANES)], axis=1)
    small = _pad_cols(small, -(-small.shape[1] // 512) * 512)
    wq = w_uq.reshape(q_lora, n_heads, HEAD_DIM + r_mla)
    wq = jnp.pad(wq, ((0, 0), (0, 0), (0, 2 * LANES - HEAD_DIM - r_mla)))
    return dict(
        w_q=sec[0].astype(BF16), w_small=small.astype(BF16), w_qb=sec[3].astype(BF16),
        w_qi=sec[6].astype(BF16), w_gates=jnp.concatenate([sec[9], sec[10]], axis=1).astype(BF16),
        w_uq=wq.reshape(q_lora, n_heads * 2 * LANES).astype(BF16),
        w_uk=w_uk.reshape(c_lora, n_heads * HEAD_DIM).astype(BF16),
        w_uv=w_uv.reshape(c_lora, n_heads * HEAD_DIM).astype(BF16),
        w_out=w_out.astype(BF16), w_up=w_up.astype(BF16), w_down=w_down.astype(BF16))


def _layer(x, ada, past, q0, pw, g_norm1, g_q_lora, g_kv_lora, g_norm2, dims, tq_mla, tq_dsa, tk):
    d, q_lora, c_lora, r_mla, n_heads, n_kv, n_idx = dims
    b, t, _ = x.shape
    m = b * t
    xf = x.reshape(m, d)
    if ada.shape[0] == 1:
        mods = [ada[:, i * d:(i + 1) * d] for i in range(N_ADA)]
    else:
        mods = [jnp.repeat(ada[:, i * d:(i + 1) * d], t, axis=0) for i in range(N_ADA)]
    sh1, sc1, gt1, sh2, sc2, gt2 = mods
    per_row = ada.shape[0] != 1

    pos = q0 + jnp.arange(t, dtype=I32)
    tab_m = jnp.tile(_rope_table(pos, MLA_THETA, r_mla), (b, 1))
    tab_d = jnp.tile(_rope_table(pos, ROPE_THETA, DSA_ROT), (b, 1))

    h = _norm(xf, g_norm1, sc1, sh1, BF16)

    def tab_extra(tab, tm):
        return (tab, (tm, 3 * LANES), lambda i, j: (i, 0))

    tm = _pick(m, 512)
    qn = _proj("proj_qlat", h, pw["w_q"], _ep_rmsnorm, BF16, tn=q_lora,
                     extras=[(g_q_lora.reshape(1, q_lora), (1, q_lora), lambda i, j: (0, 0))])
    z_small = _proj("proj_small", h, pw["w_small"], _ep_plain, F32, tn=512)
    q_dsa = _proj("proj_qdsa", h, pw["w_qb"],
                        functools.partial(_ep_rope, half=DSA_ROT // 2, period=1, roped=(0,), scale=HEAD_DIM ** -0.5),
                        BF16, extras=[tab_extra(tab_d, tm)])
    q_idx = _proj("proj_qidx", h, pw["w_qi"],
                        functools.partial(_ep_rope, half=IDX_ROT // 2, period=1, roped=(0,), scale=1.0),
                        BF16, extras=[tab_extra(tab_d, tm)])
    gates = _proj("proj_gates", h, pw["w_gates"], _ep_sigmoid, BF16)

    (ckv, ckv_b, krope, krope_b, k_new, k_new_b, v_new, v_new_b, ki_new, ki_new_b, wi) = _finish(
        z_small, g_kv_lora, tab_m, tab_d, c_lora=c_lora, r_mla=r_mla, n_kv=n_kv, idx_scale=n_idx ** -0.5)

    kvw = n_kv * HEAD_DIM
    if past is None:
        l_valid = t
        lp = t
        ckv_all, krope_all, k_all, v_all, ki_all = ckv_b, krope_b, k_new_b, v_new_b, ki_new_b
    else:
        p_ckv, p_krope, p_k, p_v, p_ki = past
        l_valid = p_ckv.shape[1] + t
        lp = -(-l_valid // tk) * tk

        def cat(p, new, width):
            p = p.reshape(b, p.shape[1], -1).astype(BF16)
            if p.shape[2] < width:
                p = jnp.pad(p, ((0, 0), (0, 0), (0, width - p.shape[2])))
            full = jnp.concatenate([p, new.reshape(b, t, width)], axis=1)
            return jnp.pad(full, ((0, 0), (0, lp - l_valid), (0, 0))).reshape(b * lp, width)

        ckv_all = cat(p_ckv, ckv_b, c_lora)
        krope_all = cat(p_krope, krope_b, LANES)
        k_all = cat(p_k, k_new_b, kvw)
        v_all = cat(p_v, v_new_b, kvw)
        ki_all = cat(p_ki, ki_new_b, LANES)

    q_full = _proj("mla_q_up", qn, pw["w_uq"],
                         functools.partial(_ep_rope, half=r_mla // 2, period=2, roped=(1,),
                                           scale=(HEAD_DIM + r_mla) ** -0.5),
                         BF16, extras=[tab_extra(tab_m, tm)])
    tml = _pick(b * lp, 512)
    k_full = _proj("mla_k_up", ckv_all, pw["w_uk"], _ep_kfull, BF16, out_cols=2 * n_heads * HEAD_DIM,
                         extras=[(krope_all, (tml, LANES), lambda i, j: (i, 0))])
    v_mla = _proj("mla_v_up", ckv_all, pw["w_uv"], _ep_plain, BF16)
    out_a = _mla_attention(q_full.reshape(b, t, -1), k_full.reshape(b, lp, -1), v_mla.reshape(b, lp, -1),
                           n_heads=n_heads, tq=min(tq_mla, t), tk=min(tk, lp), q0=q0, l_valid=l_valid)

    out_b = _dsa_attention(q_dsa.reshape(b, t, -1), q_idx.reshape(b, t, -1), wi.reshape(b, t, LANES),
                           ki_all.reshape(b, lp, LANES), k_all.reshape(b, lp, kvw), v_all.reshape(b, lp, kvw),
                           n_kv=n_kv, rep=n_heads // n_kv, n_idx=n_idx, tq=min(tq_dsa, t), tk=min(tk, lp),
                           q0=q0, l_valid=l_valid, topk=min(TOPK_MAX, l_valid // 4))

    merged = _merge(gates, out_a.reshape(m, d), out_b.reshape(m, d))

    def gate_extra(gate, tm_, tn_):
        if per_row:
            return (gate, (tm_, tn_), lambda i, j: (i, j))
        return (gate, (1, tn_), lambda i, j: (0, j))

    tn = _pick(d, 1024)
    x1 = _proj("out_proj", merged, pw["w_out"], _ep_residual, F32,
                     extras=[(xf, (tm, tn), lambda i, j: (i, j)), gate_extra(gt1, tm, tn)])
    h2 = _norm(x1, g_norm2, sc2, sh2, BF16)
    u = _proj("mlp_up", h2, pw["w_up"], _ep_relu2, BF16)
    tm2 = _pick(m, 1024)
    x2 = _proj("mlp_down", u, pw["w_down"], _ep_residual, F32, tm=tm2, tk=2048,
               extras=[(x1, (tm2, tn), lambda i, j: (i, j)), gate_extra(gt2, tm2, tn)])
    rows = (ckv.reshape(b, t, c_lora), krope.reshape(b, t, r_mla), k_new.reshape(b, t, n_kv, HEAD_DIM),
            v_new.reshape(b, t, n_kv, HEAD_DIM), ki_new.reshape(b, t, HEAD_DIM))
    return x2, rows


def kernel(x_prompt, x_sample, c_prompt, c_sample, cache_mla_ckv, cache_mla_krope, cache_dsa_k, cache_dsa_v, cache_idx_k, w_ada, b_ada, g_norm1, w_in, g_q_lora, w_uq, g_kv_lora, w_uk, w_uv, w_out, g_norm2, w_up, w_down, g_final):
    depth = w_in.shape[0]
    bp, tp, d = x_prompt.shape
    bs, ts, _ = x_sample.shape
    n_heads = d // HEAD_DIM
    dims = (d, w_uq.shape[1], cache_mla_ckv.shape[-1], cache_mla_krope.shape[-1], n_heads,
            cache_dsa_k.shape[3], n_heads)
    past_len = cache_mla_ckv.shape[2]

    c_all = jnp.concatenate([c_prompt, c_sample], axis=0)
    n_c = c_all.shape[0]
    c_all = jnp.pad(c_all, ((0, -(-n_c // 16) * 16 - n_c), (0, 0)))

    xp, xs = x_prompt, x_sample
    rows_p, rows_s = [], []
    for l in range(depth):
        ada = _ada(c_all, w_ada[l], b_ada[l])
        pw = _prep_weights(w_in[l], w_uq[l], w_uk[l], w_uv[l], w_out[l], w_up[l], w_down[l], dims)
        norms = (g_norm1[l], g_q_lora[l], g_kv_lora[l], g_norm2[l])
        xp2, rp = _layer(xp, ada[0:bp], None, 0, pw, *norms, dims, tq_mla=TQ_MLA, tq_dsa=TQ_DSA, tk=TK_PROMPT)
        past = (cache_mla_ckv[l], cache_mla_krope[l], cache_dsa_k[l], cache_dsa_v[l], cache_idx_k[l])
        xs2, rs = _layer(xs, ada[bp:bp + bs], past, past_len, pw, *norms, dims, tq_mla=TQ_MLA, tq_dsa=TQ_DSA,
                         tk=TK_SAMPLE)
        xp, xs = xp2.reshape(bp, tp, d), xs2.reshape(bs, ts, d)
        rows_p.append(rp)
        rows_s.append(rs)

    y_prompt = _norm(xp.reshape(bp * tp, d), g_final, None, None, F32).reshape(bp, tp, d)
    y_sample = _norm(xs.reshape(bs * ts, d), g_final, None, None, F32).reshape(bs, ts, d)
    stack = lambda rows, i: jnp.stack([r[i] for r in rows])
    return (y_prompt, y_sample,
            stack(rows_p, 0), stack(rows_p, 1), stack(rows_p, 2), stack(rows_p, 3), stack(rows_p, 4),
            stack(rows_s, 0), stack(rows_s, 1), stack(rows_s, 2), stack(rows_s, 3), stack(rows_s, 4))
```

```python
import functools

import numpy as np
import jax
import jax.numpy as jnp
from jax import lax
from jax.experimental import pallas as pl
from jax.experimental.pallas import tpu as pltpu

F32 = jnp.float32
BF16 = jnp.bfloat16
I32 = jnp.int32

LANES = 128
CHUNK = 64
CHUNK_LOG2 = 6
EPS = 1e-6
NEG = -1e30
INT_MIN = -2 ** 31
MLA_THETA = 10000.0
ROPE_THETA = 500000.0
HEAD_DIM = 128
DSA_ROT = HEAD_DIM // 4
IDX_ROT = HEAD_DIM // 4
TOPK_MAX = 256
N_ADA = 6
VMEM_LIMIT = 56 * 1024 * 1024
TQ_MLA, TQ_DSA = 512, 128
MLA_HEADS_PER_STEP = 4
DSA_GROUPS_PER_PASS = 4
TK_PROMPT, TK_SAMPLE = 512, 384


def _cparams(sem):
    return pltpu.CompilerParams(dimension_semantics=sem, vmem_limit_bytes=VMEM_LIMIT)


def _pick(n, pref):
    if n <= pref:
        return n
    t = pref
    while n % t:
        t //= 2
    return t


def _rope_table(pos, theta, rot_dim):
    half = rot_dim // 2
    inv = theta ** (-jnp.arange(half, dtype=F32) / half)
    ang = pos.astype(F32)[:, None] * inv[None, :]
    cos, sin = jnp.cos(ang), jnp.sin(ang)
    n = pos.shape[0]
    ones = jnp.ones((n, LANES - rot_dim), F32)
    zeros = jnp.zeros((n, LANES - rot_dim), F32)
    zh = jnp.zeros((n, half), F32)
    a = jnp.concatenate([cos, cos, ones], axis=1)
    b = jnp.concatenate([-sin, zh, zeros], axis=1)
    c = jnp.concatenate([zh, sin, zeros], axis=1)
    return jnp.concatenate([a, b, c], axis=1)


def _rope_slab(x, tab, half):
    a = tab[:, 0:LANES]
    b = tab[:, LANES:2 * LANES]
    c = tab[:, 2 * LANES:3 * LANES]
    return x * a + pltpu.roll(x, LANES - half, 1) * b + pltpu.roll(x, half, 1) * c


def _mm_body(*refs, nk, n_extra, n_out, epilogue):
    a_ref, w_ref = refs[0], refs[1]
    extra = refs[2:2 + n_extra]
    outs = refs[2 + n_extra:2 + n_extra + n_out]
    a = a_ref[...].astype(BF16)
    w = w_ref[...].astype(BF16)
    part = jnp.dot(a, w, preferred_element_type=F32)
    if nk == 1:
        epilogue(part, extra, outs)
        return
    acc_ref = refs[2 + n_extra + n_out]
    k = pl.program_id(2)

    @pl.when(k == 0)
    def _():
        acc_ref[...] = part

    @pl.when(k > 0)
    def _():
        acc_ref[...] += part

    @pl.when(k == nk - 1)
    def _():
        epilogue(acc_ref[...], extra, outs)


def _matmul(a, w, *, tm, tn, tk, extras, outs, epilogue, name, order="nm"):
    m, kdim = a.shape
    n = w.shape[1]
    gi, gj, gk = m // tm, n // tn, kdim // tk
    if order == "nm":
        grid = (gj, gi, gk)
        wrap = lambda f: (lambda pj, pi, pk: f(pi, pj, pk))
    else:
        grid = (gi, gj, gk)
        wrap = lambda f: (lambda pi, pj, pk: f(pi, pj, pk))
    in_specs = [
        pl.BlockSpec((tm, tk), wrap(lambda i, j, k: (i, k))),
        pl.BlockSpec((tk, tn), wrap(lambda i, j, k: (k, j))),
    ]
    args = [a, w]
    for arr, blk, fn in extras:
        in_specs.append(pl.BlockSpec(blk, wrap(lambda i, j, k, fn=fn: fn(i, j))))
        args.append(arr)
    out_specs = [pl.BlockSpec(blk, wrap(lambda i, j, k, fn=fn: fn(i, j))) for _, blk, fn in outs]
    out_shape = [s for s, _, _ in outs]
    scratch = [pltpu.VMEM((tm, tn), F32)] if gk > 1 else []
    body = functools.partial(_mm_body, nk=gk, n_extra=len(extras), n_out=len(outs), epilogue=epilogue)
    res = pl.pallas_call(
        body, grid=grid, in_specs=in_specs, out_specs=out_specs, out_shape=out_shape,
        scratch_shapes=scratch, name=name,
        compiler_params=_cparams(("parallel", "parallel", "arbitrary")),
    )(*args)
    return res


def _ep_plain(acc, extra, outs):
    outs[0][...] = acc.astype(outs[0].dtype)


def _ep_rmsnorm(acc, extra, outs):
    g = extra[0][...]
    y = acc * lax.rsqrt(jnp.mean(acc * acc, axis=-1, keepdims=True) + EPS) * g
    outs[0][...] = y.astype(outs[0].dtype)


def _ep_sigmoid(acc, extra, outs):
    outs[0][...] = jax.nn.sigmoid(acc).astype(outs[0].dtype)


def _ep_relu2(acc, extra, outs):
    r = jnp.maximum(acc, 0.0)
    outs[0][...] = (r * r).astype(outs[0].dtype)


def _ep_residual(acc, extra, outs):
    x = extra[0][...]
    gate = extra[1][...]
    outs[0][...] = x + gate * acc


def _ep_rope(acc, extra, outs, *, half, period, roped, scale):
    tab = extra[0][...]
    o = outs[0]
    for s in range(acc.shape[1] // LANES):
        x = acc[:, s * LANES:(s + 1) * LANES]
        if (s % period) in roped:
            x = _rope_slab(x, tab, half)
        if scale != 1.0:
            x = x * scale
        o[:, s * LANES:(s + 1) * LANES] = x.astype(o.dtype)


def _ep_kfull(acc, extra, outs):
    kr = extra[0][...]
    o = outs[0]
    for h in range(acc.shape[1] // LANES):
        o[:, (2 * h) * LANES:(2 * h + 1) * LANES] = acc[:, h * LANES:(h + 1) * LANES].astype(o.dtype)
        o[:, (2 * h + 1) * LANES:(2 * h + 2) * LANES] = kr


TM, TN, TK = 512, 1024, 4096


def _proj(name, a, w, ep, out_dtype, *, tm=TM, tn=TN, tk=TK, extras=(), out_cols=None, order="nm"):
    m, k = a.shape
    n = w.shape[1]
    tm, tn, tk = _pick(m, tm), _pick(n, tn), _pick(k, tk)
    oc = n if out_cols is None else out_cols
    outs = [(jax.ShapeDtypeStruct((m, oc), out_dtype), (tm, tn * oc // n), lambda i, j: (i, j))]
    return _matmul(a, w, tm=tm, tn=tn, tk=tk, extras=list(extras), outs=outs, epilogue=ep, name=name,
                   order=order)[0]


def _ada_body(c_ref, w_ref, b_ref, o_ref):
    c = c_ref[...]
    a = (c * jax.nn.sigmoid(c)).astype(BF16)
    o_ref[...] = jnp.dot(a, w_ref[...].astype(BF16), preferred_element_type=F32) + b_ref[...]


def _ada(c_all, w_ada, b_ada):
    r, d = c_all.shape
    n = w_ada.shape[1]
    tn = _pick(n, 512)
    return pl.pallas_call(
        _ada_body, grid=(n // tn,),
        in_specs=[pl.BlockSpec((r, d), lambda j: (0, 0)),
                  pl.BlockSpec((d, tn), lambda j: (0, j)),
                  pl.BlockSpec((1, tn), lambda j: (0, j))],
        out_specs=pl.BlockSpec((r, tn), lambda j: (0, j)),
        out_shape=jax.ShapeDtypeStruct((r, n), F32), name="adaln",
        compiler_params=_cparams(("parallel",)),
    )(c_all, w_ada, b_ada.reshape(1, n))


def _norm_body(*refs, modulate):
    x = refs[0][...]
    g = refs[1][...]
    y = x * lax.rsqrt(jnp.mean(x * x, axis=-1, keepdims=True) + EPS) * g
    if modulate:
        y = y * (1.0 + refs[2][...]) + refs[3][...]
        o = refs[4]
    else:
        o = refs[2]
    o[...] = y.astype(o.dtype)


def _norm(x, g, sc, sh, out_dtype):
    m, d = x.shape
    tm = _pick(m, 256)
    in_specs = [pl.BlockSpec((tm, d), lambda i: (i, 0)), pl.BlockSpec((1, d), lambda i: (0, 0))]
    args = [x, g.reshape(1, d)]
    if sc is not None:
        per_row = sc.shape[0] == m
        blk = (tm, d) if per_row else (1, d)
        fn = (lambda i: (i, 0)) if per_row else (lambda i: (0, 0))
        in_specs += [pl.BlockSpec(blk, fn), pl.BlockSpec(blk, fn)]
        args += [sc, sh]
    return pl.pallas_call(
        functools.partial(_norm_body, modulate=sc is not None), grid=(m // tm,),
        in_specs=in_specs, out_specs=pl.BlockSpec((tm, d), lambda i: (i, 0)),
        out_shape=jax.ShapeDtypeStruct((m, d), out_dtype), name="rmsnorm",
        compiler_params=_cparams(("parallel",)),
    )(*args)


def _finish_body(z_ref, g_ref, tm_ref, td_ref,
                 ckv_o, ckvb_o, kr_o, krb_o, k_o, kb_o, v_o, vb_o, ki_o, kib_o, wi_o,
                 *, c_lora, r_mla, n_kv, idx_scale):
    tab_m = tm_ref[...]
    tab_d = td_ref[...]
    off = 0
    ckv = z_ref[:, off:off + c_lora]
    ckv = ckv * lax.rsqrt(jnp.mean(ckv * ckv, axis=-1, keepdims=True) + EPS) * g_ref[...]
    ckv_o[...] = ckv
    ckvb_o[...] = ckv.astype(BF16)
    off += c_lora
    kr = _rope_slab(z_ref[:, off:off + LANES], tab_m, r_mla // 2)
    kr_o[...] = kr[:, 0:r_mla]
    krb_o[...] = kr.astype(BF16)
    off += LANES
    for h in range(n_kv):
        kh = _rope_slab(z_ref[:, off + h * LANES:off + (h + 1) * LANES], tab_d, DSA_ROT // 2)
        k_o[:, h * LANES:(h + 1) * LANES] = kh
        kb_o[:, h * LANES:(h + 1) * LANES] = kh.astype(BF16)
    off += n_kv * LANES
    v = z_ref[:, off:off + n_kv * LANES]
    v_o[...] = v
    vb_o[...] = v.astype(BF16)
    off += n_kv * LANES
    ki = _rope_slab(z_ref[:, off:off + LANES], tab_d, IDX_ROT // 2)
    ki_o[...] = ki
    kib_o[...] = ki.astype(BF16)
    off += LANES
    wi_o[...] = z_ref[:, off:off + LANES] * idx_scale


def _finish(z, g_kv, tab_m, tab_d, *, c_lora, r_mla, n_kv, idx_scale):
    m, zc = z.shape
    tm = _pick(m, 256)
    kvw = n_kv * LANES
    row = lambda i: (i, 0)
    shapes = [(c_lora, F32), (c_lora, BF16), (r_mla, F32), (LANES, BF16), (kvw, F32), (kvw, BF16),
              (kvw, F32), (kvw, BF16), (LANES, F32), (LANES, BF16), (LANES, F32)]
    return pl.pallas_call(
        functools.partial(_finish_body, c_lora=c_lora, r_mla=r_mla, n_kv=n_kv, idx_scale=idx_scale),
        grid=(m // tm,),
        in_specs=[pl.BlockSpec((tm, zc), row), pl.BlockSpec((1, c_lora), lambda i: (0, 0)),
                  pl.BlockSpec((tm, 3 * LANES), row), pl.BlockSpec((tm, 3 * LANES), row)],
        out_specs=[pl.BlockSpec((tm, w), row) for w, _ in shapes],
        out_shape=[jax.ShapeDtypeStruct((m, w), dt) for w, dt in shapes], name="finish_small",
        compiler_params=_cparams(("parallel",)),
    )(z, g_kv.reshape(1, c_lora), tab_m, tab_d)


def _row_limits(q_first, tq, l_valid):
    rows = q_first + lax.broadcasted_iota(I32, (tq, 1), 0)
    chunk_end = lambda p: (lax.shift_right_logical(p, CHUNK_LOG2) + 1) * CHUNK
    lim = jnp.minimum(chunk_end(rows), l_valid)
    lim_min = jnp.minimum(chunk_end(q_first), l_valid)
    lim_max = jnp.minimum(chunk_end(q_first + tq - 1), l_valid)
    return lim, lim_min, lim_max


LOG2E = 1.4426950408889634


def _softmax_step(s, v, m_sc, l_sc, acc_sc):
    m_prev = m_sc[...]
    m_new = jnp.maximum(m_prev, jnp.max(s, axis=1, keepdims=True))
    p = jnp.exp2(s - jnp.concatenate([m_new] * (s.shape[1] // LANES), axis=1))
    alpha = jnp.exp2(m_prev - m_new)
    l_sc[...] = alpha * l_sc[...] + jnp.sum(p, axis=1, keepdims=True)
    acc_sc[...] = alpha * acc_sc[...] + jnp.dot(p.astype(BF16), v, preferred_element_type=F32)
    m_sc[...] = m_new


_NT = (((1,), (1,)), ((), ()))


def _mla_body(q_ref, k_ref, v_ref, o_ref, m_sc, l_sc, acc_sc, *, hb, tq, tk, q0, l_valid):
    q_first = q0 + pl.program_id(2) * tq
    lim, lim_min, lim_max = _row_limits(q_first, tq, l_valid)
    n_full = lim_min // tk
    n_all = (lim_max + tk - 1) // tk
    col = lax.broadcasted_iota(I32, (1, tk), 1)
    m_sc[...] = jnp.full(m_sc.shape, NEG, F32)
    l_sc[...] = jnp.zeros(l_sc.shape, F32)
    acc_sc[...] = jnp.zeros(acc_sc.shape, F32)

    def step(c, masked):
        start = pl.multiple_of(c * tk, tk)
        for j in range(hb):
            s = lax.dot_general(q_ref[:, j * 2 * LANES:(j + 1) * 2 * LANES],
                                k_ref[pl.ds(start, tk), j * 2 * LANES:(j + 1) * 2 * LANES], _NT,
                                preferred_element_type=F32)
            if masked:
                s = jnp.where(col + start < lim, s, NEG)
            _softmax_step(s, v_ref[pl.ds(start, tk), j * LANES:(j + 1) * LANES],
                          m_sc.at[j], l_sc.at[j], acc_sc.at[j])

    def full_step(c, carry):
        step(c, False)
        return carry

    def masked_step(c, carry):
        step(c, True)
        return carry

    lax.fori_loop(0, n_full, full_step, 0)
    lax.fori_loop(n_full, n_all, masked_step, 0)
    for j in range(hb):
        o_ref[:, j * LANES:(j + 1) * LANES] = (acc_sc[j] * (1.0 / l_sc[j])).astype(o_ref.dtype)


def _mla_attention(q_full, k_full, v, *, n_heads, hb, tq, tk, q0, l_valid):
    b, t, _ = q_full.shape
    lp = k_full.shape[1]
    kv_mode = dict(pipeline_mode=pl.Buffered(1)) if t // tq > 2 else {}
    stat = pltpu.VMEM((hb, tq, LANES), F32)
    return pl.pallas_call(
        functools.partial(_mla_body, hb=hb, tq=tq, tk=tk, q0=q0, l_valid=l_valid),
        grid=(b, n_heads // hb, t // tq),
        in_specs=[pl.BlockSpec((None, tq, hb * 2 * LANES), lambda bi, h, qi: (bi, qi, h)),
                  pl.BlockSpec((None, lp, hb * 2 * LANES), lambda bi, h, qi: (bi, 0, h), **kv_mode),
                  pl.BlockSpec((None, lp, hb * LANES), lambda bi, h, qi: (bi, 0, h), **kv_mode)],
        out_specs=pl.BlockSpec((None, tq, hb * LANES), lambda bi, h, qi: (bi, qi, h)),
        out_shape=jax.ShapeDtypeStruct((b, t, n_heads * LANES), BF16),
        scratch_shapes=[stat, stat, stat],
        name="mla_attention",
        compiler_params=_cparams(("parallel", "parallel", "arbitrary")),
    )(q_full, k_full, v)


def _dsa_body(qb_ref, qi_ref, wi_ref, ki_ref, k_ref, v_ref, o_ref,
              keys_sc, m_sc, l_sc, acc_sc, *, tq, tk, q0, l_valid, n_idx, n_kv, rep, topk):
    q_first = q0 + pl.program_id(1) * tq
    lim, _, lim_max = _row_limits(q_first, tq, l_valid)
    n_all = (lim_max + tk - 1) // tk
    col = lax.broadcasted_iota(I32, (1, tk), 1)
    w = wi_ref[...]

    def score_chunk(c, carry):
        start = pl.multiple_of(c * tk, tk)
        kic = ki_ref[pl.ds(start, tk), :]
        s = jnp.zeros((tq, tk), F32)
        for h in range(n_idx):
            y = lax.dot_general(qi_ref[:, h * LANES:(h + 1) * LANES], kic, _NT,
                                preferred_element_type=F32)
            s = s + w[:, h:h + 1] * jnp.maximum(y, 0.0)
        bits = lax.bitcast_convert_type(s, I32)
        key = jnp.where(bits >= 0, bits, bits ^ jnp.int32(0x7FFFFFFF))
        keys_sc[c] = jnp.where(col + start < lim, key, jnp.int32(INT_MIN))
        return carry

    lax.fori_loop(0, n_all, score_chunk, 0)

    def bit_pass(i, t):
        cand = t + lax.shift_left(jnp.int32(1), jnp.int32(31) - i)

        def count_chunk(c, acc):
            ind = jnp.where(keys_sc[c] >= cand, 1.0, 0.0)
            part = ind[:, 0:LANES]
            for j in range(1, tk // LANES):
                part = part + ind[:, j * LANES:(j + 1) * LANES]
            return acc + part

        acc = lax.fori_loop(0, n_all, count_chunk, jnp.zeros((tq, LANES), F32))
        cnt = jnp.sum(acc, axis=1, keepdims=True)
        return jnp.where(cnt >= topk, cand, t)

    t = lax.fori_loop(0, 32, bit_pass, jnp.full((tq, 1), INT_MIN, I32))
    thr = jnp.maximum(t, jnp.int32(INT_MIN + 1))

    gb = m_sc.shape[0]
    for g0 in range(0, n_kv, gb):
        m_sc[...] = jnp.full(m_sc.shape, NEG, F32)
        l_sc[...] = jnp.zeros(l_sc.shape, F32)
        acc_sc[...] = jnp.zeros(acc_sc.shape, F32)

        def attend_chunk(c, carry, g0=g0):
            start = pl.multiple_of(c * tk, tk)
            sel = keys_sc[c] >= thr
            for j in range(gb):
                g = g0 + j
                qs = jnp.concatenate([qb_ref[:, (g * rep + r) * LANES:(g * rep + r + 1) * LANES]
                                      for r in range(rep)], axis=0)
                s = lax.dot_general(qs, k_ref[pl.ds(start, tk), g * LANES:(g + 1) * LANES], _NT,
                                    preferred_element_type=F32)
                s = jnp.concatenate([jnp.where(sel, s[r * tq:(r + 1) * tq], NEG) for r in range(rep)], axis=0)
                _softmax_step(s, v_ref[pl.ds(start, tk), g * LANES:(g + 1) * LANES],
                              m_sc.at[j], l_sc.at[j], acc_sc.at[j])
            return carry

        lax.fori_loop(0, n_all, attend_chunk, 0)
        for j in range(gb):
            o = acc_sc[j] * (1.0 / l_sc[j])
            for r in range(rep):
                h = (g0 + j) * rep + r
                o_ref[:, h * LANES:(h + 1) * LANES] = o[r * tq:(r + 1) * tq].astype(o_ref.dtype)


def _dsa_attention(qb, qi, wi, ki, k, v, *, n_kv, rep, n_idx, tq, tk, q0, l_valid, topk):
    b, t, _ = qb.shape
    lp = k.shape[1]
    kvw = n_kv * LANES
    kv_mode = dict(pipeline_mode=pl.Buffered(1)) if t // tq > 2 else {}
    stat = pltpu.VMEM((min(DSA_GROUPS_PER_PASS, n_kv), rep * tq, LANES), F32)
    return pl.pallas_call(
        functools.partial(_dsa_body, tq=tq, tk=tk, q0=q0, l_valid=l_valid, n_idx=n_idx, n_kv=n_kv, rep=rep,
                          topk=topk),
        grid=(b, t // tq),
        in_specs=[pl.BlockSpec((None, tq, n_kv * rep * LANES), lambda bi, qb_: (bi, qb_, 0)),
                  pl.BlockSpec((None, tq, n_idx * LANES), lambda bi, qb_: (bi, qb_, 0)),
                  pl.BlockSpec((None, tq, LANES), lambda bi, qb_: (bi, qb_, 0)),
                  pl.BlockSpec((None, lp, LANES), lambda bi, qb_: (bi, 0, 0), **kv_mode),
                  pl.BlockSpec((None, lp, kvw), lambda bi, qb_: (bi, 0, 0), **kv_mode),
                  pl.BlockSpec((None, lp, kvw), lambda bi, qb_: (bi, 0, 0), **kv_mode)],
        out_specs=pl.BlockSpec((None, tq, n_kv * rep * LANES), lambda bi, qb_: (bi, qb_, 0)),
        out_shape=jax.ShapeDtypeStruct((b, t, n_kv * rep * LANES), BF16),
        scratch_shapes=[pltpu.VMEM((lp // tk, tq, tk), I32), stat, stat, stat],
        name="dsa_attention",
        compiler_params=_cparams(("parallel", "arbitrary")),
    )(qb, qi, wi, ki, k, v)


def _merge_body(ga_ref, gb_ref, oa_ref, ob_ref, o_ref):
    y = (ga_ref[...].astype(F32) * oa_ref[...].astype(F32)
         + gb_ref[...].astype(F32) * ob_ref[...].astype(F32))
    o_ref[...] = y.astype(o_ref.dtype)


def _merge(gates, out_a, out_b):
    m, d = out_a.shape
    tm = _pick(m, 256)
    return pl.pallas_call(
        _merge_body, grid=(m // tm,),
        in_specs=[pl.BlockSpec((tm, d), lambda i: (i, 0)), pl.BlockSpec((tm, d), lambda i: (i, 1)),
                  pl.BlockSpec((tm, d), lambda i: (i, 0)), pl.BlockSpec((tm, d), lambda i: (i, 0))],
        out_specs=pl.BlockSpec((tm, d), lambda i: (i, 0)),
        out_shape=jax.ShapeDtypeStruct((m, d), BF16), name="merge_gates",
        compiler_params=_cparams(("parallel",)),
    )(gates, gates, out_a, out_b)


def _pad_cols(w, width):
    return jnp.pad(w, ((0, 0), (0, width - w.shape[1])))


def _prep_weights(w_in, w_uq, w_uk, w_uv, w_out, w_up, w_down, dims):
    d, q_lora, c_lora, r_mla, n_heads, n_kv, n_idx = dims
    splits = (q_lora, c_lora, r_mla, n_heads * HEAD_DIM, n_kv * HEAD_DIM, n_kv * HEAD_DIM,
              n_idx * HEAD_DIM, HEAD_DIM, n_idx, d, d)
    offs = np.concatenate([[0], np.cumsum(splits)])
    sec = [w_in[:, int(offs[i]):int(offs[i + 1])] for i in range(len(splits))]
    small = jnp.concatenate([sec[1], _pad_cols(sec[2], LANES), sec[4], sec[5], sec[7],
                             _pad_cols(sec[8], LANES)], axis=1)
    small = _pad_cols(small, -(-small.shape[1] // 512) * 512)
    wq = w_uq.reshape(q_lora, n_heads, HEAD_DIM + r_mla)
    wq = jnp.pad(wq, ((0, 0), (0, 0), (0, 2 * LANES - HEAD_DIM - r_mla)))
    return dict(
        w_q=sec[0].astype(BF16), w_small=small.astype(BF16), w_qb=sec[3].astype(BF16),
        w_qi=sec[6].astype(BF16), w_gates=jnp.concatenate([sec[9], sec[10]], axis=1).astype(BF16),
        w_uq=wq.reshape(q_lora, n_heads * 2 * LANES).astype(BF16),
        w_uk=w_uk.reshape(c_lora, n_heads * HEAD_DIM).astype(BF16),
        w_uv=w_uv.reshape(c_lora, n_heads * HEAD_DIM).astype(BF16),
        w_out=w_out.astype(BF16), w_up=w_up.astype(BF16), w_down=w_down.astype(BF16))


def _layer(x, ada, past, q0, pw, g_norm1, g_q_lora, g_kv_lora, g_norm2, dims, tq_mla, tq_dsa, tk):
    d, q_lora, c_lora, r_mla, n_heads, n_kv, n_idx = dims
    b, t, _ = x.shape
    m = b * t
    xf = x.reshape(m, d)
    if ada.shape[0] == 1:
        mods = [ada[:, i * d:(i + 1) * d] for i in range(N_ADA)]
    else:
        mods = [jnp.repeat(ada[:, i * d:(i + 1) * d], t, axis=0) for i in range(N_ADA)]
    sh1, sc1, gt1, sh2, sc2, gt2 = mods
    per_row = ada.shape[0] != 1

    pos = q0 + jnp.arange(t, dtype=I32)
    tab_m = jnp.tile(_rope_table(pos, MLA_THETA, r_mla), (b, 1))
    tab_d = jnp.tile(_rope_table(pos, ROPE_THETA, DSA_ROT), (b, 1))

    h = _norm(xf, g_norm1, sc1, sh1, BF16)

    def tab_extra(tab, tm):
        return (tab, (tm, 3 * LANES), lambda i, j: (i, 0))

    tm = _pick(m, 512)
    qn = _proj("proj_qlat", h, pw["w_q"], _ep_rmsnorm, BF16, tn=q_lora,
                     extras=[(g_q_lora.reshape(1, q_lora), (1, q_lora), lambda i, j: (0, 0))])
    z_small = _proj("proj_small", h, pw["w_small"], _ep_plain, F32, tn=512)
    q_dsa = _proj("proj_qdsa", h, pw["w_qb"],
                        functools.partial(_ep_rope, half=DSA_ROT // 2, period=1, roped=(0,),
                                          scale=HEAD_DIM ** -0.5 * LOG2E),
                        BF16, extras=[tab_extra(tab_d, tm)])
    q_idx = _proj("proj_qidx", h, pw["w_qi"],
                        functools.partial(_ep_rope, half=IDX_ROT // 2, period=1, roped=(0,), scale=1.0),
                        BF16, extras=[tab_extra(tab_d, tm)])
    gates = _proj("proj_gates", h, pw["w_gates"], _ep_sigmoid, BF16)

    (ckv, ckv_b, krope, krope_b, k_new, k_new_b, v_new, v_new_b, ki_new, ki_new_b, wi) = _finish(
        z_small, g_kv_lora, tab_m, tab_d, c_lora=c_lora, r_mla=r_mla, n_kv=n_kv, idx_scale=n_idx ** -0.5)

    kvw = n_kv * HEAD_DIM
    if past is None:
        l_valid = t
        lp = t
        ckv_all, krope_all, k_all, v_all, ki_all = ckv_b, krope_b, k_new_b, v_new_b, ki_new_b
    else:
        p_ckv, p_krope, p_k, p_v, p_ki = past
        l_valid = p_ckv.shape[1] + t
        lp = -(-l_valid // tk) * tk

        def cat(p, new, width):
            p = p.reshape(b, p.shape[1], -1).astype(BF16)
            if p.shape[2] < width:
                p = jnp.pad(p, ((0, 0), (0, 0), (0, width - p.shape[2])))
            full = jnp.concatenate([p, new.reshape(b, t, width)], axis=1)
            return jnp.pad(full, ((0, 0), (0, lp - l_valid), (0, 0))).reshape(b * lp, width)

        ckv_all = cat(p_ckv, ckv_b, c_lora)
        krope_all = cat(p_krope, krope_b, LANES)
        k_all = cat(p_k, k_new_b, kvw)
        v_all = cat(p_v, v_new_b, kvw)
        ki_all = cat(p_ki, ki_new_b, LANES)

    q_full = _proj("mla_q_up", qn, pw["w_uq"],
                         functools.partial(_ep_rope, half=r_mla // 2, period=2, roped=(1,),
                                           scale=(HEAD_DIM + r_mla) ** -0.5 * LOG2E),
                         BF16, extras=[tab_extra(tab_m, tm)])
    tml = _pick(b * lp, 512)
    k_full = _proj("mla_k_up", ckv_all, pw["w_uk"], _ep_kfull, BF16, out_cols=2 * n_heads * HEAD_DIM,
                         extras=[(krope_all, (tml, LANES), lambda i, j: (i, 0))])
    v_mla = _proj("mla_v_up", ckv_all, pw["w_uv"], _ep_plain, BF16)
    out_a = _mla_attention(q_full.reshape(b, t, -1), k_full.reshape(b, lp, -1), v_mla.reshape(b, lp, -1),
                           n_heads=n_heads, hb=min(MLA_HEADS_PER_STEP, n_heads), tq=min(tq_mla, t),
                           tk=min(tk, lp), q0=q0, l_valid=l_valid)

    out_b = _dsa_attention(q_dsa.reshape(b, t, -1), q_idx.reshape(b, t, -1), wi.reshape(b, t, LANES),
                           ki_all.reshape(b, lp, LANES), k_all.reshape(b, lp, kvw), v_all.reshape(b, lp, kvw),
                           n_kv=n_kv, rep=n_heads // n_kv, n_idx=n_idx, tq=min(tq_dsa, t), tk=min(tk, lp),
                           q0=q0, l_valid=l_valid, topk=min(TOPK_MAX, l_valid // 4))

    merged = _merge(gates, out_a.reshape(m, d), out_b.reshape(m, d))

    def gate_extra(gate, tm_, tn_):
        if per_row:
            return (gate, (tm_, tn_), lambda i, j: (i, j))
        return (gate, (1, tn_), lambda i, j: (0, j))

    tn = _pick(d, 1024)
    x1 = _proj("out_proj", merged, pw["w_out"], _ep_residual, F32,
                     extras=[(xf, (tm, tn), lambda i, j: (i, j)), gate_extra(gt1, tm, tn)])
    h2 = _norm(x1, g_norm2, sc2, sh2, BF16)
    u = _proj("mlp_up", h2, pw["w_up"], _ep_relu2, BF16)
    tm2 = _pick(m, 1024)
    x2 = _proj("mlp_down", u, pw["w_down"], _ep_residual, F32, tm=tm2, tk=2048,
               extras=[(x1, (tm2, tn), lambda i, j: (i, j)), gate_extra(gt2, tm2, tn)])
    rows = (ckv.reshape(b, t, c_lora), krope.reshape(b, t, r_mla), k_new.reshape(b, t, n_kv, HEAD_DIM),
            v_new.reshape(b, t, n_kv, HEAD_DIM), ki_new.reshape(b, t, HEAD_DIM))
    return x2, rows


def kernel(x_prompt, x_sample, c_prompt, c_sample, cache_mla_ckv, cache_mla_krope, cache_dsa_k, cache_dsa_v, cache_idx_k, w_ada, b_ada, g_norm1, w_in, g_q_lora, w_uq, g_kv_lora, w_uk, w_uv, w_out, g_norm2, w_up, w_down, g_final):
    depth = w_in.shape[0]
    bp, tp, d = x_prompt.shape
    bs, ts, _ = x_sample.shape
    n_heads = d // HEAD_DIM
    dims = (d, w_uq.shape[1], cache_mla_ckv.shape[-1], cache_mla_krope.shape[-1], n_heads,
            cache_dsa_k.shape[3], n_heads)
    past_len = cache_mla_ckv.shape[2]

    c_all = jnp.concatenate([c_prompt, c_sample], axis=0)
    n_c = c_all.shape[0]
    c_all = jnp.pad(c_all, ((0, -(-n_c // 16) * 16 - n_c), (0, 0)))

    xp, xs = x_prompt, x_sample
    rows_p, rows_s = [], []
    for l in range(depth):
        ada = _ada(c_all, w_ada[l], b_ada[l])
        pw = _prep_weights(w_in[l], w_uq[l], w_uk[l], w_uv[l], w_out[l], w_up[l], w_down[l], dims)
        norms = (g_norm1[l], g_q_lora[l], g_kv_lora[l], g_norm2[l])
        xp2, rp = _layer(xp, ada[0:bp], None, 0, pw, *norms, dims, tq_mla=TQ_MLA, tq_dsa=TQ_DSA, tk=TK_PROMPT)
        past = (cache_mla_ckv[l], cache_mla_krope[l], cache_dsa_k[l], cache_dsa_v[l], cache_idx_k[l])
        xs2, rs = _layer(xs, ada[bp:bp + bs], past, past_len, pw, *norms, dims, tq_mla=TQ_MLA, tq_dsa=TQ_DSA,
                         tk=TK_SAMPLE)
        xp, xs = xp2.reshape(bp, tp, d), xs2.reshape(bs, ts, d)
        rows_p.append(rp)
        rows_s.append(rs)

    y_prompt = _norm(xp.reshape(bp * tp, d), g_final, None, None, F32).reshape(bp, tp, d)
    y_sample = _norm(xs.reshape(bs * ts, d), g_final, None, None, F32).reshape(bs, ts, d)
    stack = lambda rows, i: jnp.stack([r[i] for r in rows])
    return (y_prompt, y_sample,
            stack(rows_p, 0), stack(rows_p, 1), stack(rows_p, 2), stack(rows_p, 3), stack(rows_p, 4),
            stack(rows_s, 0), stack(rows_s, 1), stack(rows_s, 2), stack(rows_s, 3), stack(rows_s, 4))
```

```python
import functools

import numpy as np
import jax
import jax.numpy as jnp
from jax import lax
from jax.experimental import pallas as pl
from jax.experimental.pallas import tpu as pltpu

F32 = jnp.float32
BF16 = jnp.bfloat16
I32 = jnp.int32

LANES = 128
CHUNK = 64
CHUNK_LOG2 = 6
EPS = 1e-6
NEG = -1e30
INT_MIN = -2 ** 31
MLA_THETA = 10000.0
ROPE_THETA = 500000.0
HEAD_DIM = 128
DSA_ROT = HEAD_DIM // 4
IDX_ROT = HEAD_DIM // 4
TOPK_MAX = 256
N_ADA = 6
VMEM_LIMIT = 56 * 1024 * 1024
TQ_MLA, TQ_DSA = 512, 128
MLA_HEADS_PER_STEP = 4
DSA_GROUPS_PER_PASS = 4
TK_PROMPT, TK_SAMPLE = 512, 384


def _cparams(sem):
    return pltpu.CompilerParams(dimension_semantics=sem, vmem_limit_bytes=VMEM_LIMIT)


def _pick(n, pref):
    if n <= pref:
        return n
    t = pref
    while n % t:
        t //= 2
    return t


def _rope_table(pos, theta, rot_dim):
    half = rot_dim // 2
    inv = theta ** (-jnp.arange(half, dtype=F32) / half)
    ang = pos.astype(F32)[:, None] * inv[None, :]
    cos, sin = jnp.cos(ang), jnp.sin(ang)
    n = pos.shape[0]
    ones = jnp.ones((n, LANES - rot_dim), F32)
    zeros = jnp.zeros((n, LANES - rot_dim), F32)
    zh = jnp.zeros((n, half), F32)
    a = jnp.concatenate([cos, cos, ones], axis=1)
    b = jnp.concatenate([-sin, zh, zeros], axis=1)
    c = jnp.concatenate([zh, sin, zeros], axis=1)
    return jnp.concatenate([a, b, c], axis=1)


def _rope_slab(x, tab, half):
    a = tab[:, 0:LANES]
    b = tab[:, LANES:2 * LANES]
    c = tab[:, 2 * LANES:3 * LANES]
    return x * a + pltpu.roll(x, LANES - half, 1) * b + pltpu.roll(x, half, 1) * c


def _mm_body(*refs, nk, n_extra, n_out, epilogue):
    a_ref, w_ref = refs[0], refs[1]
    extra = refs[2:2 + n_extra]
    outs = refs[2 + n_extra:2 + n_extra + n_out]
    a = a_ref[...].astype(BF16)
    w = w_ref[...].astype(BF16)
    part = jnp.dot(a, w, preferred_element_type=F32)
    if nk == 1:
        epilogue(part, extra, outs)
        return
    acc_ref = refs[2 + n_extra + n_out]
    k = pl.program_id(2)

    @pl.when(k == 0)
    def _():
        acc_ref[...] = part

    @pl.when(k > 0)
    def _():
        acc_ref[...] += part

    @pl.when(k == nk - 1)
    def _():
        epilogue(acc_ref[...], extra, outs)


def _matmul(a, w, *, n, col0, tm, tn, tk, extras, outs, epilogue, name, order="nm"):
    m, kdim = a.shape
    assert col0 % tn == 0
    jb = col0 // tn
    gi, gj, gk = m // tm, n // tn, kdim // tk
    if order == "nm":
        grid = (gj, gi, gk)
        wrap = lambda f: (lambda pj, pi, pk: f(pi, pj, pk))
    else:
        grid = (gi, gj, gk)
        wrap = lambda f: (lambda pi, pj, pk: f(pi, pj, pk))
    in_specs = [
        pl.BlockSpec((tm, tk), wrap(lambda i, j, k: (i, k))),
        pl.BlockSpec((tk, tn), wrap(lambda i, j, k: (k, j + jb))),
    ]
    args = [a, w]
    for arr, blk, fn in extras:
        in_specs.append(pl.BlockSpec(blk, wrap(lambda i, j, k, fn=fn: fn(i, j))))
        args.append(arr)
    out_specs = [pl.BlockSpec(blk, wrap(lambda i, j, k, fn=fn: fn(i, j))) for _, blk, fn in outs]
    out_shape = [s for s, _, _ in outs]
    scratch = [pltpu.VMEM((tm, tn), F32)] if gk > 1 else []
    body = functools.partial(_mm_body, nk=gk, n_extra=len(extras), n_out=len(outs), epilogue=epilogue)
    res = pl.pallas_call(
        body, grid=grid, in_specs=in_specs, out_specs=out_specs, out_shape=out_shape,
        scratch_shapes=scratch, name=name,
        compiler_params=_cparams(("parallel", "parallel", "arbitrary")),
    )(*args)
    return res


def _ep_plain(acc, extra, outs):
    outs[0][...] = acc.astype(outs[0].dtype)


def _ep_rmsnorm(acc, extra, outs):
    g = extra[0][...]
    y = acc * lax.rsqrt(jnp.mean(acc * acc, axis=-1, keepdims=True) + EPS) * g
    outs[0][...] = y.astype(outs[0].dtype)


def _ep_sigmoid(acc, extra, outs):
    outs[0][...] = jax.nn.sigmoid(acc).astype(outs[0].dtype)


def _ep_relu2(acc, extra, outs):
    r = jnp.maximum(acc, 0.0)
    outs[0][...] = (r * r).astype(outs[0].dtype)


def _ep_residual(acc, extra, outs):
    x = extra[0][...]
    gate = extra[1][...]
    outs[0][...] = x + gate * acc


def _ep_rope(acc, extra, outs, *, half, period, roped, scale):
    tab = extra[0][...]
    o = outs[0]
    for s in range(acc.shape[1] // LANES):
        x = acc[:, s * LANES:(s + 1) * LANES]
        if (s % period) in roped:
            x = _rope_slab(x, tab, half)
        if scale != 1.0:
            x = x * scale
        o[:, s * LANES:(s + 1) * LANES] = x.astype(o.dtype)


def _ep_kfull(acc, extra, outs):
    kr = extra[0][...]
    o = outs[0]
    for h in range(acc.shape[1] // LANES):
        o[:, (2 * h) * LANES:(2 * h + 1) * LANES] = acc[:, h * LANES:(h + 1) * LANES].astype(o.dtype)
        o[:, (2 * h + 1) * LANES:(2 * h + 2) * LANES] = kr


TM, TN, TK = 512, 1024, 4096


def _proj(name, a, w, ep, out_dtype, *, cols=None, tm=TM, tn=TN, tk=TK, extras=(), out_cols=None, order="nm"):
    m, k = a.shape
    col0, n = (0, w.shape[1]) if cols is None else cols
    tm, tn, tk = _pick(m, tm), _pick(n, tn), _pick(k, tk)
    oc = n if out_cols is None else out_cols
    outs = [(jax.ShapeDtypeStruct((m, oc), out_dtype), (tm, tn * oc // n), lambda i, j: (i, j))]
    return _matmul(a, w, n=n, col0=col0, tm=tm, tn=tn, tk=tk, extras=list(extras), outs=outs, epilogue=ep,
                   name=name, order=order)[0]


def _ada_body(c_ref, w_ref, b_ref, o_ref):
    c = c_ref[...]
    a = (c * jax.nn.sigmoid(c)).astype(BF16)
    o_ref[...] = jnp.dot(a, w_ref[...].astype(BF16), preferred_element_type=F32) + b_ref[...]


def _ada(c_all, w_ada, b_ada):
    r, d = c_all.shape
    n = w_ada.shape[1]
    tn = _pick(n, 512)
    return pl.pallas_call(
        _ada_body, grid=(n // tn,),
        in_specs=[pl.BlockSpec((r, d), lambda j: (0, 0)),
                  pl.BlockSpec((d, tn), lambda j: (0, j)),
                  pl.BlockSpec((1, tn), lambda j: (0, j))],
        out_specs=pl.BlockSpec((r, tn), lambda j: (0, j)),
        out_shape=jax.ShapeDtypeStruct((r, n), F32), name="adaln",
        compiler_params=_cparams(("parallel",)),
    )(c_all, w_ada, b_ada.reshape(1, n))


def _norm_body(*refs, modulate):
    x = refs[0][...]
    g = refs[1][...]
    y = x * lax.rsqrt(jnp.mean(x * x, axis=-1, keepdims=True) + EPS) * g
    if modulate:
        y = y * (1.0 + refs[2][...]) + refs[3][...]
        o = refs[4]
    else:
        o = refs[2]
    o[...] = y.astype(o.dtype)


def _norm(x, g, sc, sh, out_dtype):
    m, d = x.shape
    tm = _pick(m, 256)
    in_specs = [pl.BlockSpec((tm, d), lambda i: (i, 0)), pl.BlockSpec((1, d), lambda i: (0, 0))]
    args = [x, g.reshape(1, d)]
    if sc is not None:
        per_row = sc.shape[0] == m
        blk = (tm, d) if per_row else (1, d)
        fn = (lambda i: (i, 0)) if per_row else (lambda i: (0, 0))
        in_specs += [pl.BlockSpec(blk, fn), pl.BlockSpec(blk, fn)]
        args += [sc, sh]
    return pl.pallas_call(
        functools.partial(_norm_body, modulate=sc is not None), grid=(m // tm,),
        in_specs=in_specs, out_specs=pl.BlockSpec((tm, d), lambda i: (i, 0)),
        out_shape=jax.ShapeDtypeStruct((m, d), out_dtype), name="rmsnorm",
        compiler_params=_cparams(("parallel",)),
    )(*args)


def _finish_body(z_ref, g_ref, tm_ref, td_ref,
                 ckv_o, ckvb_o, kr_o, krb_o, k_o, kb_o, v_o, vb_o, ki_o, kib_o, wi_o,
                 *, c_lora, r_mla, n_kv, idx_scale):
    tab_m = tm_ref[...]
    tab_d = td_ref[...]
    off = 0
    ckv = z_ref[:, off:off + c_lora]
    ckv = ckv * lax.rsqrt(jnp.mean(ckv * ckv, axis=-1, keepdims=True) + EPS) * g_ref[...]
    ckv_o[...] = ckv
    ckvb_o[...] = ckv.astype(BF16)
    off += c_lora
    kr = _rope_slab(z_ref[:, off:off + LANES], tab_m, r_mla // 2)
    kr_o[...] = kr[:, 0:r_mla]
    krb_o[...] = kr.astype(BF16)
    off += LANES
    for h in range(n_kv):
        kh = _rope_slab(z_ref[:, off + h * LANES:off + (h + 1) * LANES], tab_d, DSA_ROT // 2)
        k_o[:, h * LANES:(h + 1) * LANES] = kh
        kb_o[:, h * LANES:(h + 1) * LANES] = kh.astype(BF16)
    off += n_kv * LANES
    v = z_ref[:, off:off + n_kv * LANES]
    v_o[...] = v
    vb_o[...] = v.astype(BF16)
    off += n_kv * LANES
    ki = _rope_slab(z_ref[:, off:off + LANES], tab_d, IDX_ROT // 2)
    ki_o[...] = ki
    kib_o[...] = ki.astype(BF16)
    off += LANES
    wi_o[...] = z_ref[:, off:off + LANES] * idx_scale


def _finish(z, g_kv, tab_m, tab_d, *, c_lora, r_mla, n_kv, idx_scale):
    m, zc = z.shape
    tm = _pick(m, 256)
    kvw = n_kv * LANES
    row = lambda i: (i, 0)
    shapes = [(c_lora, F32), (c_lora, BF16), (r_mla, F32), (LANES, BF16), (kvw, F32), (kvw, BF16),
              (kvw, F32), (kvw, BF16), (LANES, F32), (LANES, BF16), (LANES, F32)]
    return pl.pallas_call(
        functools.partial(_finish_body, c_lora=c_lora, r_mla=r_mla, n_kv=n_kv, idx_scale=idx_scale),
        grid=(m // tm,),
        in_specs=[pl.BlockSpec((tm, zc), row), pl.BlockSpec((1, c_lora), lambda i: (0, 0)),
                  pl.BlockSpec((tm, 3 * LANES), row), pl.BlockSpec((tm, 3 * LANES), row)],
        out_specs=[pl.BlockSpec((tm, w), row) for w, _ in shapes],
        out_shape=[jax.ShapeDtypeStruct((m, w), dt) for w, dt in shapes], name="finish_small",
        compiler_params=_cparams(("parallel",)),
    )(z, g_kv.reshape(1, c_lora), tab_m, tab_d)


def _row_limits(q_first, tq, l_valid):
    rows = q_first + lax.broadcasted_iota(I32, (tq, 1), 0)
    chunk_end = lambda p: (lax.shift_right_logical(p, CHUNK_LOG2) + 1) * CHUNK
    lim = jnp.minimum(chunk_end(rows), l_valid)
    lim_min = jnp.minimum(chunk_end(q_first), l_valid)
    lim_max = jnp.minimum(chunk_end(q_first + tq - 1), l_valid)
    return lim, lim_min, lim_max


LOG2E = 1.4426950408889634


def _lane_tile(x, width):
    return x if width == LANES else jnp.concatenate([x] * (width // LANES), axis=1)


def _softmax_update(s, v, m_prev, l_prev, acc_prev):
    m_new = jnp.maximum(m_prev, jnp.max(s, axis=1, keepdims=True))
    p = jnp.exp2(s - _lane_tile(m_new, s.shape[1]))
    alpha = jnp.exp2(m_prev - m_new)
    l_new = alpha * l_prev + jnp.sum(p, axis=1, keepdims=True)
    acc_new = _lane_tile(alpha, acc_prev.shape[1]) * acc_prev + jnp.dot(p.astype(BF16), v,
                                                                        preferred_element_type=F32)
    return m_new, l_new, acc_new


def _softmax_step(s, v, m_sc, l_sc, acc_sc):
    m_sc[...], l_sc[...], acc_sc[...] = _softmax_update(s, v, m_sc[...], l_sc[...], acc_sc[...])


_NT = (((1,), (1,)), ((), ()))


def _mla_body(q_ref, k_ref, v_ref, o_ref, m_sc, l_sc, acc_sc, s_sc, *, hb, tq, tk, q0, l_valid):
    q_first = q0 + pl.program_id(2) * tq
    lim, lim_min, lim_max = _row_limits(q_first, tq, l_valid)
    n_full = lim_min // tk
    n_all = (lim_max + tk - 1) // tk
    col = lax.broadcasted_iota(I32, (1, tk), 1)
    m_sc[...] = jnp.full(m_sc.shape, NEG, F32)
    l_sc[...] = jnp.zeros(l_sc.shape, F32)
    acc_sc[...] = jnp.zeros(acc_sc.shape, F32)

    def scores(c, j):
        start = pl.multiple_of(c * tk, tk)
        return lax.dot_general(q_ref[:, j * 2 * LANES:(j + 1) * 2 * LANES],
                               k_ref[pl.ds(start, tk), j * 2 * LANES:(j + 1) * 2 * LANES], _NT,
                               preferred_element_type=F32)

    for j in range(hb):
        s_sc[j] = scores(0, j)

    def step(c, masked):
        start = pl.multiple_of(c * tk, tk)
        nxt = jnp.minimum(c + 1, n_all - 1)
        for j in range(hb):
            s = s_sc[j]
            if masked:
                s = jnp.where(col + start < lim, s, NEG)
            _softmax_step(s, v_ref[pl.ds(start, tk), j * LANES:(j + 1) * LANES],
                          m_sc.at[j], l_sc.at[j], acc_sc.at[j])
            s_sc[j] = scores(nxt, j)

    def full_step(c, carry):
        step(c, False)
        return carry

    def masked_step(c, carry):
        step(c, True)
        return carry

    lax.fori_loop(0, n_full, full_step, 0)
    lax.fori_loop(n_full, n_all, masked_step, 0)
    for j in range(hb):
        o_ref[:, j * LANES:(j + 1) * LANES] = (acc_sc[j] * (1.0 / l_sc[j])).astype(o_ref.dtype)


def _mla_attention(q_full, k_full, v, *, n_heads, hb, tq, tk, q0, l_valid):
    b, t, _ = q_full.shape
    lp = k_full.shape[1]
    kv_mode = dict(pipeline_mode=pl.Buffered(1)) if t // tq > 2 else {}
    stat = pltpu.VMEM((hb, tq, LANES), F32)
    return pl.pallas_call(
        functools.partial(_mla_body, hb=hb, tq=tq, tk=tk, q0=q0, l_valid=l_valid),
        grid=(b, n_heads // hb, t // tq),
        in_specs=[pl.BlockSpec((None, tq, hb * 2 * LANES), lambda bi, h, qi: (bi, qi, h)),
                  pl.BlockSpec((None, lp, hb * 2 * LANES), lambda bi, h, qi: (bi, 0, h), **kv_mode),
                  pl.BlockSpec((None, lp, hb * LANES), lambda bi, h, qi: (bi, 0, h), **kv_mode)],
        out_specs=pl.BlockSpec((None, tq, hb * LANES), lambda bi, h, qi: (bi, qi, h)),
        out_shape=jax.ShapeDtypeStruct((b, t, n_heads * LANES), BF16),
        scratch_shapes=[stat, stat, stat, pltpu.VMEM((hb, tq, tk), F32)],
        name="mla_attention",
        compiler_params=_cparams(("parallel", "parallel", "arbitrary")),
    )(q_full, k_full, v)


def _absorb_q_body(q_ref, w_ref, o_ref, *, c_lora):
    lat = lax.dot_general(q_ref[:, 0:LANES], w_ref[...], _NT, preferred_element_type=F32)
    o_ref[:, 0:c_lora] = lat.astype(o_ref.dtype)
    o_ref[:, c_lora:c_lora + LANES] = q_ref[:, LANES:2 * LANES]


def _absorb_q(q_full, w_uk, *, n_heads, c_lora):
    m = q_full.shape[0]
    return pl.pallas_call(
        functools.partial(_absorb_q_body, c_lora=c_lora), grid=(n_heads,),
        in_specs=[pl.BlockSpec((m, 2 * LANES), lambda h: (0, h)), pl.BlockSpec((c_lora, LANES), lambda h: (0, h))],
        out_specs=pl.BlockSpec((m, c_lora + LANES), lambda h: (0, h)),
        out_shape=jax.ShapeDtypeStruct((m, n_heads * (c_lora + LANES)), BF16), name="mla_absorb_q",
        compiler_params=_cparams(("parallel",)),
    )(q_full, w_uk)


def _absorb_o_body(o_lat_ref, w_ref, o_ref):
    o_ref[...] = jnp.dot(o_lat_ref[...], w_ref[...], preferred_element_type=F32).astype(o_ref.dtype)


def _absorb_o(o_lat, w_uv, *, n_heads, c_lora):
    m = o_lat.shape[0]
    return pl.pallas_call(
        _absorb_o_body, grid=(n_heads,),
        in_specs=[pl.BlockSpec((m, c_lora), lambda h: (0, h)), pl.BlockSpec((c_lora, LANES), lambda h: (0, h))],
        out_specs=pl.BlockSpec((m, LANES), lambda h: (0, h)),
        out_shape=jax.ShapeDtypeStruct((m, n_heads * LANES), BF16), name="mla_absorb_o",
        compiler_params=_cparams(("parallel",)),
    )(o_lat, w_uv)


def _mla_abs_body(q_ref, kc_ref, o_ref, *, bb, t, n_heads, c_lora, tk, q0, l_valid):
    rows = t * n_heads
    tok = lax.broadcasted_iota(I32, (t, n_heads, 1), 0).reshape(rows, 1)
    lim = jnp.minimum((lax.shift_right_logical(q0 + tok, CHUNK_LOG2) + 1) * CHUNK, l_valid)
    lim_min = min((q0 // CHUNK + 1) * CHUNK, l_valid)
    lim_max = min(((q0 + t - 1) // CHUNK + 1) * CHUNK, l_valid)
    col = lax.broadcasted_iota(I32, (1, tk), 1)
    for j in range(bb):
        q = q_ref[j]
        m = jnp.full((rows, LANES), NEG, F32)
        l = jnp.zeros((rows, LANES), F32)
        acc = jnp.zeros((rows, c_lora), F32)
        for c in range(-(-lim_max // tk)):
            kc = kc_ref[j, c * tk:(c + 1) * tk, :]
            s = lax.dot_general(q, kc, _NT, preferred_element_type=F32)
            if (c + 1) * tk > lim_min:
                s = jnp.where(col + c * tk < lim, s, NEG)
            m, l, acc = _softmax_update(s, kc[:, 0:c_lora], m, l, acc)
        o_ref[j] = (acc * _lane_tile(1.0 / l, c_lora)).astype(o_ref.dtype)


def _mla_abs_attention(q_abs, kc, *, t, n_heads, c_lora, tk, q0, l_valid):
    b, rows, dk = q_abs.shape
    lp = kc.shape[1]
    bb = _pick(b, 4)
    return pl.pallas_call(
        functools.partial(_mla_abs_body, bb=bb, t=t, n_heads=n_heads, c_lora=c_lora, tk=tk, q0=q0, l_valid=l_valid),
        grid=(b // bb,),
        in_specs=[pl.BlockSpec((bb, rows, dk), lambda i: (i, 0, 0)), pl.BlockSpec((bb, lp, dk), lambda i: (i, 0, 0))],
        out_specs=pl.BlockSpec((bb, rows, c_lora), lambda i: (i, 0, 0)),
        out_shape=jax.ShapeDtypeStruct((b, rows, c_lora), BF16), name="mla_absorbed_attention",
        compiler_params=_cparams(("parallel",)),
    )(q_abs, kc)


def _dsa_body(qb_ref, qi_ref, wi_ref, ki_ref, k_ref, v_ref, o_ref,
              keys_sc, m_sc, l_sc, acc_sc, *, tq, tk, q0, l_valid, n_idx, n_kv, rep, topk):
    q_first = q0 + pl.program_id(1) * tq
    lim, _, lim_max = _row_limits(q_first, tq, l_valid)
    n_all = (lim_max + tk - 1) // tk
    col = lax.broadcasted_iota(I32, (1, tk), 1)
    w = wi_ref[...]

    def score_chunk(c, carry):
        start = pl.multiple_of(c * tk, tk)
        kic = ki_ref[pl.ds(start, tk), :]
        s = jnp.zeros((tq, tk), F32)
        for h in range(n_idx):
            y = lax.dot_general(qi_ref[:, h * LANES:(h + 1) * LANES], kic, _NT,
                                preferred_element_type=F32)
            s = s + w[:, h:h + 1] * jnp.maximum(y, 0.0)
        bits = lax.bitcast_convert_type(s, I32)
        key = jnp.where(bits >= 0, bits, bits ^ jnp.int32(0x7FFFFFFF))
        keys_sc[c] = jnp.where(col + start < lim, key, jnp.int32(INT_MIN))
        return carry

    lax.fori_loop(0, n_all, score_chunk, 0)

    def bit_pass(i, t):
        cand = t + lax.shift_left(jnp.int32(1), jnp.int32(31) - i)

        def count_chunk(c, acc):
            ind = jnp.where(keys_sc[c] >= cand, 1.0, 0.0)
            part = ind[:, 0:LANES]
            for j in range(1, tk // LANES):
                part = part + ind[:, j * LANES:(j + 1) * LANES]
            return acc + part

        acc = lax.fori_loop(0, n_all, count_chunk, jnp.zeros((tq, LANES), F32))
        cnt = jnp.sum(acc, axis=1, keepdims=True)
        return jnp.where(cnt >= topk, cand, t)

    t = lax.fori_loop(0, 32, bit_pass, jnp.full((tq, 1), INT_MIN, I32))
    thr = jnp.maximum(t, jnp.int32(INT_MIN + 1))

    gb = m_sc.shape[0]
    for g0 in range(0, n_kv, gb):
        m_sc[...] = jnp.full(m_sc.shape, NEG, F32)
        l_sc[...] = jnp.zeros(l_sc.shape, F32)
        acc_sc[...] = jnp.zeros(acc_sc.shape, F32)

        def attend_chunk(c, carry, g0=g0):
            start = pl.multiple_of(c * tk, tk)
            sel = keys_sc[c] >= thr
            for j in range(gb):
                g = g0 + j
                qs = jnp.concatenate([qb_ref[:, (g * rep + r) * LANES:(g * rep + r + 1) * LANES]
                                      for r in range(rep)], axis=0)
                s = lax.dot_general(qs, k_ref[pl.ds(start, tk), g * LANES:(g + 1) * LANES], _NT,
                                    preferred_element_type=F32)
                s = jnp.concatenate([jnp.where(sel, s[r * tq:(r + 1) * tq], NEG) for r in range(rep)], axis=0)
                _softmax_step(s, v_ref[pl.ds(start, tk), g * LANES:(g + 1) * LANES],
                              m_sc.at[j], l_sc.at[j], acc_sc.at[j])
            return carry

        lax.fori_loop(0, n_all, attend_chunk, 0)
        for j in range(gb):
            o = acc_sc[j] * (1.0 / l_sc[j])
            for r in range(rep):
                h = (g0 + j) * rep + r
                o_ref[:, h * LANES:(h + 1) * LANES] = o[r * tq:(r + 1) * tq].astype(o_ref.dtype)


def _dsa_attention(qb, qi, wi, ki, k, v, *, n_kv, rep, n_idx, tq, tk, q0, l_valid, topk):
    b, t, _ = qb.shape
    lp = k.shape[1]
    kvw = n_kv * LANES
    kv_mode = dict(pipeline_mode=pl.Buffered(1)) if t // tq > 2 else {}
    stat = pltpu.VMEM((min(DSA_GROUPS_PER_PASS, n_kv), rep * tq, LANES), F32)
    return pl.pallas_call(
        functools.partial(_dsa_body, tq=tq, tk=tk, q0=q0, l_valid=l_valid, n_idx=n_idx, n_kv=n_kv, rep=rep,
                          topk=topk),
        grid=(b, t // tq),
        in_specs=[pl.BlockSpec((None, tq, n_kv * rep * LANES), lambda bi, qb_: (bi, qb_, 0)),
                  pl.BlockSpec((None, tq, n_idx * LANES), lambda bi, qb_: (bi, qb_, 0)),
                  pl.BlockSpec((None, tq, LANES), lambda bi, qb_: (bi, qb_, 0)),
                  pl.BlockSpec((None, lp, LANES), lambda bi, qb_: (bi, 0, 0), **kv_mode),
                  pl.BlockSpec((None, lp, kvw), lambda bi, qb_: (bi, 0, 0), **kv_mode),
                  pl.BlockSpec((None, lp, kvw), lambda bi, qb_: (bi, 0, 0), **kv_mode)],
        out_specs=pl.BlockSpec((None, tq, n_kv * rep * LANES), lambda bi, qb_: (bi, qb_, 0)),
        out_shape=jax.ShapeDtypeStruct((b, t, n_kv * rep * LANES), BF16),
        scratch_shapes=[pltpu.VMEM((lp // tk, tq, tk), I32), stat, stat, stat],
        name="dsa_attention",
        compiler_params=_cparams(("parallel", "arbitrary")),
    )(qb, qi, wi, ki, k, v)


def _merge_body(ga_ref, gb_ref, oa_ref, ob_ref, o_ref):
    y = (ga_ref[...].astype(F32) * oa_ref[...].astype(F32)
         + gb_ref[...].astype(F32) * ob_ref[...].astype(F32))
    o_ref[...] = y.astype(o_ref.dtype)


def _merge(gates, out_a, out_b):
    m, d = out_a.shape
    tm = _pick(m, 256)
    return pl.pallas_call(
        _merge_body, grid=(m // tm,),
        in_specs=[pl.BlockSpec((tm, d), lambda i: (i, 0)), pl.BlockSpec((tm, d), lambda i: (i, 1)),
                  pl.BlockSpec((tm, d), lambda i: (i, 0)), pl.BlockSpec((tm, d), lambda i: (i, 0))],
        out_specs=pl.BlockSpec((tm, d), lambda i: (i, 0)),
        out_shape=jax.ShapeDtypeStruct((m, d), BF16), name="merge_gates",
        compiler_params=_cparams(("parallel",)),
    )(gates, gates, out_a, out_b)


def _pad_cols(w, width):
    return jnp.pad(w, ((0, 0), (0, width - w.shape[1])))


def _prep_weights(w_in, w_uq, w_uk, w_uv, w_out, w_up, w_down, dims):
    d, q_lora, c_lora, r_mla, n_heads, n_kv, n_idx = dims
    splits = (q_lora, c_lora, r_mla, n_heads * HEAD_DIM, n_kv * HEAD_DIM, n_kv * HEAD_DIM,
              n_idx * HEAD_DIM, HEAD_DIM, n_idx, d, d)
    offs = np.concatenate([[0], np.cumsum(splits)])
    sec = [w_in[:, int(offs[i]):int(offs[i + 1])] for i in range(len(splits))]
    small = jnp.concatenate([sec[1], _pad_cols(sec[2], LANES), sec[4], sec[5], sec[7],
                             _pad_cols(sec[8], LANES)], axis=1)
    small = _pad_cols(small, -(-small.shape[1] // 512) * 512)
    parts = [("q", sec[0]), ("small", small), ("qb", sec[3]), ("qi", sec[6]),
             ("gates", jnp.concatenate([sec[9], sec[10]], axis=1))]
    windows, off = {}, 0
    for i, (key, p) in enumerate(parts):
        windows[key] = (off, p.shape[1])
        parts[i] = (key, _pad_cols(p, -(-p.shape[1] // TN) * TN))
        off += parts[i][1].shape[1]
    wq = w_uq.reshape(q_lora, n_heads, HEAD_DIM + r_mla)
    wq = jnp.pad(wq, ((0, 0), (0, 0), (0, 2 * LANES - HEAD_DIM - r_mla)))
    return dict(
        w_in=jnp.concatenate([p for _, p in parts], axis=1).astype(BF16), win=windows,
        w_uq=wq.reshape(q_lora, n_heads * 2 * LANES).astype(BF16),
        w_uk=w_uk.reshape(c_lora, n_heads * HEAD_DIM).astype(BF16),
        w_uv=w_uv.reshape(c_lora, n_heads * HEAD_DIM).astype(BF16),
        w_out=w_out.astype(BF16), w_up=w_up.astype(BF16), w_down=w_down.astype(BF16))


def _layer(x, ada, past, q0, pw, g_norm1, g_q_lora, g_kv_lora, g_norm2, dims, tq_mla, tq_dsa, tk):
    d, q_lora, c_lora, r_mla, n_heads, n_kv, n_idx = dims
    b, t, _ = x.shape
    m = b * t
    xf = x.reshape(m, d)
    if ada.shape[0] == 1:
        mods = [ada[:, i * d:(i + 1) * d] for i in range(N_ADA)]
    else:
        mods = [jnp.repeat(ada[:, i * d:(i + 1) * d], t, axis=0) for i in range(N_ADA)]
    sh1, sc1, gt1, sh2, sc2, gt2 = mods
    per_row = ada.shape[0] != 1

    pos = q0 + jnp.arange(t, dtype=I32)
    tab_m = jnp.tile(_rope_table(pos, MLA_THETA, r_mla), (b, 1))
    tab_d = jnp.tile(_rope_table(pos, ROPE_THETA, DSA_ROT), (b, 1))

    h = _norm(xf, g_norm1, sc1, sh1, BF16)

    def tab_extra(tab, tm):
        return (tab, (tm, 3 * LANES), lambda i, j: (i, 0))

    tm = _pick(m, 512)
    qn = _proj("proj_qlat", h, pw["w_in"], _ep_rmsnorm, BF16, cols=pw["win"]["q"], tn=q_lora,
                     extras=[(g_q_lora.reshape(1, q_lora), (1, q_lora), lambda i, j: (0, 0))])
    z_small = _proj("proj_small", h, pw["w_in"], _ep_plain, F32, cols=pw["win"]["small"], tn=512)
    q_dsa = _proj("proj_qdsa", h, pw["w_in"],
                  functools.partial(_ep_rope, half=DSA_ROT // 2, period=1, roped=(0,),
                                    scale=HEAD_DIM ** -0.5 * LOG2E),
                  BF16, cols=pw["win"]["qb"], extras=[tab_extra(tab_d, tm)])
    q_idx = _proj("proj_qidx", h, pw["w_in"],
                  functools.partial(_ep_rope, half=IDX_ROT // 2, period=1, roped=(0,), scale=1.0),
                  BF16, cols=pw["win"]["qi"], extras=[tab_extra(tab_d, tm)])
    gates = _proj("proj_gates", h, pw["w_in"], _ep_sigmoid, BF16, cols=pw["win"]["gates"])

    (ckv, ckv_b, krope, krope_b, k_new, k_new_b, v_new, v_new_b, ki_new, ki_new_b, wi) = _finish(
        z_small, g_kv_lora, tab_m, tab_d, c_lora=c_lora, r_mla=r_mla, n_kv=n_kv, idx_scale=n_idx ** -0.5)

    kvw = n_kv * HEAD_DIM
    if past is None:
        l_valid = t
        lp = t
        ckv_all, krope_all, k_all, v_all, ki_all = ckv_b, krope_b, k_new_b, v_new_b, ki_new_b
    else:
        p_ckv, p_krope, p_k, p_v, p_ki = past
        l_valid = p_ckv.shape[1] + t
        lp = -(-l_valid // tk) * tk

        def cat(p, new, width):
            p = p.reshape(b, p.shape[1], -1).astype(BF16)
            if p.shape[2] < width:
                p = jnp.pad(p, ((0, 0), (0, 0), (0, width - p.shape[2])))
            full = jnp.concatenate([p, new.reshape(b, t, width)], axis=1)
            return jnp.pad(full, ((0, 0), (0, lp - l_valid), (0, 0))).reshape(b * lp, width)

        ckv_all = cat(p_ckv, ckv_b, c_lora)
        krope_all = cat(p_krope, krope_b, LANES)
        k_all = cat(p_k, k_new_b, kvw)
        v_all = cat(p_v, v_new_b, kvw)
        ki_all = cat(p_ki, ki_new_b, LANES)

    q_full = _proj("mla_q_up", qn, pw["w_uq"],
                         functools.partial(_ep_rope, half=r_mla // 2, period=2, roped=(1,),
                                           scale=(HEAD_DIM + r_mla) ** -0.5 * LOG2E),
                         BF16, extras=[tab_extra(tab_m, tm)])
    if past is not None and t * n_heads <= 1024 and n_heads % 8 == 0:
        kc = jnp.concatenate([ckv_all.reshape(b, lp, c_lora), krope_all.reshape(b, lp, LANES)], axis=2)
        q_abs = _absorb_q(q_full, pw["w_uk"], n_heads=n_heads, c_lora=c_lora)
        o_lat = _mla_abs_attention(q_abs.reshape(b, t * n_heads, c_lora + LANES), kc, t=t, n_heads=n_heads,
                                   c_lora=c_lora, tk=min(tk, lp), q0=q0, l_valid=l_valid)
        out_a = _absorb_o(o_lat.reshape(m, n_heads * c_lora), pw["w_uv"], n_heads=n_heads, c_lora=c_lora)
    else:
        tml = _pick(b * lp, 512)
        k_full = _proj("mla_k_up", ckv_all, pw["w_uk"], _ep_kfull, BF16, out_cols=2 * n_heads * HEAD_DIM,
                       extras=[(krope_all, (tml, LANES), lambda i, j: (i, 0))])
        v_mla = _proj("mla_v_up", ckv_all, pw["w_uv"], _ep_plain, BF16)
        out_a = _mla_attention(q_full.reshape(b, t, -1), k_full.reshape(b, lp, -1), v_mla.reshape(b, lp, -1),
                               n_heads=n_heads, hb=min(MLA_HEADS_PER_STEP, n_heads), tq=min(tq_mla, t),
                               tk=min(tk, lp), q0=q0, l_valid=l_valid)

    out_b = _dsa_attention(q_dsa.reshape(b, t, -1), q_idx.reshape(b, t, -1), wi.reshape(b, t, LANES),
                           ki_all.reshape(b, lp, LANES), k_all.reshape(b, lp, kvw), v_all.reshape(b, lp, kvw),
                           n_kv=n_kv, rep=n_heads // n_kv, n_idx=n_idx, tq=min(tq_dsa, t), tk=min(tk, lp),
                           q0=q0, l_valid=l_valid, topk=min(TOPK_MAX, l_valid // 4))

    merged = _merge(gates, out_a.reshape(m, d), out_b.reshape(m, d))

    def gate_extra(gate, tm_, tn_):
        if per_row:
            return (gate, (tm_, tn_), lambda i, j: (i, j))
        return (gate, (1, tn_), lambda i, j: (0, j))

    tn = _pick(d, 1024)
    x1 = _proj("out_proj", merged, pw["w_out"], _ep_residual, F32,
                     extras=[(xf, (tm, tn), lambda i, j: (i, j)), gate_extra(gt1, tm, tn)])
    h2 = _norm(x1, g_norm2, sc2, sh2, BF16)
    u = _proj("mlp_up", h2, pw["w_up"], _ep_relu2, BF16)
    tm2 = _pick(m, 1024)
    x2 = _proj("mlp_down", u, pw["w_down"], _ep_residual, F32, tm=tm2, tk=2048,
               extras=[(x1, (tm2, tn), lambda i, j: (i, j)), gate_extra(gt2, tm2, tn)])
    rows = (ckv.reshape(b, t, c_lora), krope.reshape(b, t, r_mla), k_new.reshape(b, t, n_kv, HEAD_DIM),
            v_new.reshape(b, t, n_kv, HEAD_DIM), ki_new.reshape(b, t, HEAD_DIM))
    return x2, rows


def kernel(x_prompt, x_sample, c_prompt, c_sample, cache_mla_ckv, cache_mla_krope, cache_dsa_k, cache_dsa_v, cache_idx_k, w_ada, b_ada, g_norm1, w_in, g_q_lora, w_uq, g_kv_lora, w_uk, w_uv, w_out, g_norm2, w_up, w_down, g_final):
    depth = w_in.shape[0]
    bp, tp, d = x_prompt.shape
    bs, ts, _ = x_sample.shape
    n_heads = d // HEAD_DIM
    dims = (d, w_uq.shape[1], cache_mla_ckv.shape[-1], cache_mla_krope.shape[-1], n_heads,
            cache_dsa_k.shape[3], n_heads)
    past_len = cache_mla_ckv.shape[2]

    c_all = jnp.concatenate([c_prompt, c_sample], axis=0)
    n_c = c_all.shape[0]
    c_all = jnp.pad(c_all, ((0, -(-n_c // 16) * 16 - n_c), (0, 0)))

    xp, xs = x_prompt, x_sample
    rows_p, rows_s = [], []
    for l in range(depth):
        ada = _ada(c_all, w_ada[l], b_ada[l])
        pw = _prep_weights(w_in[l], w_uq[l], w_uk[l], w_uv[l], w_out[l], w_up[l], w_down[l], dims)
        norms = (g_norm1[l], g_q_lora[l], g_kv_lora[l], g_norm2[l])
        xp2, rp = _layer(xp, ada[0:bp], None, 0, pw, *norms, dims, tq_mla=TQ_MLA, tq_dsa=TQ_DSA, tk=TK_PROMPT)
        past = (cache_mla_ckv[l], cache_mla_krope[l], cache_dsa_k[l], cache_dsa_v[l], cache_idx_k[l])
        xs2, rs = _layer(xs, ada[bp:bp + bs], past, past_len, pw, *norms, dims, tq_mla=TQ_MLA, tq_dsa=TQ_DSA,
                         tk=TK_SAMPLE)
        xp, xs = xp2.reshape(bp, tp, d), xs2.reshape(bs, ts, d)
        rows_p.append(rp)
        rows_s.append(rs)

    y_prompt = _norm(xp.reshape(bp * tp, d), g_final, None, None, F32).reshape(bp, tp, d)
    y_sample = _norm(xs.reshape(bs * ts, d), g_final, None, None, F32).reshape(bs, ts, d)
    stack = lambda rows, i: jnp.stack([r[i] for r in rows])
    return (y_prompt, y_sample,
            stack(rows_p, 0), stack(rows_p, 1), stack(rows_p, 2), stack(rows_p, 3), stack(rows_p, 4),
            stack(rows_s, 0), stack(rows_s, 1), stack(rows_s, 2), stack(rows_s, 3), stack(rows_s, 4))
```

```python
import functools

import numpy as np
import jax
import jax.numpy as jnp
from jax import lax
from jax.experimental import pallas as pl
from jax.experimental.pallas import tpu as pltpu

F32 = jnp.float32
BF16 = jnp.bfloat16
I32 = jnp.int32

LANES = 128
CHUNK = 64
CHUNK_LOG2 = 6
EPS = 1e-6
NEG = -1e30
INT_MIN = -2 ** 31
MLA_THETA = 10000.0
ROPE_THETA = 500000.0
HEAD_DIM = 128
DSA_ROT = HEAD_DIM // 4
IDX_ROT = HEAD_DIM // 4
TOPK_MAX = 256
N_ADA = 6
VMEM_LIMIT = 56 * 1024 * 1024
VMEM_LIMIT_DSA = 60 * 1024 * 1024
TQ_MLA, TQ_DSA = 512, 128
MLA_HEADS_PER_STEP = 4
DSA_GROUPS_PER_PASS = 4
TK_PROMPT, TK_SAMPLE = 512, 384


def _cparams(sem):
    return pltpu.CompilerParams(dimension_semantics=sem, vmem_limit_bytes=VMEM_LIMIT)


def _pick(n, pref):
    if n <= pref:
        return n
    t = pref
    while n % t:
        t //= 2
    return t


def _rope_table(pos, theta, rot_dim):
    half = rot_dim // 2
    inv = theta ** (-jnp.arange(half, dtype=F32) / half)
    ang = pos.astype(F32)[:, None] * inv[None, :]
    cos, sin = jnp.cos(ang), jnp.sin(ang)
    n = pos.shape[0]
    ones = jnp.ones((n, LANES - rot_dim), F32)
    zeros = jnp.zeros((n, LANES - rot_dim), F32)
    zh = jnp.zeros((n, half), F32)
    a = jnp.concatenate([cos, cos, ones], axis=1)
    b = jnp.concatenate([-sin, zh, zeros], axis=1)
    c = jnp.concatenate([zh, sin, zeros], axis=1)
    return jnp.concatenate([a, b, c], axis=1)


def _rope_slab(x, tab, half):
    a = tab[:, 0:LANES]
    b = tab[:, LANES:2 * LANES]
    c = tab[:, 2 * LANES:3 * LANES]
    return x * a + pltpu.roll(x, LANES - half, 1) * b + pltpu.roll(x, half, 1) * c


def _mm_body(*refs, nk, n_extra, n_out, epilogue):
    a_ref, w_ref = refs[0], refs[1]
    extra = refs[2:2 + n_extra]
    outs = refs[2 + n_extra:2 + n_extra + n_out]
    a = a_ref[...].astype(BF16)
    w = w_ref[...].astype(BF16)
    part = jnp.dot(a, w, preferred_element_type=F32)
    if nk == 1:
        epilogue(part, extra, outs)
        return
    acc_ref = refs[2 + n_extra + n_out]
    k = pl.program_id(2)

    @pl.when(k == 0)
    def _():
        acc_ref[...] = part

    @pl.when(k > 0)
    def _():
        acc_ref[...] += part

    @pl.when(k == nk - 1)
    def _():
        epilogue(acc_ref[...], extra, outs)


def _matmul(a, w, *, n, col0, tm, tn, tk, extras, outs, epilogue, name, order="nm"):
    m, kdim = a.shape
    assert col0 % tn == 0
    jb = col0 // tn
    gi, gj, gk = m // tm, n // tn, kdim // tk
    if order == "nm":
        grid = (gj, gi, gk)
        wrap = lambda f: (lambda pj, pi, pk: f(pi, pj, pk))
    else:
        grid = (gi, gj, gk)
        wrap = lambda f: (lambda pi, pj, pk: f(pi, pj, pk))
    in_specs = [
        pl.BlockSpec((tm, tk), wrap(lambda i, j, k: (i, k))),
        pl.BlockSpec((tk, tn), wrap(lambda i, j, k: (k, j + jb))),
    ]
    args = [a, w]
    for arr, blk, fn in extras:
        in_specs.append(pl.BlockSpec(blk, wrap(lambda i, j, k, fn=fn: fn(i, j))))
        args.append(arr)
    out_specs = [pl.BlockSpec(blk, wrap(lambda i, j, k, fn=fn: fn(i, j))) for _, blk, fn in outs]
    out_shape = [s for s, _, _ in outs]
    scratch = [pltpu.VMEM((tm, tn), F32)] if gk > 1 else []
    body = functools.partial(_mm_body, nk=gk, n_extra=len(extras), n_out=len(outs), epilogue=epilogue)
    res = pl.pallas_call(
        body, grid=grid, in_specs=in_specs, out_specs=out_specs, out_shape=out_shape,
        scratch_shapes=scratch, name=name,
        compiler_params=_cparams(("parallel", "parallel", "arbitrary")),
    )(*args)
    return res


def _ep_plain(acc, extra, outs):
    outs[0][...] = acc.astype(outs[0].dtype)


def _ep_rmsnorm(acc, extra, outs):
    g = extra[0][...]
    y = acc * lax.rsqrt(jnp.mean(acc * acc, axis=-1, keepdims=True) + EPS) * g
    outs[0][...] = y.astype(outs[0].dtype)


def _ep_sigmoid(acc, extra, outs):
    outs[0][...] = jax.nn.sigmoid(acc).astype(outs[0].dtype)


def _ep_relu2(acc, extra, outs):
    r = jnp.maximum(acc, 0.0)
    outs[0][...] = (r * r).astype(outs[0].dtype)


def _ep_residual(acc, extra, outs):
    x = extra[0][...]
    gate = extra[1][...]
    outs[0][...] = x + gate * acc


def _ep_rope(acc, extra, outs, *, half, period, roped, scale):
    tab = extra[0][...]
    o = outs[0]
    for s in range(acc.shape[1] // LANES):
        x = acc[:, s * LANES:(s + 1) * LANES]
        if (s % period) in roped:
            x = _rope_slab(x, tab, half)
        if scale != 1.0:
            x = x * scale
        o[:, s * LANES:(s + 1) * LANES] = x.astype(o.dtype)


def _ep_kfull(acc, extra, outs):
    kr = extra[0][...]
    o = outs[0]
    for h in range(acc.shape[1] // LANES):
        o[:, (2 * h) * LANES:(2 * h + 1) * LANES] = acc[:, h * LANES:(h + 1) * LANES].astype(o.dtype)
        o[:, (2 * h + 1) * LANES:(2 * h + 2) * LANES] = kr


TM, TN, TK = 512, 1024, 4096


def _proj(name, a, w, ep, out_dtype, *, cols=None, tm=TM, tn=TN, tk=TK, extras=(), out_cols=None, order="nm"):
    m, k = a.shape
    col0, n = (0, w.shape[1]) if cols is None else cols
    tm, tn, tk = _pick(m, tm), _pick(n, tn), _pick(k, tk)
    oc = n if out_cols is None else out_cols
    outs = [(jax.ShapeDtypeStruct((m, oc), out_dtype), (tm, tn * oc // n), lambda i, j: (i, j))]
    return _matmul(a, w, n=n, col0=col0, tm=tm, tn=tn, tk=tk, extras=list(extras), outs=outs, epilogue=ep,
                   name=name, order=order)[0]


def _ada_body(c_ref, w_ref, b_ref, o_ref):
    c = c_ref[...]
    a = (c * jax.nn.sigmoid(c)).astype(BF16)
    o_ref[...] = jnp.dot(a, w_ref[...].astype(BF16), preferred_element_type=F32) + b_ref[...]


def _ada(c_all, w_ada, b_ada):
    r, d = c_all.shape
    n = w_ada.shape[1]
    tn = _pick(n, 512)
    return pl.pallas_call(
        _ada_body, grid=(n // tn,),
        in_specs=[pl.BlockSpec((r, d), lambda j: (0, 0)),
                  pl.BlockSpec((d, tn), lambda j: (0, j)),
                  pl.BlockSpec((1, tn), lambda j: (0, j))],
        out_specs=pl.BlockSpec((r, tn), lambda j: (0, j)),
        out_shape=jax.ShapeDtypeStruct((r, n), F32), name="adaln",
        compiler_params=_cparams(("parallel",)),
    )(c_all, w_ada, b_ada.reshape(1, n))


def _norm_body(*refs, modulate):
    x = refs[0][...]
    g = refs[1][...]
    y = x * lax.rsqrt(jnp.mean(x * x, axis=-1, keepdims=True) + EPS) * g
    if modulate:
        y = y * (1.0 + refs[2][...]) + refs[3][...]
        o = refs[4]
    else:
        o = refs[2]
    o[...] = y.astype(o.dtype)


def _norm(x, g, sc, sh, out_dtype):
    m, d = x.shape
    tm = _pick(m, 256)
    in_specs = [pl.BlockSpec((tm, d), lambda i: (i, 0)), pl.BlockSpec((1, d), lambda i: (0, 0))]
    args = [x, g.reshape(1, d)]
    if sc is not None:
        per_row = sc.shape[0] == m
        blk = (tm, d) if per_row else (1, d)
        fn = (lambda i: (i, 0)) if per_row else (lambda i: (0, 0))
        in_specs += [pl.BlockSpec(blk, fn), pl.BlockSpec(blk, fn)]
        args += [sc, sh]
    return pl.pallas_call(
        functools.partial(_norm_body, modulate=sc is not None), grid=(m // tm,),
        in_specs=in_specs, out_specs=pl.BlockSpec((tm, d), lambda i: (i, 0)),
        out_shape=jax.ShapeDtypeStruct((m, d), out_dtype), name="rmsnorm",
        compiler_params=_cparams(("parallel",)),
    )(*args)


def _finish_body(z_ref, g_ref, tm_ref, td_ref,
                 ckv_o, ckvb_o, kr_o, krb_o, k_o, kb_o, v_o, vb_o, ki_o, kib_o, wi_o,
                 *, c_lora, r_mla, n_kv, idx_scale):
    tab_m = tm_ref[...]
    tab_d = td_ref[...]
    off = 0
    ckv = z_ref[:, off:off + c_lora]
    ckv = ckv * lax.rsqrt(jnp.mean(ckv * ckv, axis=-1, keepdims=True) + EPS) * g_ref[...]
    ckv_o[...] = ckv
    ckvb_o[...] = ckv.astype(BF16)
    off += c_lora
    kr = _rope_slab(z_ref[:, off:off + LANES], tab_m, r_mla // 2)
    kr_o[...] = kr[:, 0:r_mla]
    krb_o[...] = kr.astype(BF16)
    off += LANES
    for h in range(n_kv):
        kh = _rope_slab(z_ref[:, off + h * LANES:off + (h + 1) * LANES], tab_d, DSA_ROT // 2)
        k_o[:, h * LANES:(h + 1) * LANES] = kh
        kb_o[:, h * LANES:(h + 1) * LANES] = kh.astype(BF16)
    off += n_kv * LANES
    v = z_ref[:, off:off + n_kv * LANES]
    v_o[...] = v
    vb_o[...] = v.astype(BF16)
    off += n_kv * LANES
    ki = _rope_slab(z_ref[:, off:off + LANES], tab_d, IDX_ROT // 2)
    ki_o[...] = ki
    kib_o[...] = ki.astype(BF16)
    off += LANES
    wi_o[...] = z_ref[:, off:off + LANES] * idx_scale


def _finish(z, g_kv, tab_m, tab_d, *, c_lora, r_mla, n_kv, idx_scale):
    m, zc = z.shape
    tm = _pick(m, 256)
    kvw = n_kv * LANES
    row = lambda i: (i, 0)
    shapes = [(c_lora, F32), (c_lora, BF16), (r_mla, F32), (LANES, BF16), (kvw, F32), (kvw, BF16),
              (kvw, F32), (kvw, BF16), (LANES, F32), (LANES, BF16), (LANES, F32)]
    return pl.pallas_call(
        functools.partial(_finish_body, c_lora=c_lora, r_mla=r_mla, n_kv=n_kv, idx_scale=idx_scale),
        grid=(m // tm,),
        in_specs=[pl.BlockSpec((tm, zc), row), pl.BlockSpec((1, c_lora), lambda i: (0, 0)),
                  pl.BlockSpec((tm, 3 * LANES), row), pl.BlockSpec((tm, 3 * LANES), row)],
        out_specs=[pl.BlockSpec((tm, w), row) for w, _ in shapes],
        out_shape=[jax.ShapeDtypeStruct((m, w), dt) for w, dt in shapes], name="finish_small",
        compiler_params=_cparams(("parallel",)),
    )(z, g_kv.reshape(1, c_lora), tab_m, tab_d)


def _row_limits(q_first, tq, l_valid):
    rows = q_first + lax.broadcasted_iota(I32, (tq, 1), 0)
    chunk_end = lambda p: (lax.shift_right_logical(p, CHUNK_LOG2) + 1) * CHUNK
    lim = jnp.minimum(chunk_end(rows), l_valid)
    lim_min = jnp.minimum(chunk_end(q_first), l_valid)
    lim_max = jnp.minimum(chunk_end(q_first + tq - 1), l_valid)
    return lim, lim_min, lim_max


LOG2E = 1.4426950408889634


def _lane_tile(x, width):
    return x if width == LANES else jnp.concatenate([x] * (width // LANES), axis=1)


def _softmax_update(s, v, m_prev, l_prev, acc_prev):
    m_new = jnp.maximum(m_prev, jnp.max(s, axis=1, keepdims=True))
    p = jnp.exp2(s - _lane_tile(m_new, s.shape[1]))
    alpha = jnp.exp2(m_prev - m_new)
    l_new = alpha * l_prev + jnp.sum(p, axis=1, keepdims=True)
    acc_new = _lane_tile(alpha, acc_prev.shape[1]) * acc_prev + jnp.dot(p.astype(BF16), v,
                                                                        preferred_element_type=F32)
    return m_new, l_new, acc_new


def _softmax_step(s, v, m_sc, l_sc, acc_sc):
    m_sc[...], l_sc[...], acc_sc[...] = _softmax_update(s, v, m_sc[...], l_sc[...], acc_sc[...])


_NT = (((1,), (1,)), ((), ()))


def _mla_body(q_ref, k_ref, v_ref, o_ref, m_sc, l_sc, acc_sc, s_sc, *, hb, tq, tk, q0, l_valid):
    q_first = q0 + pl.program_id(2) * tq
    lim, lim_min, lim_max = _row_limits(q_first, tq, l_valid)
    n_full = lim_min // tk
    n_all = (lim_max + tk - 1) // tk
    col = lax.broadcasted_iota(I32, (1, tk), 1)
    m_sc[...] = jnp.full(m_sc.shape, NEG, F32)
    l_sc[...] = jnp.zeros(l_sc.shape, F32)
    acc_sc[...] = jnp.zeros(acc_sc.shape, F32)

    def scores(c, j):
        start = pl.multiple_of(c * tk, tk)
        return lax.dot_general(q_ref[:, j * 2 * LANES:(j + 1) * 2 * LANES],
                               k_ref[pl.ds(start, tk), j * 2 * LANES:(j + 1) * 2 * LANES], _NT,
                               preferred_element_type=F32)

    for j in range(hb):
        s_sc[j] = scores(0, j)

    def step(c, masked):
        start = pl.multiple_of(c * tk, tk)
        nxt = jnp.minimum(c + 1, n_all - 1)
        for j in range(hb):
            s = s_sc[j]
            if masked:
                s = jnp.where(col + start < lim, s, NEG)
            _softmax_step(s, v_ref[pl.ds(start, tk), j * LANES:(j + 1) * LANES],
                          m_sc.at[j], l_sc.at[j], acc_sc.at[j])
            s_sc[j] = scores(nxt, j)

    def full_step(c, carry):
        step(c, False)
        return carry

    def masked_step(c, carry):
        step(c, True)
        return carry

    lax.fori_loop(0, n_full, full_step, 0)
    lax.fori_loop(n_full, n_all, masked_step, 0)
    for j in range(hb):
        o_ref[:, j * LANES:(j + 1) * LANES] = (acc_sc[j] * (1.0 / l_sc[j])).astype(o_ref.dtype)


def _mla_attention(q_full, k_full, v, *, n_heads, hb, tq, tk, q0, l_valid):
    b, t, _ = q_full.shape
    lp = k_full.shape[1]
    kv_mode = dict(pipeline_mode=pl.Buffered(1)) if t // tq > 2 else {}
    stat = pltpu.VMEM((hb, tq, LANES), F32)
    return pl.pallas_call(
        functools.partial(_mla_body, hb=hb, tq=tq, tk=tk, q0=q0, l_valid=l_valid),
        grid=(b, n_heads // hb, t // tq),
        in_specs=[pl.BlockSpec((None, tq, hb * 2 * LANES), lambda bi, h, qi: (bi, qi, h)),
                  pl.BlockSpec((None, lp, hb * 2 * LANES), lambda bi, h, qi: (bi, 0, h), **kv_mode),
                  pl.BlockSpec((None, lp, hb * LANES), lambda bi, h, qi: (bi, 0, h), **kv_mode)],
        out_specs=pl.BlockSpec((None, tq, hb * LANES), lambda bi, h, qi: (bi, qi, h)),
        out_shape=jax.ShapeDtypeStruct((b, t, n_heads * LANES), BF16),
        scratch_shapes=[stat, stat, stat, pltpu.VMEM((hb, tq, tk), F32)],
        name="mla_attention",
        compiler_params=_cparams(("parallel", "parallel", "arbitrary")),
    )(q_full, k_full, v)


def _absorb_q_body(q_ref, w_ref, o_ref, *, c_lora):
    lat = lax.dot_general(q_ref[:, 0:LANES], w_ref[...], _NT, preferred_element_type=F32)
    o_ref[:, 0:c_lora] = lat.astype(o_ref.dtype)
    o_ref[:, c_lora:c_lora + LANES] = q_ref[:, LANES:2 * LANES]


def _absorb_q(q_full, w_uk, *, n_heads, c_lora):
    m = q_full.shape[0]
    return pl.pallas_call(
        functools.partial(_absorb_q_body, c_lora=c_lora), grid=(n_heads,),
        in_specs=[pl.BlockSpec((m, 2 * LANES), lambda h: (0, h)), pl.BlockSpec((c_lora, LANES), lambda h: (0, h))],
        out_specs=pl.BlockSpec((m, c_lora + LANES), lambda h: (0, h)),
        out_shape=jax.ShapeDtypeStruct((m, n_heads * (c_lora + LANES)), BF16), name="mla_absorb_q",
        compiler_params=_cparams(("parallel",)),
    )(q_full, w_uk)


def _absorb_o_body(o_lat_ref, w_ref, o_ref):
    o_ref[...] = jnp.dot(o_lat_ref[...], w_ref[...], preferred_element_type=F32).astype(o_ref.dtype)


def _absorb_o(o_lat, w_uv, *, n_heads, c_lora):
    m = o_lat.shape[0]
    return pl.pallas_call(
        _absorb_o_body, grid=(n_heads,),
        in_specs=[pl.BlockSpec((m, c_lora), lambda h: (0, h)), pl.BlockSpec((c_lora, LANES), lambda h: (0, h))],
        out_specs=pl.BlockSpec((m, LANES), lambda h: (0, h)),
        out_shape=jax.ShapeDtypeStruct((m, n_heads * LANES), BF16), name="mla_absorb_o",
        compiler_params=_cparams(("parallel",)),
    )(o_lat, w_uv)


def _mla_abs_body(q_ref, kc_ref, o_ref, *, bb, t, n_heads, c_lora, tk, q0, l_valid):
    rows = t * n_heads
    tok = lax.broadcasted_iota(I32, (t, n_heads, 1), 0).reshape(rows, 1)
    lim = jnp.minimum((lax.shift_right_logical(q0 + tok, CHUNK_LOG2) + 1) * CHUNK, l_valid)
    lim_min = min((q0 // CHUNK + 1) * CHUNK, l_valid)
    lim_max = min(((q0 + t - 1) // CHUNK + 1) * CHUNK, l_valid)
    col = lax.broadcasted_iota(I32, (1, tk), 1)
    for j in range(bb):
        q = q_ref[j]
        m = jnp.full((rows, LANES), NEG, F32)
        l = jnp.zeros((rows, LANES), F32)
        acc = jnp.zeros((rows, c_lora), F32)
        for c in range(-(-lim_max // tk)):
            kc = kc_ref[j, c * tk:(c + 1) * tk, :]
            s = lax.dot_general(q, kc, _NT, preferred_element_type=F32)
            if (c + 1) * tk > lim_min:
                s = jnp.where(col + c * tk < lim, s, NEG)
            m, l, acc = _softmax_update(s, kc[:, 0:c_lora], m, l, acc)
        o_ref[j] = (acc * _lane_tile(1.0 / l, c_lora)).astype(o_ref.dtype)


def _mla_abs_attention(q_abs, kc, *, t, n_heads, c_lora, tk, q0, l_valid):
    b, rows, dk = q_abs.shape
    lp = kc.shape[1]
    bb = _pick(b, 4)
    return pl.pallas_call(
        functools.partial(_mla_abs_body, bb=bb, t=t, n_heads=n_heads, c_lora=c_lora, tk=tk, q0=q0, l_valid=l_valid),
        grid=(b // bb,),
        in_specs=[pl.BlockSpec((bb, rows, dk), lambda i: (i, 0, 0)), pl.BlockSpec((bb, lp, dk), lambda i: (i, 0, 0))],
        out_specs=pl.BlockSpec((bb, rows, c_lora), lambda i: (i, 0, 0)),
        out_shape=jax.ShapeDtypeStruct((b, rows, c_lora), BF16), name="mla_absorbed_attention",
        compiler_params=_cparams(("parallel",)),
    )(q_abs, kc)


def _dsa_body(qb_ref, qi_ref, wi_ref, ki_ref, kt_ref, v_ref, o_ref,
              keys_sc, m_sc, l_sc, acc_sc, s_sc, *, tq, tk, q0, l_valid, n_idx, n_kv, rep, topk):
    q_first = q0 + pl.program_id(1) * tq
    lim, _, lim_max = _row_limits(q_first, tq, l_valid)
    n_all = (lim_max + tk - 1) // tk
    col = lax.broadcasted_iota(I32, (1, tk), 1)
    w = wi_ref[...]

    def score_chunk(c, carry):
        start = pl.multiple_of(c * tk, tk)
        kic = ki_ref[pl.ds(start, tk), :]
        s = jnp.zeros((tq, tk), F32)
        for h in range(n_idx):
            y = lax.dot_general(qi_ref[:, h * LANES:(h + 1) * LANES], kic, _NT,
                                preferred_element_type=F32)
            s = s + w[:, h:h + 1] * jnp.maximum(y, 0.0)
        bits = lax.bitcast_convert_type(s, I32)
        key = jnp.where(bits >= 0, bits, bits ^ jnp.int32(0x7FFFFFFF))
        keys_sc[c] = jnp.where(col + start < lim, key, jnp.int32(INT_MIN))
        return carry

    lax.fori_loop(0, n_all, score_chunk, 0)

    def bit_pass(i, t):
        cand = t + lax.shift_left(jnp.int32(1), jnp.int32(31) - i)

        def count_chunk(c, acc):
            ind = jnp.where(keys_sc[c] >= cand, 1.0, 0.0)
            part = ind[:, 0:LANES]
            for j in range(1, tk // LANES):
                part = part + ind[:, j * LANES:(j + 1) * LANES]
            return acc + part

        acc = lax.fori_loop(0, n_all, count_chunk, jnp.zeros((tq, LANES), F32))
        cnt = jnp.sum(acc, axis=1, keepdims=True)
        return jnp.where(cnt >= topk, cand, t)

    t = lax.fori_loop(0, 32, bit_pass, jnp.full((tq, 1), INT_MIN, I32))
    thr = jnp.maximum(t, jnp.int32(INT_MIN + 1))

    rows = rep * tq
    tok = jnp.bitwise_and(lax.broadcasted_iota(I32, (rows, LANES), 0), tq - 1)
    eye = jnp.where(tok == lax.broadcasted_iota(I32, (rows, LANES), 1), 1.0, 0.0).astype(BF16)

    gb = m_sc.shape[0]
    for g0 in range(0, n_kv, gb):
        m_sc[...] = jnp.full(m_sc.shape, NEG, F32)
        l_sc[...] = jnp.zeros(l_sc.shape, F32)
        acc_sc[...] = jnp.zeros(acc_sc.shape, F32)

        def mask_bias(c):
            bias = jnp.where(keys_sc[c] >= thr, 0.0, NEG).astype(BF16)
            if tq < LANES:
                bias = jnp.concatenate([bias, jnp.zeros((LANES - tq, tk), BF16)], axis=0)
            return bias

        def scores(c, g, bias):
            qs = jnp.concatenate([qb_ref[:, (g * rep + r) * LANES:(g * rep + r + 1) * LANES]
                                  for r in range(rep)], axis=0)
            return jnp.dot(jnp.concatenate([qs, eye], axis=1), jnp.concatenate([kt_ref[g, c], bias], axis=0),
                           preferred_element_type=F32)

        bias0 = mask_bias(0)
        for j in range(gb):
            s_sc[j] = scores(0, g0 + j, bias0)

        def attend_chunk(c, carry, g0=g0, scores=scores, mask_bias=mask_bias):
            start = pl.multiple_of(c * tk, tk)
            nxt = jnp.minimum(c + 1, n_all - 1)
            bias = mask_bias(nxt)
            for j in range(gb):
                g = g0 + j
                _softmax_step(s_sc[j], v_ref[pl.ds(start, tk), g * LANES:(g + 1) * LANES],
                              m_sc.at[j], l_sc.at[j], acc_sc.at[j])
                s_sc[j] = scores(nxt, g, bias)
            return carry

        lax.fori_loop(0, n_all, attend_chunk, 0)
        for j in range(gb):
            o = acc_sc[j] * (1.0 / l_sc[j])
            for r in range(rep):
                h = (g0 + j) * rep + r
                o_ref[:, h * LANES:(h + 1) * LANES] = o[r * tq:(r + 1) * tq].astype(o_ref.dtype)


def _dsa_attention(qb, qi, wi, ki, k, v, *, n_kv, rep, n_idx, tq, tk, q0, l_valid, topk):
    b, t, _ = qb.shape
    lp = k.shape[1]
    kvw = n_kv * LANES
    assert tq <= LANES and tq & (tq - 1) == 0, "the token one-hot needs a power-of-two block of at most 128 rows"
    kt = k.reshape(b, lp // tk, tk, n_kv, LANES).transpose(0, 3, 1, 4, 2)
    kv_mode = dict(pipeline_mode=pl.Buffered(1)) if t // tq > 2 else {}
    stat = pltpu.VMEM((min(DSA_GROUPS_PER_PASS, n_kv), rep * tq, LANES), F32)
    return pl.pallas_call(
        functools.partial(_dsa_body, tq=tq, tk=tk, q0=q0, l_valid=l_valid, n_idx=n_idx, n_kv=n_kv, rep=rep,
                          topk=topk),
        grid=(b, t // tq),
        in_specs=[pl.BlockSpec((None, tq, n_kv * rep * LANES), lambda bi, qb_: (bi, qb_, 0)),
                  pl.BlockSpec((None, tq, n_idx * LANES), lambda bi, qb_: (bi, qb_, 0)),
                  pl.BlockSpec((None, tq, LANES), lambda bi, qb_: (bi, qb_, 0)),
                  pl.BlockSpec((None, lp, LANES), lambda bi, qb_: (bi, 0, 0), **kv_mode),
                  pl.BlockSpec((None, n_kv, lp // tk, LANES, tk), lambda bi, qb_: (bi, 0, 0, 0, 0), **kv_mode),
                  pl.BlockSpec((None, lp, kvw), lambda bi, qb_: (bi, 0, 0), **kv_mode)],
        out_specs=pl.BlockSpec((None, tq, n_kv * rep * LANES), lambda bi, qb_: (bi, qb_, 0)),
        out_shape=jax.ShapeDtypeStruct((b, t, n_kv * rep * LANES), BF16),
        scratch_shapes=[pltpu.VMEM((lp // tk, tq, tk), I32), stat, stat, stat,
                        pltpu.VMEM((stat.shape[0], rep * tq, tk), F32)],
        name="dsa_attention",
        compiler_params=pltpu.CompilerParams(dimension_semantics=("parallel", "arbitrary"),
                                             vmem_limit_bytes=VMEM_LIMIT_DSA),
    )(qb, qi, wi, ki, kt, v)


def _merge_body(ga_ref, gb_ref, oa_ref, ob_ref, o_ref):
    y = (ga_ref[...].astype(F32) * oa_ref[...].astype(F32)
         + gb_ref[...].astype(F32) * ob_ref[...].astype(F32))
    o_ref[...] = y.astype(o_ref.dtype)


def _merge(gates, out_a, out_b):
    m, d = out_a.shape
    tm = _pick(m, 256)
    return pl.pallas_call(
        _merge_body, grid=(m // tm,),
        in_specs=[pl.BlockSpec((tm, d), lambda i: (i, 0)), pl.BlockSpec((tm, d), lambda i: (i, 1)),
                  pl.BlockSpec((tm, d), lambda i: (i, 0)), pl.BlockSpec((tm, d), lambda i: (i, 0))],
        out_specs=pl.BlockSpec((tm, d), lambda i: (i, 0)),
        out_shape=jax.ShapeDtypeStruct((m, d), BF16), name="merge_gates",
        compiler_params=_cparams(("parallel",)),
    )(gates, gates, out_a, out_b)


def _pad_cols(w, width):
    return jnp.pad(w, ((0, 0), (0, width - w.shape[1])))


def _prep_weights(w_in, w_uq, w_uk, w_uv, w_out, w_up, w_down, dims):
    d, q_lora, c_lora, r_mla, n_heads, n_kv, n_idx = dims
    splits = (q_lora, c_lora, r_mla, n_heads * HEAD_DIM, n_kv * HEAD_DIM, n_kv * HEAD_DIM,
              n_idx * HEAD_DIM, HEAD_DIM, n_idx, d, d)
    offs = np.concatenate([[0], np.cumsum(splits)])
    sec = [w_in[:, int(offs[i]):int(offs[i + 1])] for i in range(len(splits))]
    small = jnp.concatenate([sec[1], _pad_cols(sec[2], LANES), sec[4], sec[5], sec[7],
                             _pad_cols(sec[8], LANES)], axis=1)
    small = _pad_cols(small, -(-small.shape[1] // 512) * 512)
    parts = [("q", sec[0]), ("small", small), ("qb", sec[3]), ("qi", sec[6]),
             ("gates", jnp.concatenate([sec[9], sec[10]], axis=1))]
    windows, off = {}, 0
    for i, (key, p) in enumerate(parts):
        windows[key] = (off, p.shape[1])
        parts[i] = (key, _pad_cols(p, -(-p.shape[1] // TN) * TN))
        off += parts[i][1].shape[1]
    wq = w_uq.reshape(q_lora, n_heads, HEAD_DIM + r_mla)
    wq = jnp.pad(wq, ((0, 0), (0, 0), (0, 2 * LANES - HEAD_DIM - r_mla)))
    return dict(
        w_in=jnp.concatenate([p for _, p in parts], axis=1).astype(BF16), win=windows,
        w_uq=wq.reshape(q_lora, n_heads * 2 * LANES).astype(BF16),
        w_uk=w_uk.reshape(c_lora, n_heads * HEAD_DIM).astype(BF16),
        w_uv=w_uv.reshape(c_lora, n_heads * HEAD_DIM).astype(BF16),
        w_out=w_out.astype(BF16), w_up=w_up.astype(BF16), w_down=w_down.astype(BF16))


def _layer(x, ada, past, q0, pw, g_norm1, g_q_lora, g_kv_lora, g_norm2, dims, tq_mla, tq_dsa, tk):
    d, q_lora, c_lora, r_mla, n_heads, n_kv, n_idx = dims
    b, t, _ = x.shape
    m = b * t
    xf = x.reshape(m, d)
    if ada.shape[0] == 1:
        mods = [ada[:, i * d:(i + 1) * d] for i in range(N_ADA)]
    else:
        mods = [jnp.repeat(ada[:, i * d:(i + 1) * d], t, axis=0) for i in range(N_ADA)]
    sh1, sc1, gt1, sh2, sc2, gt2 = mods
    per_row = ada.shape[0] != 1

    pos = q0 + jnp.arange(t, dtype=I32)
    tab_m = jnp.tile(_rope_table(pos, MLA_THETA, r_mla), (b, 1))
    tab_d = jnp.tile(_rope_table(pos, ROPE_THETA, DSA_ROT), (b, 1))

    h = _norm(xf, g_norm1, sc1, sh1, BF16)

    def tab_extra(tab, tm):
        return (tab, (tm, 3 * LANES), lambda i, j: (i, 0))

    tm = _pick(m, 512)
    qn = _proj("proj_qlat", h, pw["w_in"], _ep_rmsnorm, BF16, cols=pw["win"]["q"], tn=q_lora,
                     extras=[(g_q_lora.reshape(1, q_lora), (1, q_lora), lambda i, j: (0, 0))])
    z_small = _proj("proj_small", h, pw["w_in"], _ep_plain, F32, cols=pw["win"]["small"], tn=512)
    q_dsa = _proj("proj_qdsa", h, pw["w_in"],
                  functools.partial(_ep_rope, half=DSA_ROT // 2, period=1, roped=(0,),
                                    scale=HEAD_DIM ** -0.5 * LOG2E),
                  BF16, cols=pw["win"]["qb"], extras=[tab_extra(tab_d, tm)])
    q_idx = _proj("proj_qidx", h, pw["w_in"],
                  functools.partial(_ep_rope, half=IDX_ROT // 2, period=1, roped=(0,), scale=1.0),
                  BF16, cols=pw["win"]["qi"], extras=[tab_extra(tab_d, tm)])
    gates = _proj("proj_gates", h, pw["w_in"], _ep_sigmoid, BF16, cols=pw["win"]["gates"])

    (ckv, ckv_b, krope, krope_b, k_new, k_new_b, v_new, v_new_b, ki_new, ki_new_b, wi) = _finish(
        z_small, g_kv_lora, tab_m, tab_d, c_lora=c_lora, r_mla=r_mla, n_kv=n_kv, idx_scale=n_idx ** -0.5)

    kvw = n_kv * HEAD_DIM
    if past is None:
        l_valid = t
        lp = t
        ckv_all, krope_all, k_all, v_all, ki_all = ckv_b, krope_b, k_new_b, v_new_b, ki_new_b
    else:
        p_ckv, p_krope, p_k, p_v, p_ki = past
        l_valid = p_ckv.shape[1] + t
        lp = -(-l_valid // tk) * tk

        def cat(p, new, width):
            p = p.reshape(b, p.shape[1], -1).astype(BF16)
            if p.shape[2] < width:
                p = jnp.pad(p, ((0, 0), (0, 0), (0, width - p.shape[2])))
            full = jnp.concatenate([p, new.reshape(b, t, width)], axis=1)
            return jnp.pad(full, ((0, 0), (0, lp - l_valid), (0, 0))).reshape(b * lp, width)

        ckv_all = cat(p_ckv, ckv_b, c_lora)
        krope_all = cat(p_krope, krope_b, LANES)
        k_all = cat(p_k, k_new_b, kvw)
        v_all = cat(p_v, v_new_b, kvw)
        ki_all = cat(p_ki, ki_new_b, LANES)

    q_full = _proj("mla_q_up", qn, pw["w_uq"],
                         functools.partial(_ep_rope, half=r_mla // 2, period=2, roped=(1,),
                                           scale=(HEAD_DIM + r_mla) ** -0.5 * LOG2E),
                         BF16, extras=[tab_extra(tab_m, tm)])
    if past is not None and t * n_heads <= 1024 and n_heads % 8 == 0:
        kc = jnp.concatenate([ckv_all.reshape(b, lp, c_lora), krope_all.reshape(b, lp, LANES)], axis=2)
        q_abs = _absorb_q(q_full, pw["w_uk"], n_heads=n_heads, c_lora=c_lora)
        o_lat = _mla_abs_attention(q_abs.reshape(b, t * n_heads, c_lora + LANES), kc, t=t, n_heads=n_heads,
                                   c_lora=c_lora, tk=min(tk, lp), q0=q0, l_valid=l_valid)
        out_a = _absorb_o(o_lat.reshape(m, n_heads * c_lora), pw["w_uv"], n_heads=n_heads, c_lora=c_lora)
    else:
        tml = _pick(b * lp, 512)
        k_full = _proj("mla_k_up", ckv_all, pw["w_uk"], _ep_kfull, BF16, out_cols=2 * n_heads * HEAD_DIM,
                       extras=[(krope_all, (tml, LANES), lambda i, j: (i, 0))])
        v_mla = _proj("mla_v_up", ckv_all, pw["w_uv"], _ep_plain, BF16)
        out_a = _mla_attention(q_full.reshape(b, t, -1), k_full.reshape(b, lp, -1), v_mla.reshape(b, lp, -1),
                               n_heads=n_heads, hb=min(MLA_HEADS_PER_STEP, n_heads), tq=min(tq_mla, t),
                               tk=min(tk, lp), q0=q0, l_valid=l_valid)

    out_b = _dsa_attention(q_dsa.reshape(b, t, -1), q_idx.reshape(b, t, -1), wi.reshape(b, t, LANES),
                           ki_all.reshape(b, lp, LANES), k_all.reshape(b, lp, kvw), v_all.reshape(b, lp, kvw),
                           n_kv=n_kv, rep=n_heads // n_kv, n_idx=n_idx, tq=min(tq_dsa, t), tk=min(tk, lp),
                           q0=q0, l_valid=l_valid, topk=min(TOPK_MAX, l_valid // 4))

    merged = _merge(gates, out_a.reshape(m, d), out_b.reshape(m, d))

    def gate_extra(gate, tm_, tn_):
        if per_row:
            return (gate, (tm_, tn_), lambda i, j: (i, j))
        return (gate, (1, tn_), lambda i, j: (0, j))

    tn = _pick(d, 1024)
    x1 = _proj("out_proj", merged, pw["w_out"], _ep_residual, F32,
                     extras=[(xf, (tm, tn), lambda i, j: (i, j)), gate_extra(gt1, tm, tn)])
    h2 = _norm(x1, g_norm2, sc2, sh2, BF16)
    u = _proj("mlp_up", h2, pw["w_up"], _ep_relu2, BF16)
    tm2 = _pick(m, 1024)
    x2 = _proj("mlp_down", u, pw["w_down"], _ep_residual, F32, tm=tm2, tk=2048,
               extras=[(x1, (tm2, tn), lambda i, j: (i, j)), gate_extra(gt2, tm2, tn)])
    rows = (ckv.reshape(b, t, c_lora), krope.reshape(b, t, r_mla), k_new.reshape(b, t, n_kv, HEAD_DIM),
            v_new.reshape(b, t, n_kv, HEAD_DIM), ki_new.reshape(b, t, HEAD_DIM))
    return x2, rows


def kernel(x_prompt, x_sample, c_prompt, c_sample, cache_mla_ckv, cache_mla_krope, cache_dsa_k, cache_dsa_v, cache_idx_k, w_ada, b_ada, g_norm1, w_in, g_q_lora, w_uq, g_kv_lora, w_uk, w_uv, w_out, g_norm2, w_up, w_down, g_final):
    depth = w_in.shape[0]
    bp, tp, d = x_prompt.shape
    bs, ts, _ = x_sample.shape
    n_heads = d // HEAD_DIM
    dims = (d, w_uq.shape[1], cache_mla_ckv.shape[-1], cache_mla_krope.shape[-1], n_heads,
            cache_dsa_k.shape[3], n_heads)
    past_len = cache_mla_ckv.shape[2]

    c_all = jnp.concatenate([c_prompt, c_sample], axis=0)
    n_c = c_all.shape[0]
    c_all = jnp.pad(c_all, ((0, -(-n_c // 16) * 16 - n_c), (0, 0)))

    xp, xs = x_prompt, x_sample
    rows_p, rows_s = [], []
    for l in range(depth):
        ada = _ada(c_all, w_ada[l], b_ada[l])
        pw = _prep_weights(w_in[l], w_uq[l], w_uk[l], w_uv[l], w_out[l], w_up[l], w_down[l], dims)
        norms = (g_norm1[l], g_q_lora[l], g_kv_lora[l], g_norm2[l])
        xp2, rp = _layer(xp, ada[0:bp], None, 0, pw, *norms, dims, tq_mla=TQ_MLA, tq_dsa=TQ_DSA, tk=TK_PROMPT)
        past = (cache_mla_ckv[l], cache_mla_krope[l], cache_dsa_k[l], cache_dsa_v[l], cache_idx_k[l])
        xs2, rs = _layer(xs, ada[bp:bp + bs], past, past_len, pw, *norms, dims, tq_mla=TQ_MLA, tq_dsa=TQ_DSA,
                         tk=TK_SAMPLE)
        xp, xs = xp2.reshape(bp, tp, d), xs2.reshape(bs, ts, d)
        rows_p.append(rp)
        rows_s.append(rs)

    y_prompt = _norm(xp.reshape(bp * tp, d), g_final, None, None, F32).reshape(bp, tp, d)
    y_sample = _norm(xs.reshape(bs * ts, d), g_final, None, None, F32).reshape(bs, ts, d)
    stack = lambda rows, i: jnp.stack([r[i] for r in rows])
    return (y_prompt, y_sample,
            stack(rows_p, 0), stack(rows_p, 1), stack(rows_p, 2), stack(rows_p, 3), stack(rows_p, 4),
            stack(rows_s, 0), stack(rows_s, 1), stack(rows_s, 2), stack(rows_s, 3), stack(rows_s, 4))
```

```python
import functools

import numpy as np
import jax
import jax.numpy as jnp
from jax import lax
from jax.experimental import pallas as pl
from jax.experimental.pallas import tpu as pltpu

F32 = jnp.float32
BF16 = jnp.bfloat16
I32 = jnp.int32

LANES = 128
CHUNK = 64
CHUNK_LOG2 = 6
EPS = 1e-6
NEG = -1e30
INT_MIN = -2 ** 31
INDEX_BITS = 24
MLA_THETA = 10000.0
ROPE_THETA = 500000.0
HEAD_DIM = 128
DSA_ROT = HEAD_DIM // 4
IDX_ROT = HEAD_DIM // 4
TOPK_MAX = 256
N_ADA = 6
VMEM_LIMIT = 56 * 1024 * 1024
VMEM_LIMIT_DSA = 60 * 1024 * 1024
TQ_MLA, TQ_DSA = 512, 128
MLA_HEADS_PER_STEP = 4
DSA_GROUPS_PER_PASS = 4
TK_PROMPT, TK_SAMPLE = 512, 384


def _cparams(sem):
    return pltpu.CompilerParams(dimension_semantics=sem, vmem_limit_bytes=VMEM_LIMIT)


def _pick(n, pref):
    if n <= pref:
        return n
    t = pref
    while n % t:
        t //= 2
    return t


def _rope_table(pos, theta, rot_dim):
    half = rot_dim // 2
    inv = theta ** (-jnp.arange(half, dtype=F32) / half)
    ang = pos.astype(F32)[:, None] * inv[None, :]
    cos, sin = jnp.cos(ang), jnp.sin(ang)
    n = pos.shape[0]
    ones = jnp.ones((n, LANES - rot_dim), F32)
    zeros = jnp.zeros((n, LANES - rot_dim), F32)
    zh = jnp.zeros((n, half), F32)
    a = jnp.concatenate([cos, cos, ones], axis=1)
    b = jnp.concatenate([-sin, zh, zeros], axis=1)
    c = jnp.concatenate([zh, sin, zeros], axis=1)
    return jnp.concatenate([a, b, c], axis=1)


def _rope_slab(x, tab, half):
    a = tab[:, 0:LANES]
    b = tab[:, LANES:2 * LANES]
    c = tab[:, 2 * LANES:3 * LANES]
    return x * a + pltpu.roll(x, LANES - half, 1) * b + pltpu.roll(x, half, 1) * c


def _mm_body(*refs, nk, n_extra, n_out, epilogue):
    a_ref, w_ref = refs[0], refs[1]
    extra = refs[2:2 + n_extra]
    outs = refs[2 + n_extra:2 + n_extra + n_out]
    a = a_ref[...].astype(BF16)
    w = w_ref[...].astype(BF16)
    part = jnp.dot(a, w, preferred_element_type=F32)
    if nk == 1:
        epilogue(part, extra, outs)
        return
    acc_ref = refs[2 + n_extra + n_out]
    k = pl.program_id(2)

    @pl.when(k == 0)
    def _():
        acc_ref[...] = part

    @pl.when(k > 0)
    def _():
        acc_ref[...] += part

    @pl.when(k == nk - 1)
    def _():
        epilogue(acc_ref[...], extra, outs)


def _matmul(a, w, *, n, col0, tm, tn, tk, extras, outs, epilogue, name, order="nm"):
    m, kdim = a.shape
    assert col0 % tn == 0
    jb = col0 // tn
    gi, gj, gk = m // tm, n // tn, kdim // tk
    if order == "nm":
        grid = (gj, gi, gk)
        wrap = lambda f: (lambda pj, pi, pk: f(pi, pj, pk))
    else:
        grid = (gi, gj, gk)
        wrap = lambda f: (lambda pi, pj, pk: f(pi, pj, pk))
    in_specs = [
        pl.BlockSpec((tm, tk), wrap(lambda i, j, k: (i, k))),
        pl.BlockSpec((tk, tn), wrap(lambda i, j, k: (k, j + jb))),
    ]
    args = [a, w]
    for arr, blk, fn in extras:
        in_specs.append(pl.BlockSpec(blk, wrap(lambda i, j, k, fn=fn: fn(i, j))))
        args.append(arr)
    out_specs = [pl.BlockSpec(blk, wrap(lambda i, j, k, fn=fn: fn(i, j))) for _, blk, fn in outs]
    out_shape = [s for s, _, _ in outs]
    scratch = [pltpu.VMEM((tm, tn), F32)] if gk > 1 else []
    body = functools.partial(_mm_body, nk=gk, n_extra=len(extras), n_out=len(outs), epilogue=epilogue)
    res = pl.pallas_call(
        body, grid=grid, in_specs=in_specs, out_specs=out_specs, out_shape=out_shape,
        scratch_shapes=scratch, name=name,
        compiler_params=_cparams(("parallel", "parallel", "arbitrary")),
    )(*args)
    return res


def _ep_plain(acc, extra, outs):
    outs[0][...] = acc.astype(outs[0].dtype)


def _ep_rmsnorm(acc, extra, outs):
    g = extra[0][...]
    y = acc * lax.rsqrt(jnp.mean(acc * acc, axis=-1, keepdims=True) + EPS) * g
    outs[0][...] = y.astype(outs[0].dtype)


def _ep_sigmoid(acc, extra, outs):
    outs[0][...] = jax.nn.sigmoid(acc).astype(outs[0].dtype)


def _ep_relu2(acc, extra, outs):
    r = jnp.maximum(acc, 0.0)
    outs[0][...] = (r * r).astype(outs[0].dtype)


def _ep_residual(acc, extra, outs):
    x = extra[0][...]
    gate = extra[1][...]
    outs[0][...] = x + gate * acc


def _ep_rope(acc, extra, outs, *, half, period, roped, scale):
    tab = extra[0][...]
    o = outs[0]
    for s in range(acc.shape[1] // LANES):
        x = acc[:, s * LANES:(s + 1) * LANES]
        if (s % period) in roped:
            x = _rope_slab(x, tab, half)
        if scale != 1.0:
            x = x * scale
        o[:, s * LANES:(s + 1) * LANES] = x.astype(o.dtype)


def _ep_kfull(acc, extra, outs):
    kr = extra[0][...]
    o = outs[0]
    for h in range(acc.shape[1] // LANES):
        o[:, (2 * h) * LANES:(2 * h + 1) * LANES] = acc[:, h * LANES:(h + 1) * LANES].astype(o.dtype)
        o[:, (2 * h + 1) * LANES:(2 * h + 2) * LANES] = kr


TM, TN, TK = 512, 1024, 4096


def _proj(name, a, w, ep, out_dtype, *, cols=None, tm=TM, tn=TN, tk=TK, extras=(), out_cols=None, order="nm"):
    m, k = a.shape
    col0, n = (0, w.shape[1]) if cols is None else cols
    tm, tn, tk = _pick(m, tm), _pick(n, tn), _pick(k, tk)
    oc = n if out_cols is None else out_cols
    outs = [(jax.ShapeDtypeStruct((m, oc), out_dtype), (tm, tn * oc // n), lambda i, j: (i, j))]
    return _matmul(a, w, n=n, col0=col0, tm=tm, tn=tn, tk=tk, extras=list(extras), outs=outs, epilogue=ep,
                   name=name, order=order)[0]


def _ada_body(c_ref, w_ref, b_ref, o_ref):
    c = c_ref[...]
    a = (c * jax.nn.sigmoid(c)).astype(BF16)
    o_ref[...] = jnp.dot(a, w_ref[...].astype(BF16), preferred_element_type=F32) + b_ref[...]


def _ada(c_all, w_ada, b_ada):
    r, d = c_all.shape
    n = w_ada.shape[1]
    tn = _pick(n, 512)
    return pl.pallas_call(
        _ada_body, grid=(n // tn,),
        in_specs=[pl.BlockSpec((r, d), lambda j: (0, 0)),
                  pl.BlockSpec((d, tn), lambda j: (0, j)),
                  pl.BlockSpec((1, tn), lambda j: (0, j))],
        out_specs=pl.BlockSpec((r, tn), lambda j: (0, j)),
        out_shape=jax.ShapeDtypeStruct((r, n), F32), name="adaln",
        compiler_params=_cparams(("parallel",)),
    )(c_all, w_ada, b_ada.reshape(1, n))


def _norm_body(*refs, modulate):
    x = refs[0][...]
    g = refs[1][...]
    y = x * lax.rsqrt(jnp.mean(x * x, axis=-1, keepdims=True) + EPS) * g
    if modulate:
        y = y * (1.0 + refs[2][...]) + refs[3][...]
        o = refs[4]
    else:
        o = refs[2]
    o[...] = y.astype(o.dtype)


def _norm(x, g, sc, sh, out_dtype):
    m, d = x.shape
    tm = _pick(m, 256)
    in_specs = [pl.BlockSpec((tm, d), lambda i: (i, 0)), pl.BlockSpec((1, d), lambda i: (0, 0))]
    args = [x, g.reshape(1, d)]
    if sc is not None:
        per_row = sc.shape[0] == m
        blk = (tm, d) if per_row else (1, d)
        fn = (lambda i: (i, 0)) if per_row else (lambda i: (0, 0))
        in_specs += [pl.BlockSpec(blk, fn), pl.BlockSpec(blk, fn)]
        args += [sc, sh]
    return pl.pallas_call(
        functools.partial(_norm_body, modulate=sc is not None), grid=(m // tm,),
        in_specs=in_specs, out_specs=pl.BlockSpec((tm, d), lambda i: (i, 0)),
        out_shape=jax.ShapeDtypeStruct((m, d), out_dtype), name="rmsnorm",
        compiler_params=_cparams(("parallel",)),
    )(*args)


def _finish_body(z_ref, g_ref, tm_ref, td_ref,
                 ckv_o, ckvb_o, kr_o, krb_o, k_o, kb_o, v_o, vb_o, ki_o, kib_o, wi_o,
                 *, c_lora, r_mla, n_kv, idx_scale):
    tab_m = tm_ref[...]
    tab_d = td_ref[...]
    off = 0
    ckv = z_ref[:, off:off + c_lora]
    ckv = ckv * lax.rsqrt(jnp.mean(ckv * ckv, axis=-1, keepdims=True) + EPS) * g_ref[...]
    ckv_o[...] = ckv
    ckvb_o[...] = ckv.astype(BF16)
    off += c_lora
    kr = _rope_slab(z_ref[:, off:off + LANES], tab_m, r_mla // 2)
    kr_o[...] = kr[:, 0:r_mla]
    krb_o[...] = kr.astype(BF16)
    off += LANES
    for h in range(n_kv):
        kh = _rope_slab(z_ref[:, off + h * LANES:off + (h + 1) * LANES], tab_d, DSA_ROT // 2)
        k_o[:, h * LANES:(h + 1) * LANES] = kh
        kb_o[:, h * LANES:(h + 1) * LANES] = kh.astype(BF16)
    off += n_kv * LANES
    v = z_ref[:, off:off + n_kv * LANES]
    v_o[...] = v
    vb_o[...] = v.astype(BF16)
    off += n_kv * LANES
    ki = _rope_slab(z_ref[:, off:off + LANES], tab_d, IDX_ROT // 2)
    ki_o[...] = ki
    kib_o[...] = ki.astype(BF16)
    off += LANES
    wi_o[...] = z_ref[:, off:off + LANES] * idx_scale


def _finish(z, g_kv, tab_m, tab_d, *, c_lora, r_mla, n_kv, idx_scale):
    m, zc = z.shape
    tm = _pick(m, 256)
    kvw = n_kv * LANES
    row = lambda i: (i, 0)
    shapes = [(c_lora, F32), (c_lora, BF16), (r_mla, F32), (LANES, BF16), (kvw, F32), (kvw, BF16),
              (kvw, F32), (kvw, BF16), (LANES, F32), (LANES, BF16), (LANES, F32)]
    return pl.pallas_call(
        functools.partial(_finish_body, c_lora=c_lora, r_mla=r_mla, n_kv=n_kv, idx_scale=idx_scale),
        grid=(m // tm,),
        in_specs=[pl.BlockSpec((tm, zc), row), pl.BlockSpec((1, c_lora), lambda i: (0, 0)),
                  pl.BlockSpec((tm, 3 * LANES), row), pl.BlockSpec((tm, 3 * LANES), row)],
        out_specs=[pl.BlockSpec((tm, w), row) for w, _ in shapes],
        out_shape=[jax.ShapeDtypeStruct((m, w), dt) for w, dt in shapes], name="finish_small",
        compiler_params=_cparams(("parallel",)),
    )(z, g_kv.reshape(1, c_lora), tab_m, tab_d)


def _row_limits(q_first, tq, l_valid):
    rows = q_first + lax.broadcasted_iota(I32, (tq, 1), 0)
    chunk_end = lambda p: (lax.shift_right_logical(p, CHUNK_LOG2) + 1) * CHUNK
    lim = jnp.minimum(chunk_end(rows), l_valid)
    lim_min = jnp.minimum(chunk_end(q_first), l_valid)
    lim_max = jnp.minimum(chunk_end(q_first + tq - 1), l_valid)
    return lim, lim_min, lim_max


LOG2E = 1.4426950408889634


def _lane_tile(x, width):
    return x if width == LANES else jnp.concatenate([x] * (width // LANES), axis=1)


def _softmax_update(s, v, m_prev, l_prev, acc_prev):
    m_new = jnp.maximum(m_prev, jnp.max(s, axis=1, keepdims=True))
    p = jnp.exp2(s - _lane_tile(m_new, s.shape[1]))
    alpha = jnp.exp2(m_prev - m_new)
    l_new = alpha * l_prev + jnp.sum(p, axis=1, keepdims=True)
    acc_new = _lane_tile(alpha, acc_prev.shape[1]) * acc_prev + jnp.dot(p.astype(BF16), v,
                                                                        preferred_element_type=F32)
    return m_new, l_new, acc_new


def _softmax_step(s, v, m_sc, l_sc, acc_sc):
    m_sc[...], l_sc[...], acc_sc[...] = _softmax_update(s, v, m_sc[...], l_sc[...], acc_sc[...])


_NT = (((1,), (1,)), ((), ()))


def _gate_mix(out_a, ga, gb, out_b):
    return (ga.astype(F32) * out_a + gb.astype(F32) * out_b.astype(F32)).astype(BF16)


def _mla_body(q_ref, k_ref, v_ref, ga_ref, gb_ref, ob_ref, o_ref, m_sc, l_sc, acc_sc, s_sc,
              *, hb, tq, tk, q0, l_valid):
    q_first = q0 + pl.program_id(2) * tq
    lim, lim_min, lim_max = _row_limits(q_first, tq, l_valid)
    n_full = lim_min // tk
    n_all = (lim_max + tk - 1) // tk
    col = lax.broadcasted_iota(I32, (1, tk), 1)
    m_sc[...] = jnp.full(m_sc.shape, NEG, F32)
    l_sc[...] = jnp.zeros(l_sc.shape, F32)
    acc_sc[...] = jnp.zeros(acc_sc.shape, F32)

    def scores(c, j):
        start = pl.multiple_of(c * tk, tk)
        return lax.dot_general(q_ref[:, j * 2 * LANES:(j + 1) * 2 * LANES],
                               k_ref[pl.ds(start, tk), j * 2 * LANES:(j + 1) * 2 * LANES], _NT,
                               preferred_element_type=F32)

    for j in range(hb):
        s_sc[j] = scores(0, j)

    def step(c, masked):
        start = pl.multiple_of(c * tk, tk)
        nxt = jnp.minimum(c + 1, n_all - 1)
        for j in range(hb):
            s = s_sc[j]
            if masked:
                s = jnp.where(col + start < lim, s, NEG)
            _softmax_step(s, v_ref[pl.ds(start, tk), j * LANES:(j + 1) * LANES],
                          m_sc.at[j], l_sc.at[j], acc_sc.at[j])
            s_sc[j] = scores(nxt, j)

    def full_step(c, carry):
        step(c, False)
        return carry

    def masked_step(c, carry):
        step(c, True)
        return carry

    lax.fori_loop(0, n_full, full_step, 0)
    lax.fori_loop(n_full, n_all, masked_step, 0)
    for j in range(hb):
        cols = slice(j * LANES, (j + 1) * LANES)
        o_ref[:, cols] = _gate_mix(acc_sc[j] * (1.0 / l_sc[j]), ga_ref[:, cols], gb_ref[:, cols], ob_ref[:, cols])


def _mla_attention(q_full, k_full, v, gates, out_b, *, n_heads, hb, tq, tk, q0, l_valid):
    b, t, _ = q_full.shape
    lp = k_full.shape[1]
    kv_mode = dict(pipeline_mode=pl.Buffered(1)) if t // tq > 2 else {}
    stat = pltpu.VMEM((hb, tq, LANES), F32)
    return pl.pallas_call(
        functools.partial(_mla_body, hb=hb, tq=tq, tk=tk, q0=q0, l_valid=l_valid),
        grid=(b, n_heads // hb, t // tq),
        in_specs=[pl.BlockSpec((None, tq, hb * 2 * LANES), lambda bi, h, qi: (bi, qi, h)),
                  pl.BlockSpec((None, lp, hb * 2 * LANES), lambda bi, h, qi: (bi, 0, h), **kv_mode),
                  pl.BlockSpec((None, lp, hb * LANES), lambda bi, h, qi: (bi, 0, h), **kv_mode),
                  pl.BlockSpec((None, tq, hb * LANES), lambda bi, h, qi: (bi, qi, h)),
                  pl.BlockSpec((None, tq, hb * LANES), lambda bi, h, qi: (bi, qi, n_heads // hb + h)),
                  pl.BlockSpec((None, tq, hb * LANES), lambda bi, h, qi: (bi, qi, h))],
        out_specs=pl.BlockSpec((None, tq, hb * LANES), lambda bi, h, qi: (bi, qi, h)),
        out_shape=jax.ShapeDtypeStruct((b, t, n_heads * LANES), BF16),
        scratch_shapes=[stat, stat, stat, pltpu.VMEM((hb, tq, tk), F32)],
        name="mla_attention",
        compiler_params=_cparams(("parallel", "parallel", "arbitrary")),
    )(q_full, k_full, v, gates, gates, out_b)


def _absorb_q_body(q_ref, w_ref, o_ref, *, c_lora):
    lat = lax.dot_general(q_ref[:, 0:LANES], w_ref[...], _NT, preferred_element_type=F32)
    o_ref[:, 0:c_lora] = lat.astype(o_ref.dtype)
    o_ref[:, c_lora:c_lora + LANES] = q_ref[:, LANES:2 * LANES]


def _absorb_q(q_full, w_uk, *, n_heads, c_lora):
    m = q_full.shape[0]
    return pl.pallas_call(
        functools.partial(_absorb_q_body, c_lora=c_lora), grid=(n_heads,),
        in_specs=[pl.BlockSpec((m, 2 * LANES), lambda h: (0, h)), pl.BlockSpec((c_lora, LANES), lambda h: (0, h))],
        out_specs=pl.BlockSpec((m, c_lora + LANES), lambda h: (0, h)),
        out_shape=jax.ShapeDtypeStruct((m, n_heads * (c_lora + LANES)), BF16), name="mla_absorb_q",
        compiler_params=_cparams(("parallel",)),
    )(q_full, w_uk)


def _absorb_o_body(o_lat_ref, w_ref, ga_ref, gb_ref, ob_ref, o_ref):
    out_a = jnp.dot(o_lat_ref[...], w_ref[...], preferred_element_type=F32)
    o_ref[...] = _gate_mix(out_a, ga_ref[...], gb_ref[...], ob_ref[...])


def _absorb_o(o_lat, w_uv, gates, out_b, *, n_heads, c_lora):
    m = o_lat.shape[0]
    head = lambda h: (0, h)
    return pl.pallas_call(
        _absorb_o_body, grid=(n_heads,),
        in_specs=[pl.BlockSpec((m, c_lora), head), pl.BlockSpec((c_lora, LANES), head),
                  pl.BlockSpec((m, LANES), head), pl.BlockSpec((m, LANES), lambda h: (0, n_heads + h)),
                  pl.BlockSpec((m, LANES), head)],
        out_specs=pl.BlockSpec((m, LANES), head),
        out_shape=jax.ShapeDtypeStruct((m, n_heads * LANES), BF16), name="mla_absorb_o",
        compiler_params=_cparams(("parallel",)),
    )(o_lat, w_uv, gates, gates, out_b)


def _mla_abs_body(q_ref, kc_ref, o_ref, *, bb, t, n_heads, c_lora, tk, q0, l_valid):
    rows = t * n_heads
    tok = lax.broadcasted_iota(I32, (t, n_heads, 1), 0).reshape(rows, 1)
    lim = jnp.minimum((lax.shift_right_logical(q0 + tok, CHUNK_LOG2) + 1) * CHUNK, l_valid)
    lim_min = min((q0 // CHUNK + 1) * CHUNK, l_valid)
    lim_max = min(((q0 + t - 1) // CHUNK + 1) * CHUNK, l_valid)
    col = lax.broadcasted_iota(I32, (1, tk), 1)
    for j in range(bb):
        q = q_ref[j]
        m = jnp.full((rows, LANES), NEG, F32)
        l = jnp.zeros((rows, LANES), F32)
        acc = jnp.zeros((rows, c_lora), F32)
        for c in range(-(-lim_max // tk)):
            kc = kc_ref[j, c * tk:(c + 1) * tk, :]
            s = lax.dot_general(q, kc, _NT, preferred_element_type=F32)
            if (c + 1) * tk > lim_min:
                s = jnp.where(col + c * tk < lim, s, NEG)
            m, l, acc = _softmax_update(s, kc[:, 0:c_lora], m, l, acc)
        o_ref[j] = (acc * _lane_tile(1.0 / l, c_lora)).astype(o_ref.dtype)


def _mla_abs_attention(q_abs, kc, *, t, n_heads, c_lora, tk, q0, l_valid):
    b, rows, dk = q_abs.shape
    lp = kc.shape[1]
    bb = _pick(b, 4)
    return pl.pallas_call(
        functools.partial(_mla_abs_body, bb=bb, t=t, n_heads=n_heads, c_lora=c_lora, tk=tk, q0=q0, l_valid=l_valid),
        grid=(b // bb,),
        in_specs=[pl.BlockSpec((bb, rows, dk), lambda i: (i, 0, 0)), pl.BlockSpec((bb, lp, dk), lambda i: (i, 0, 0))],
        out_specs=pl.BlockSpec((bb, rows, c_lora), lambda i: (i, 0, 0)),
        out_shape=jax.ShapeDtypeStruct((b, rows, c_lora), BF16), name="mla_absorbed_attention",
        compiler_params=_cparams(("parallel",)),
    )(q_abs, kc)


def _dsa_body(qb_ref, qi_ref, wi_ref, ki_ref, kt_ref, v_ref, o_ref,
              keys_sc, m_sc, l_sc, acc_sc, s_sc, *, tq, tk, q0, l_valid, n_idx, n_kv, rep, topk):
    q_first = q0 + pl.program_id(1) * tq
    lim, _, lim_max = _row_limits(q_first, tq, l_valid)
    n_all = (lim_max + tk - 1) // tk
    col = lax.broadcasted_iota(I32, (1, tk), 1)
    w = wi_ref[...]

    def score_chunk(c, carry):
        start = pl.multiple_of(c * tk, tk)
        kic = ki_ref[pl.ds(start, tk), :]
        s = jnp.zeros((tq, tk), F32)
        for h in range(n_idx):
            y = lax.dot_general(qi_ref[:, h * LANES:(h + 1) * LANES], kic, _NT,
                                preferred_element_type=F32)
            s = s + w[:, h:h + 1] * jnp.maximum(y, 0.0)
        s = jnp.where(s == 0.0, 0.0, s)
        bits = lax.bitcast_convert_type(s, I32)
        key = jnp.where(bits >= 0, bits, bits ^ jnp.int32(0x7FFFFFFF))
        keys_sc[c] = jnp.where(col + start < lim, key, jnp.int32(INT_MIN))
        return carry

    lax.fori_loop(0, n_all, score_chunk, 0)

    def count_rows(indicator):
        def count_chunk(c, acc):
            ind = indicator(c, keys_sc[c])
            part = ind[:, 0:LANES]
            for j in range(1, tk // LANES):
                part = part + ind[:, j * LANES:(j + 1) * LANES]
            return acc + part

        acc = lax.fori_loop(0, n_all, count_chunk, jnp.zeros((tq, LANES), F32))
        return jnp.sum(acc, axis=1, keepdims=True)

    def count_ge(cand):
        return count_rows(lambda c, kk: jnp.where(kk >= cand, 1.0, 0.0))

    def bit_pass(i, t):
        cand = t + lax.shift_left(jnp.int32(1), jnp.int32(31) - i)
        return jnp.where(count_ge(cand) >= topk, cand, t)

    t = lax.fori_loop(0, 32, bit_pass, jnp.full((tq, 1), INT_MIN, I32))
    thr = jnp.maximum(t, jnp.int32(INT_MIN + 1))

    @pl.when(jnp.max(count_ge(thr)) > topk)
    def _():
        slots = topk - count_ge(thr + 1)

        def tied_before(bound):
            return count_rows(lambda c, kk: jnp.where(kk == thr, jnp.where(col + c * tk < bound, 1.0, 0.0), 0.0))

        def index_pass(i, j_last):
            cand = j_last + lax.shift_left(jnp.int32(1), jnp.int32(INDEX_BITS - 1) - i)
            return jnp.where(tied_before(cand) < slots, cand, j_last)

        j_last = lax.fori_loop(0, INDEX_BITS, index_pass, jnp.zeros((tq, 1), I32))

        def demote(c, carry):
            kk = keys_sc[c]
            keys_sc[c] = jnp.where(kk == thr, jnp.where(col + c * tk > j_last, thr - 1, kk), kk)
            return carry

        lax.fori_loop(0, n_all, demote, 0)

    rows = rep * tq
    tok = jnp.bitwise_and(lax.broadcasted_iota(I32, (rows, LANES), 0), tq - 1)
    eye = jnp.where(tok == lax.broadcasted_iota(I32, (rows, LANES), 1), 1.0, 0.0).astype(BF16)

    gb = m_sc.shape[0]
    for g0 in range(0, n_kv, gb):
        m_sc[...] = jnp.full(m_sc.shape, NEG, F32)
        l_sc[...] = jnp.zeros(l_sc.shape, F32)
        acc_sc[...] = jnp.zeros(acc_sc.shape, F32)

        def mask_bias(c):
            bias = jnp.where(keys_sc[c] >= thr, 0.0, NEG).astype(BF16)
            if tq < LANES:
                bias = jnp.concatenate([bias, jnp.zeros((LANES - tq, tk), BF16)], axis=0)
            return bias

        def scores(c, g, bias):
            qs = jnp.concatenate([qb_ref[:, (g * rep + r) * LANES:(g * rep + r + 1) * LANES]
                                  for r in range(rep)], axis=0)
            return jnp.dot(jnp.concatenate([qs, eye], axis=1), jnp.concatenate([kt_ref[g, c], bias], axis=0),
                           preferred_element_type=F32)

        bias0 = mask_bias(0)
        for j in range(gb):
            s_sc[j] = scores(0, g0 + j, bias0)

        def attend_chunk(c, carry, g0=g0, scores=scores, mask_bias=mask_bias):
            start = pl.multiple_of(c * tk, tk)
            nxt = jnp.minimum(c + 1, n_all - 1)
            bias = mask_bias(nxt)
            for j in range(gb):
                g = g0 + j
                _softmax_step(s_sc[j], v_ref[pl.ds(start, tk), g * LANES:(g + 1) * LANES],
                              m_sc.at[j], l_sc.at[j], acc_sc.at[j])
                s_sc[j] = scores(nxt, g, bias)
            return carry

        lax.fori_loop(0, n_all, attend_chunk, 0)
        for j in range(gb):
            o = acc_sc[j] * (1.0 / l_sc[j])
            for r in range(rep):
                h = (g0 + j) * rep + r
                o_ref[:, h * LANES:(h + 1) * LANES] = o[r * tq:(r + 1) * tq].astype(o_ref.dtype)


def _dsa_attention(qb, qi, wi, ki, k, v, *, n_kv, rep, n_idx, tq, tk, q0, l_valid, topk):
    b, t, _ = qb.shape
    lp = k.shape[1]
    kvw = n_kv * LANES
    assert tq <= LANES and tq & (tq - 1) == 0, "the token one-hot needs a power-of-two block of at most 128 rows"
    kt = k.reshape(b, lp // tk, tk, n_kv, LANES).transpose(0, 3, 1, 4, 2)
    kv_mode = dict(pipeline_mode=pl.Buffered(1)) if t // tq > 2 else {}
    stat = pltpu.VMEM((min(DSA_GROUPS_PER_PASS, n_kv), rep * tq, LANES), F32)
    return pl.pallas_call(
        functools.partial(_dsa_body, tq=tq, tk=tk, q0=q0, l_valid=l_valid, n_idx=n_idx, n_kv=n_kv, rep=rep,
                          topk=topk),
        grid=(b, t // tq),
        in_specs=[pl.BlockSpec((None, tq, n_kv * rep * LANES), lambda bi, qb_: (bi, qb_, 0)),
                  pl.BlockSpec((None, tq, n_idx * LANES), lambda bi, qb_: (bi, qb_, 0)),
                  pl.BlockSpec((None, tq, LANES), lambda bi, qb_: (bi, qb_, 0)),
                  pl.BlockSpec((None, lp, LANES), lambda bi, qb_: (bi, 0, 0), **kv_mode),
                  pl.BlockSpec((None, n_kv, lp // tk, LANES, tk), lambda bi, qb_: (bi, 0, 0, 0, 0), **kv_mode),
                  pl.BlockSpec((None, lp, kvw), lambda bi, qb_: (bi, 0, 0), **kv_mode)],
        out_specs=pl.BlockSpec((None, tq, n_kv * rep * LANES), lambda bi, qb_: (bi, qb_, 0)),
        out_shape=jax.ShapeDtypeStruct((b, t, n_kv * rep * LANES), BF16),
        scratch_shapes=[pltpu.VMEM((lp // tk, tq, tk), I32), stat, stat, stat,
                        pltpu.VMEM((stat.shape[0], rep * tq, tk), F32)],
        name="dsa_attention",
        compiler_params=pltpu.CompilerParams(dimension_semantics=("parallel", "arbitrary"),
                                             vmem_limit_bytes=VMEM_LIMIT_DSA),
    )(qb, qi, wi, ki, kt, v)


def _pad_cols(w, width):
    return jnp.pad(w, ((0, 0), (0, width - w.shape[1])))


def _prep_weights(w_in, w_uq, w_uk, w_uv, w_out, w_up, w_down, dims):
    d, q_lora, c_lora, r_mla, n_heads, n_kv, n_idx = dims
    splits = (q_lora, c_lora, r_mla, n_heads * HEAD_DIM, n_kv * HEAD_DIM, n_kv * HEAD_DIM,
              n_idx * HEAD_DIM, HEAD_DIM, n_idx, d, d)
    offs = np.concatenate([[0], np.cumsum(splits)])
    sec = [w_in[:, int(offs[i]):int(offs[i + 1])] for i in range(len(splits))]
    small = jnp.concatenate([sec[1], _pad_cols(sec[2], LANES), sec[4], sec[5], sec[7],
                             _pad_cols(sec[8], LANES)], axis=1)
    small = _pad_cols(small, -(-small.shape[1] // 512) * 512)
    parts = [("q", sec[0]), ("small", small), ("qb", sec[3]), ("qi", sec[6]),
             ("gates", jnp.concatenate([sec[9], sec[10]], axis=1))]
    windows, off = {}, 0
    for i, (key, p) in enumerate(parts):
        windows[key] = (off, p.shape[1])
        parts[i] = (key, _pad_cols(p, -(-p.shape[1] // TN) * TN))
        off += parts[i][1].shape[1]
    wq = w_uq.reshape(q_lora, n_heads, HEAD_DIM + r_mla)
    wq = jnp.pad(wq, ((0, 0), (0, 0), (0, 2 * LANES - HEAD_DIM - r_mla)))
    return dict(
        w_in=jnp.concatenate([p for _, p in parts], axis=1).astype(BF16), win=windows,
        w_uq=wq.reshape(q_lora, n_heads * 2 * LANES).astype(BF16),
        w_uk=w_uk.reshape(c_lora, n_heads * HEAD_DIM).astype(BF16),
        w_uv=w_uv.reshape(c_lora, n_heads * HEAD_DIM).astype(BF16),
        w_out=w_out.astype(BF16), w_up=w_up.astype(BF16), w_down=w_down.astype(BF16))


def _layer(x, ada, past, q0, pw, g_norm1, g_q_lora, g_kv_lora, g_norm2, dims, tq_mla, tq_dsa, tk):
    d, q_lora, c_lora, r_mla, n_heads, n_kv, n_idx = dims
    b, t, _ = x.shape
    m = b * t
    xf = x.reshape(m, d)
    if ada.shape[0] == 1:
        mods = [ada[:, i * d:(i + 1) * d] for i in range(N_ADA)]
    else:
        mods = [jnp.repeat(ada[:, i * d:(i + 1) * d], t, axis=0) for i in range(N_ADA)]
    sh1, sc1, gt1, sh2, sc2, gt2 = mods
    per_row = ada.shape[0] != 1

    pos = q0 + jnp.arange(t, dtype=I32)
    tab_m = jnp.tile(_rope_table(pos, MLA_THETA, r_mla), (b, 1))
    tab_d = jnp.tile(_rope_table(pos, ROPE_THETA, DSA_ROT), (b, 1))

    h = _norm(xf, g_norm1, sc1, sh1, BF16)

    def tab_extra(tab, tm):
        return (tab, (tm, 3 * LANES), lambda i, j: (i, 0))

    tm = _pick(m, 512)
    qn = _proj("proj_qlat", h, pw["w_in"], _ep_rmsnorm, BF16, cols=pw["win"]["q"], tn=q_lora,
                     extras=[(g_q_lora.reshape(1, q_lora), (1, q_lora), lambda i, j: (0, 0))])
    z_small = _proj("proj_small", h, pw["w_in"], _ep_plain, F32, cols=pw["win"]["small"], tn=512)
    q_dsa = _proj("proj_qdsa", h, pw["w_in"],
                  functools.partial(_ep_rope, half=DSA_ROT // 2, period=1, roped=(0,),
                                    scale=HEAD_DIM ** -0.5 * LOG2E),
                  BF16, cols=pw["win"]["qb"], extras=[tab_extra(tab_d, tm)])
    q_idx = _proj("proj_qidx", h, pw["w_in"],
                  functools.partial(_ep_rope, half=IDX_ROT // 2, period=1, roped=(0,), scale=1.0),
                  BF16, cols=pw["win"]["qi"], extras=[tab_extra(tab_d, tm)])
    gates = _proj("proj_gates", h, pw["w_in"], _ep_sigmoid, BF16, cols=pw["win"]["gates"])

    (ckv, ckv_b, krope, krope_b, k_new, k_new_b, v_new, v_new_b, ki_new, ki_new_b, wi) = _finish(
        z_small, g_kv_lora, tab_m, tab_d, c_lora=c_lora, r_mla=r_mla, n_kv=n_kv, idx_scale=n_idx ** -0.5)

    kvw = n_kv * HEAD_DIM
    if past is None:
        l_valid = t
        lp = t
        ckv_all, krope_all, k_all, v_all, ki_all = ckv_b, krope_b, k_new_b, v_new_b, ki_new_b
    else:
        p_ckv, p_krope, p_k, p_v, p_ki = past
        l_valid = p_ckv.shape[1] + t
        lp = -(-l_valid // tk) * tk

        def cat(p, new, width):
            p = p.reshape(b, p.shape[1], -1).astype(BF16)
            if p.shape[2] < width:
                p = jnp.pad(p, ((0, 0), (0, 0), (0, width - p.shape[2])))
            full = jnp.concatenate([p, new.reshape(b, t, width)], axis=1)
            return jnp.pad(full, ((0, 0), (0, lp - l_valid), (0, 0))).reshape(b * lp, width)

        ckv_all = cat(p_ckv, ckv_b, c_lora)
        krope_all = cat(p_krope, krope_b, LANES)
        k_all = cat(p_k, k_new_b, kvw)
        v_all = cat(p_v, v_new_b, kvw)
        ki_all = cat(p_ki, ki_new_b, LANES)

    out_b = _dsa_attention(q_dsa.reshape(b, t, -1), q_idx.reshape(b, t, -1), wi.reshape(b, t, LANES),
                           ki_all.reshape(b, lp, LANES), k_all.reshape(b, lp, kvw), v_all.reshape(b, lp, kvw),
                           n_kv=n_kv, rep=n_heads // n_kv, n_idx=n_idx, tq=min(tq_dsa, t), tk=min(tk, lp),
                           q0=q0, l_valid=l_valid, topk=min(TOPK_MAX, l_valid // 4))

    q_full = _proj("mla_q_up", qn, pw["w_uq"],
                         functools.partial(_ep_rope, half=r_mla // 2, period=2, roped=(1,),
                                           scale=(HEAD_DIM + r_mla) ** -0.5 * LOG2E),
                         BF16, extras=[tab_extra(tab_m, tm)])
    if past is not None and t * n_heads <= 1024 and n_heads % 8 == 0:
        kc = jnp.concatenate([ckv_all.reshape(b, lp, c_lora), krope_all.reshape(b, lp, LANES)], axis=2)
        q_abs = _absorb_q(q_full, pw["w_uk"], n_heads=n_heads, c_lora=c_lora)
        o_lat = _mla_abs_attention(q_abs.reshape(b, t * n_heads, c_lora + LANES), kc, t=t, n_heads=n_heads,
                                   c_lora=c_lora, tk=min(tk, lp), q0=q0, l_valid=l_valid)
        merged = _absorb_o(o_lat.reshape(m, n_heads * c_lora), pw["w_uv"], gates, out_b.reshape(m, d),
                           n_heads=n_heads, c_lora=c_lora)
    else:
        tml = _pick(b * lp, 512)
        k_full = _proj("mla_k_up", ckv_all, pw["w_uk"], _ep_kfull, BF16, out_cols=2 * n_heads * HEAD_DIM,
                       extras=[(krope_all, (tml, LANES), lambda i, j: (i, 0))])
        v_mla = _proj("mla_v_up", ckv_all, pw["w_uv"], _ep_plain, BF16)
        merged = _mla_attention(q_full.reshape(b, t, -1), k_full.reshape(b, lp, -1), v_mla.reshape(b, lp, -1),
                                gates.reshape(b, t, 2 * d), out_b,
                                n_heads=n_heads, hb=min(MLA_HEADS_PER_STEP, n_heads), tq=min(tq_mla, t),
                                tk=min(tk, lp), q0=q0, l_valid=l_valid).reshape(m, d)

    def gate_extra(gate, tm_, tn_):
        if per_row:
            return (gate, (tm_, tn_), lambda i, j: (i, j))
        return (gate, (1, tn_), lambda i, j: (0, j))

    tn = _pick(d, 1024)
    x1 = _proj("out_proj", merged, pw["w_out"], _ep_residual, F32,
                     extras=[(xf, (tm, tn), lambda i, j: (i, j)), gate_extra(gt1, tm, tn)])
    h2 = _norm(x1, g_norm2, sc2, sh2, BF16)
    u = _proj("mlp_up", h2, pw["w_up"], _ep_relu2, BF16)
    tm2 = _pick(m, 1024)
    x2 = _proj("mlp_down", u, pw["w_down"], _ep_residual, F32, tm=tm2, tk=2048,
               extras=[(x1, (tm2, tn), lambda i, j: (i, j)), gate_extra(gt2, tm2, tn)])
    rows = (ckv.reshape(b, t, c_lora), krope.reshape(b, t, r_mla), k_new.reshape(b, t, n_kv, HEAD_DIM),
            v_new.reshape(b, t, n_kv, HEAD_DIM), ki_new.reshape(b, t, HEAD_DIM))
    return x2, rows


def kernel(x_prompt, x_sample, c_prompt, c_sample, cache_mla_ckv, cache_mla_krope, cache_dsa_k, cache_dsa_v, cache_idx_k, w_ada, b_ada, g_norm1, w_in, g_q_lora, w_uq, g_kv_lora, w_uk, w_uv, w_out, g_norm2, w_up, w_down, g_final):
    depth = w_in.shape[0]
    bp, tp, d = x_prompt.shape
    bs, ts, _ = x_sample.shape
    n_heads = d // HEAD_DIM
    dims = (d, w_uq.shape[1], cache_mla_ckv.shape[-1], cache_mla_krope.shape[-1], n_heads,
            cache_dsa_k.shape[3], n_heads)
    past_len = cache_mla_ckv.shape[2]

    c_all = jnp.concatenate([c_prompt, c_sample], axis=0)
    n_c = c_all.shape[0]
    c_all = jnp.pad(c_all, ((0, -(-n_c // 16) * 16 - n_c), (0, 0)))

    xp, xs = x_prompt, x_sample
    rows_p, rows_s = [], []
    for l in range(depth):
        ada = _ada(c_all, w_ada[l], b_ada[l])
        pw = _prep_weights(w_in[l], w_uq[l], w_uk[l], w_uv[l], w_out[l], w_up[l], w_down[l], dims)
        norms = (g_norm1[l], g_q_lora[l], g_kv_lora[l], g_norm2[l])
        xp2, rp = _layer(xp, ada[0:bp], None, 0, pw, *norms, dims, tq_mla=TQ_MLA, tq_dsa=TQ_DSA, tk=TK_PROMPT)
        past = (cache_mla_ckv[l], cache_mla_krope[l], cache_dsa_k[l], cache_dsa_v[l], cache_idx_k[l])
        xs2, rs = _layer(xs, ada[bp:bp + bs], past, past_len, pw, *norms, dims, tq_mla=TQ_MLA, tq_dsa=TQ_DSA,
                         tk=TK_SAMPLE)
        xp, xs = xp2.reshape(bp, tp, d), xs2.reshape(bs, ts, d)
        rows_p.append(rp)
        rows_s.append(rs)

    y_prompt = _norm(xp.reshape(bp * tp, d), g_final, None, None, F32).reshape(bp, tp, d)
    y_sample = _norm(xs.reshape(bs * ts, d), g_final, None, None, F32).reshape(bs, ts, d)
    stack = lambda rows, i: jnp.stack([r[i] for r in rows])
    return (y_prompt, y_sample,
            stack(rows_p, 0), stack(rows_p, 1), stack(rows_p, 2), stack(rows_p, 3), stack(rows_p, 4),
            stack(rows_s, 0), stack(rows_s, 1), stack(rows_s, 2), stack(rows_s, 3), stack(rows_s, 4))
```

```python
import functools

import numpy as np
import jax
import jax.numpy as jnp
from jax import lax
from jax.experimental import pallas as pl
from jax.experimental.pallas import tpu as pltpu

F32 = jnp.float32
BF16 = jnp.bfloat16
I32 = jnp.int32

LANES = 128
CHUNK = 64
CHUNK_LOG2 = 6
EPS = 1e-6
NEG = -2.0 ** 100
INT_MIN = -2 ** 31
INDEX_BITS = 24
MLA_THETA = 10000.0
ROPE_THETA = 500000.0
HEAD_DIM = 128
DSA_ROT = HEAD_DIM // 4
IDX_ROT = HEAD_DIM // 4
TOPK_MAX = 256
N_ADA = 6
VMEM_LIMIT = 56 * 1024 * 1024
VMEM_LIMIT_DSA = 60 * 1024 * 1024
TQ_MLA, TQ_DSA = 512, 128
MLA_HEADS_PER_STEP = 4
DSA_GROUPS_PER_PASS = 4
TK_PROMPT, TK_SAMPLE = 512, 384


def _cparams(sem):
    return pltpu.CompilerParams(dimension_semantics=sem, vmem_limit_bytes=VMEM_LIMIT)


def _pick(n, pref):
    if n <= pref:
        return n
    t = pref
    while n % t:
        t //= 2
    return t


def _rope_table(pos, theta, rot_dim):
    half = rot_dim // 2
    inv = theta ** (-jnp.arange(half, dtype=F32) / half)
    ang = pos.astype(F32)[:, None] * inv[None, :]
    cos, sin = jnp.cos(ang), jnp.sin(ang)
    n = pos.shape[0]
    ones = jnp.ones((n, LANES - rot_dim), F32)
    zeros = jnp.zeros((n, LANES - rot_dim), F32)
    zh = jnp.zeros((n, half), F32)
    a = jnp.concatenate([cos, cos, ones], axis=1)
    b = jnp.concatenate([-sin, zh, zeros], axis=1)
    c = jnp.concatenate([zh, sin, zeros], axis=1)
    return jnp.concatenate([a, b, c], axis=1)


def _rope_slab(x, tab, half):
    a = tab[:, 0:LANES]
    b = tab[:, LANES:2 * LANES]
    c = tab[:, 2 * LANES:3 * LANES]
    return x * a + pltpu.roll(x, LANES - half, 1) * b + pltpu.roll(x, half, 1) * c


def _mm_body(*refs, nk, n_extra, n_out, epilogue):
    a_ref, w_ref = refs[0], refs[1]
    extra = refs[2:2 + n_extra]
    outs = refs[2 + n_extra:2 + n_extra + n_out]
    a = a_ref[...].astype(BF16)
    w = w_ref[...].astype(BF16)
    part = jnp.dot(a, w, preferred_element_type=F32)
    if nk == 1:
        epilogue(part, extra, outs)
        return
    acc_ref = refs[2 + n_extra + n_out]
    k = pl.program_id(2)

    @pl.when(k == 0)
    def _():
        acc_ref[...] = part

    @pl.when(k > 0)
    def _():
        acc_ref[...] += part

    @pl.when(k == nk - 1)
    def _():
        epilogue(acc_ref[...], extra, outs)


def _matmul(a, w, *, n, col0, tm, tn, tk, extras, outs, epilogue, name, order="nm"):
    m, kdim = a.shape
    assert col0 % tn == 0
    jb = col0 // tn
    gi, gj, gk = m // tm, n // tn, kdim // tk
    if order == "nm":
        grid = (gj, gi, gk)
        wrap = lambda f: (lambda pj, pi, pk: f(pi, pj, pk))
    else:
        grid = (gi, gj, gk)
        wrap = lambda f: (lambda pi, pj, pk: f(pi, pj, pk))
    in_specs = [
        pl.BlockSpec((tm, tk), wrap(lambda i, j, k: (i, k))),
        pl.BlockSpec((tk, tn), wrap(lambda i, j, k: (k, j + jb))),
    ]
    args = [a, w]
    for arr, blk, fn in extras:
        in_specs.append(pl.BlockSpec(blk, wrap(lambda i, j, k, fn=fn: fn(i, j))))
        args.append(arr)
    out_specs = [pl.BlockSpec(blk, wrap(lambda i, j, k, fn=fn: fn(i, j))) for _, blk, fn in outs]
    out_shape = [s for s, _, _ in outs]
    scratch = [pltpu.VMEM((tm, tn), F32)] if gk > 1 else []
    body = functools.partial(_mm_body, nk=gk, n_extra=len(extras), n_out=len(outs), epilogue=epilogue)
    res = pl.pallas_call(
        body, grid=grid, in_specs=in_specs, out_specs=out_specs, out_shape=out_shape,
        scratch_shapes=scratch, name=name,
        compiler_params=_cparams(("parallel", "parallel", "arbitrary")),
    )(*args)
    return res


def _ep_plain(acc, extra, outs):
    outs[0][...] = acc.astype(outs[0].dtype)


def _ep_rmsnorm(acc, extra, outs):
    g = extra[0][...]
    y = acc * lax.rsqrt(jnp.mean(acc * acc, axis=-1, keepdims=True) + EPS) * g
    outs[0][...] = y.astype(outs[0].dtype)


def _ep_sigmoid(acc, extra, outs):
    outs[0][...] = jax.nn.sigmoid(acc).astype(outs[0].dtype)


def _ep_relu2(acc, extra, outs):
    r = jnp.maximum(acc, 0.0)
    outs[0][...] = (r * r).astype(outs[0].dtype)


def _ep_residual(acc, extra, outs):
    x = extra[0][...]
    gate = extra[1][...]
    outs[0][...] = x + gate * acc


def _ep_rope(acc, extra, outs, *, half, period, roped, scale):
    tab = extra[0][...]
    o = outs[0]
    for s in range(acc.shape[1] // LANES):
        x = acc[:, s * LANES:(s + 1) * LANES]
        if (s % period) in roped:
            x = _rope_slab(x, tab, half)
        if scale != 1.0:
            x = x * scale
        o[:, s * LANES:(s + 1) * LANES] = x.astype(o.dtype)


def _ep_kfull(acc, extra, outs):
    kr = extra[0][...]
    o = outs[0]
    for h in range(acc.shape[1] // LANES):
        o[:, (2 * h) * LANES:(2 * h + 1) * LANES] = acc[:, h * LANES:(h + 1) * LANES].astype(o.dtype)
        o[:, (2 * h + 1) * LANES:(2 * h + 2) * LANES] = kr


TM, TN, TK = 512, 1024, 4096


def _proj(name, a, w, ep, out_dtype, *, cols=None, tm=TM, tn=TN, tk=TK, extras=(), out_cols=None, order="nm"):
    m, k = a.shape
    col0, n = (0, w.shape[1]) if cols is None else cols
    tm, tn, tk = _pick(m, tm), _pick(n, tn), _pick(k, tk)
    oc = n if out_cols is None else out_cols
    outs = [(jax.ShapeDtypeStruct((m, oc), out_dtype), (tm, tn * oc // n), lambda i, j: (i, j))]
    return _matmul(a, w, n=n, col0=col0, tm=tm, tn=tn, tk=tk, extras=list(extras), outs=outs, epilogue=ep,
                   name=name, order=order)[0]


def _ada_body(c_ref, w_ref, b_ref, o_ref):
    c = c_ref[...]
    a = (c * jax.nn.sigmoid(c)).astype(BF16)
    o_ref[...] = jnp.dot(a, w_ref[...].astype(BF16), preferred_element_type=F32) + b_ref[...]


def _ada(c_all, w_ada, b_ada):
    r, d = c_all.shape
    n = w_ada.shape[1]
    tn = _pick(n, 512)
    return pl.pallas_call(
        _ada_body, grid=(n // tn,),
        in_specs=[pl.BlockSpec((r, d), lambda j: (0, 0)),
                  pl.BlockSpec((d, tn), lambda j: (0, j)),
                  pl.BlockSpec((1, tn), lambda j: (0, j))],
        out_specs=pl.BlockSpec((r, tn), lambda j: (0, j)),
        out_shape=jax.ShapeDtypeStruct((r, n), F32), name="adaln",
        compiler_params=_cparams(("parallel",)),
    )(c_all, w_ada, b_ada.reshape(1, n))


def _norm_body(*refs, modulate):
    x = refs[0][...]
    g = refs[1][...]
    y = x * lax.rsqrt(jnp.mean(x * x, axis=-1, keepdims=True) + EPS) * g
    if modulate:
        y = y * (1.0 + refs[2][...]) + refs[3][...]
        o = refs[4]
    else:
        o = refs[2]
    o[...] = y.astype(o.dtype)


def _norm(x, g, sc, sh, out_dtype):
    m, d = x.shape
    tm = _pick(m, 256)
    in_specs = [pl.BlockSpec((tm, d), lambda i: (i, 0)), pl.BlockSpec((1, d), lambda i: (0, 0))]
    args = [x, g.reshape(1, d)]
    if sc is not None:
        per_row = sc.shape[0] == m
        blk = (tm, d) if per_row else (1, d)
        fn = (lambda i: (i, 0)) if per_row else (lambda i: (0, 0))
        in_specs += [pl.BlockSpec(blk, fn), pl.BlockSpec(blk, fn)]
        args += [sc, sh]
    return pl.pallas_call(
        functools.partial(_norm_body, modulate=sc is not None), grid=(m // tm,),
        in_specs=in_specs, out_specs=pl.BlockSpec((tm, d), lambda i: (i, 0)),
        out_shape=jax.ShapeDtypeStruct((m, d), out_dtype), name="rmsnorm",
        compiler_params=_cparams(("parallel",)),
    )(*args)


def _finish_body(z_ref, g_ref, tm_ref, td_ref,
                 ckv_o, ckvb_o, kr_o, krb_o, k_o, kb_o, v_o, vb_o, ki_o, kib_o, wi_o,
                 *, c_lora, r_mla, n_kv, idx_scale):
    tab_m = tm_ref[...]
    tab_d = td_ref[...]
    off = 0
    ckv = z_ref[:, off:off + c_lora]
    ckv = ckv * lax.rsqrt(jnp.mean(ckv * ckv, axis=-1, keepdims=True) + EPS) * g_ref[...]
    ckv_o[...] = ckv
    ckvb_o[...] = ckv.astype(BF16)
    off += c_lora
    kr = _rope_slab(z_ref[:, off:off + LANES], tab_m, r_mla // 2)
    kr_o[...] = kr[:, 0:r_mla]
    krb_o[...] = kr.astype(BF16)
    off += LANES
    for h in range(n_kv):
        kh = _rope_slab(z_ref[:, off + h * LANES:off + (h + 1) * LANES], tab_d, DSA_ROT // 2)
        k_o[:, h * LANES:(h + 1) * LANES] = kh
        kb_o[:, h * LANES:(h + 1) * LANES] = kh.astype(BF16)
    off += n_kv * LANES
    v = z_ref[:, off:off + n_kv * LANES]
    v_o[...] = v
    vb_o[...] = v.astype(BF16)
    off += n_kv * LANES
    ki = _rope_slab(z_ref[:, off:off + LANES], tab_d, IDX_ROT // 2)
    ki_o[...] = ki
    kib_o[...] = ki.astype(BF16)
    off += LANES
    wi_o[...] = z_ref[:, off:off + LANES] * idx_scale


def _finish(z, g_kv, tab_m, tab_d, *, c_lora, r_mla, n_kv, idx_scale):
    m, zc = z.shape
    tm = _pick(m, 256)
    kvw = n_kv * LANES
    row = lambda i: (i, 0)
    shapes = [(c_lora, F32), (c_lora, BF16), (r_mla, F32), (LANES, BF16), (kvw, F32), (kvw, BF16),
              (kvw, F32), (kvw, BF16), (LANES, F32), (LANES, BF16), (LANES, F32)]
    return pl.pallas_call(
        functools.partial(_finish_body, c_lora=c_lora, r_mla=r_mla, n_kv=n_kv, idx_scale=idx_scale),
        grid=(m // tm,),
        in_specs=[pl.BlockSpec((tm, zc), row), pl.BlockSpec((1, c_lora), lambda i: (0, 0)),
                  pl.BlockSpec((tm, 3 * LANES), row), pl.BlockSpec((tm, 3 * LANES), row)],
        out_specs=[pl.BlockSpec((tm, w), row) for w, _ in shapes],
        out_shape=[jax.ShapeDtypeStruct((m, w), dt) for w, dt in shapes], name="finish_small",
        compiler_params=_cparams(("parallel",)),
    )(z, g_kv.reshape(1, c_lora), tab_m, tab_d)


def _row_limits(q_first, tq, l_valid):
    rows = q_first + lax.broadcasted_iota(I32, (tq, 1), 0)
    chunk_end = lambda p: (lax.shift_right_logical(p, CHUNK_LOG2) + 1) * CHUNK
    lim = jnp.minimum(chunk_end(rows), l_valid)
    lim_min = jnp.minimum(chunk_end(q_first), l_valid)
    lim_max = jnp.minimum(chunk_end(q_first + tq - 1), l_valid)
    return lim, lim_min, lim_max


LOG2E = 1.4426950408889634


def _lane_tile(x, width):
    return x if width == LANES else jnp.concatenate([x] * (width // LANES), axis=1)


def _softmax_update(s, v, m_prev, l_prev, acc_prev):
    m_new = jnp.maximum(m_prev, jnp.max(s, axis=1, keepdims=True))
    p = jnp.exp2(s - _lane_tile(m_new, s.shape[1]))
    alpha = jnp.exp2(m_prev - m_new)
    l_new = alpha * l_prev + jnp.sum(p, axis=1, keepdims=True)
    acc_new = _lane_tile(alpha, acc_prev.shape[1]) * acc_prev + jnp.dot(p.astype(BF16), v,
                                                                        preferred_element_type=F32)
    return m_new, l_new, acc_new


def _softmax_step(s, v, m_sc, acc_sc):
    sb = s.astype(BF16)
    m_prev = m_sc[...]
    m_new = jnp.maximum(m_prev, jnp.max(sb, axis=1, keepdims=True).astype(F32))
    p = jnp.exp2(sb - _lane_tile(m_new.astype(BF16), s.shape[1]))
    alpha = jnp.exp2(m_prev - m_new)
    v_ones = jnp.concatenate([v, jnp.ones(v.shape, BF16)], axis=1)
    acc_sc[...] = _lane_tile(alpha, 2 * LANES) * acc_sc[...] + jnp.dot(p, v_ones, preferred_element_type=F32)
    m_sc[...] = m_new


def _softmax_result(acc_sc):
    acc = acc_sc[...]
    return acc[:, 0:LANES] * (1.0 / acc[:, LANES:2 * LANES])


_NT = (((1,), (1,)), ((), ()))


def _gate_mix(out_a, ga, gb, out_b):
    return (ga.astype(F32) * out_a + gb.astype(F32) * out_b.astype(F32)).astype(BF16)


def _mla_body(q_ref, k_ref, v_ref, ga_ref, gb_ref, ob_ref, o_ref, m_sc, acc_sc, s_sc,
              *, hb, tq, tk, q0, l_valid):
    q_first = q0 + pl.program_id(2) * tq
    lim, lim_min, lim_max = _row_limits(q_first, tq, l_valid)
    n_full = lim_min // tk
    n_all = (lim_max + tk - 1) // tk
    col = lax.broadcasted_iota(I32, (1, tk), 1)
    m_sc[...] = jnp.full(m_sc.shape, NEG, F32)
    acc_sc[...] = jnp.zeros(acc_sc.shape, F32)

    def scores(c, j):
        start = pl.multiple_of(c * tk, tk)
        return lax.dot_general(q_ref[:, j * 2 * LANES:(j + 1) * 2 * LANES],
                               k_ref[pl.ds(start, tk), j * 2 * LANES:(j + 1) * 2 * LANES], _NT,
                               preferred_element_type=F32)

    for j in range(hb):
        s_sc[j] = scores(0, j)

    def step(c, masked):
        start = pl.multiple_of(c * tk, tk)
        nxt = jnp.minimum(c + 1, n_all - 1)
        for j in range(hb):
            s = s_sc[j]
            if masked:
                s = jnp.where(col + start < lim, s, NEG)
            _softmax_step(s, v_ref[pl.ds(start, tk), j * LANES:(j + 1) * LANES], m_sc.at[j], acc_sc.at[j])
            s_sc[j] = scores(nxt, j)

    def full_step(c, carry):
        step(c, False)
        return carry

    def masked_step(c, carry):
        step(c, True)
        return carry

    lax.fori_loop(0, n_full, full_step, 0)
    lax.fori_loop(n_full, n_all, masked_step, 0)
    for j in range(hb):
        cols = slice(j * LANES, (j + 1) * LANES)
        o_ref[:, cols] = _gate_mix(_softmax_result(acc_sc.at[j]), ga_ref[:, cols], gb_ref[:, cols], ob_ref[:, cols])


def _mla_attention(q_full, k_full, v, gates, out_b, *, n_heads, hb, tq, tk, q0, l_valid):
    b, t, _ = q_full.shape
    lp = k_full.shape[1]
    kv_mode = dict(pipeline_mode=pl.Buffered(1)) if t // tq > 2 else {}
    stat = pltpu.VMEM((hb, tq, LANES), F32)
    return pl.pallas_call(
        functools.partial(_mla_body, hb=hb, tq=tq, tk=tk, q0=q0, l_valid=l_valid),
        grid=(b, n_heads // hb, t // tq),
        in_specs=[pl.BlockSpec((None, tq, hb * 2 * LANES), lambda bi, h, qi: (bi, qi, h)),
                  pl.BlockSpec((None, lp, hb * 2 * LANES), lambda bi, h, qi: (bi, 0, h), **kv_mode),
                  pl.BlockSpec((None, lp, hb * LANES), lambda bi, h, qi: (bi, 0, h), **kv_mode),
                  pl.BlockSpec((None, tq, hb * LANES), lambda bi, h, qi: (bi, qi, h)),
                  pl.BlockSpec((None, tq, hb * LANES), lambda bi, h, qi: (bi, qi, n_heads // hb + h)),
                  pl.BlockSpec((None, tq, hb * LANES), lambda bi, h, qi: (bi, qi, h))],
        out_specs=pl.BlockSpec((None, tq, hb * LANES), lambda bi, h, qi: (bi, qi, h)),
        out_shape=jax.ShapeDtypeStruct((b, t, n_heads * LANES), BF16),
        scratch_shapes=[stat, pltpu.VMEM((hb, tq, 2 * LANES), F32), pltpu.VMEM((hb, tq, tk), F32)],
        name="mla_attention",
        compiler_params=_cparams(("parallel", "parallel", "arbitrary")),
    )(q_full, k_full, v, gates, gates, out_b)


def _absorb_q_body(q_ref, w_ref, o_ref, *, c_lora):
    lat = lax.dot_general(q_ref[:, 0:LANES], w_ref[...], _NT, preferred_element_type=F32)
    o_ref[:, 0:c_lora] = lat.astype(o_ref.dtype)
    o_ref[:, c_lora:c_lora + LANES] = q_ref[:, LANES:2 * LANES]


def _absorb_q(q_full, w_uk, *, n_heads, c_lora):
    m = q_full.shape[0]
    return pl.pallas_call(
        functools.partial(_absorb_q_body, c_lora=c_lora), grid=(n_heads,),
        in_specs=[pl.BlockSpec((m, 2 * LANES), lambda h: (0, h)), pl.BlockSpec((c_lora, LANES), lambda h: (0, h))],
        out_specs=pl.BlockSpec((m, c_lora + LANES), lambda h: (0, h)),
        out_shape=jax.ShapeDtypeStruct((m, n_heads * (c_lora + LANES)), BF16), name="mla_absorb_q",
        compiler_params=_cparams(("parallel",)),
    )(q_full, w_uk)


def _absorb_o_body(o_lat_ref, w_ref, ga_ref, gb_ref, ob_ref, o_ref):
    out_a = jnp.dot(o_lat_ref[...], w_ref[...], preferred_element_type=F32)
    o_ref[...] = _gate_mix(out_a, ga_ref[...], gb_ref[...], ob_ref[...])


def _absorb_o(o_lat, w_uv, gates, out_b, *, n_heads, c_lora):
    m = o_lat.shape[0]
    head = lambda h: (0, h)
    return pl.pallas_call(
        _absorb_o_body, grid=(n_heads,),
        in_specs=[pl.BlockSpec((m, c_lora), head), pl.BlockSpec((c_lora, LANES), head),
                  pl.BlockSpec((m, LANES), head), pl.BlockSpec((m, LANES), lambda h: (0, n_heads + h)),
                  pl.BlockSpec((m, LANES), head)],
        out_specs=pl.BlockSpec((m, LANES), head),
        out_shape=jax.ShapeDtypeStruct((m, n_heads * LANES), BF16), name="mla_absorb_o",
        compiler_params=_cparams(("parallel",)),
    )(o_lat, w_uv, gates, gates, out_b)


def _mla_abs_body(q_ref, kc_ref, o_ref, *, bb, t, n_heads, c_lora, tk, q0, l_valid):
    rows = t * n_heads
    tok = lax.broadcasted_iota(I32, (t, n_heads, 1), 0).reshape(rows, 1)
    lim = jnp.minimum((lax.shift_right_logical(q0 + tok, CHUNK_LOG2) + 1) * CHUNK, l_valid)
    lim_min = min((q0 // CHUNK + 1) * CHUNK, l_valid)
    lim_max = min(((q0 + t - 1) // CHUNK + 1) * CHUNK, l_valid)
    col = lax.broadcasted_iota(I32, (1, tk), 1)
    for j in range(bb):
        q = q_ref[j]
        m = jnp.full((rows, LANES), NEG, F32)
        l = jnp.zeros((rows, LANES), F32)
        acc = jnp.zeros((rows, c_lora), F32)
        for c in range(-(-lim_max // tk)):
            kc = kc_ref[j, c * tk:(c + 1) * tk, :]
            s = lax.dot_general(q, kc, _NT, preferred_element_type=F32)
            if (c + 1) * tk > lim_min:
                s = jnp.where(col + c * tk < lim, s, NEG)
            m, l, acc = _softmax_update(s, kc[:, 0:c_lora], m, l, acc)
        o_ref[j] = (acc * _lane_tile(1.0 / l, c_lora)).astype(o_ref.dtype)


def _mla_abs_attention(q_abs, kc, *, t, n_heads, c_lora, tk, q0, l_valid):
    b, rows, dk = q_abs.shape
    lp = kc.shape[1]
    bb = _pick(b, 4)
    return pl.pallas_call(
        functools.partial(_mla_abs_body, bb=bb, t=t, n_heads=n_heads, c_lora=c_lora, tk=tk, q0=q0, l_valid=l_valid),
        grid=(b // bb,),
        in_specs=[pl.BlockSpec((bb, rows, dk), lambda i: (i, 0, 0)), pl.BlockSpec((bb, lp, dk), lambda i: (i, 0, 0))],
        out_specs=pl.BlockSpec((bb, rows, c_lora), lambda i: (i, 0, 0)),
        out_shape=jax.ShapeDtypeStruct((b, rows, c_lora), BF16), name="mla_absorbed_attention",
        compiler_params=_cparams(("parallel",)),
    )(q_abs, kc)


def _dsa_body(qb_ref, qi_ref, wi_ref, ki_ref, kt_ref, v_ref, o_ref,
              keys_sc, m_sc, acc_sc, s_sc, *, tq, tk, q0, l_valid, n_idx, n_kv, rep, topk):
    q_first = q0 + pl.program_id(1) * tq
    lim, _, lim_max = _row_limits(q_first, tq, l_valid)
    n_all = (lim_max + tk - 1) // tk
    col = lax.broadcasted_iota(I32, (1, tk), 1)
    w = wi_ref[...]

    def score_chunk(c, carry):
        start = pl.multiple_of(c * tk, tk)
        kic = ki_ref[pl.ds(start, tk), :]
        s = jnp.zeros((tq, tk), F32)
        for h in range(n_idx):
            y = lax.dot_general(qi_ref[:, h * LANES:(h + 1) * LANES], kic, _NT,
                                preferred_element_type=F32)
            s = s + w[:, h:h + 1] * jnp.maximum(y, 0.0)
        s = jnp.where(s == 0.0, 0.0, s)
        bits = lax.bitcast_convert_type(s, I32)
        key = jnp.where(bits >= 0, bits, bits ^ jnp.int32(0x7FFFFFFF))
        keys_sc[c] = jnp.where(col + start < lim, key, jnp.int32(INT_MIN))
        return carry

    lax.fori_loop(0, n_all, score_chunk, 0)

    def count_rows(indicator):
        def count_chunk(c, acc):
            ind = indicator(c, keys_sc[c])
            part = ind[:, 0:LANES]
            for j in range(1, tk // LANES):
                part = part + ind[:, j * LANES:(j + 1) * LANES]
            return acc + part

        acc = lax.fori_loop(0, n_all, count_chunk, jnp.zeros((tq, LANES), F32))
        return jnp.sum(acc, axis=1, keepdims=True)

    def count_ge(cand):
        return count_rows(lambda c, kk: jnp.where(kk >= cand, 1.0, 0.0))

    def bit_pass(i, carry):
        t, n_at_t = carry
        cand = t + lax.shift_left(jnp.int32(1), jnp.int32(31) - i)
        n = count_ge(cand)
        keep = n >= topk
        return jnp.where(keep, cand, t), jnp.where(keep, n, n_at_t)

    t, n_at_t = lax.fori_loop(0, 32, bit_pass, (jnp.full((tq, 1), INT_MIN, I32), jnp.zeros((tq, 1), F32)))
    thr = jnp.maximum(t, jnp.int32(INT_MIN + 1))

    @pl.when(jnp.max(n_at_t) > topk)
    def _():
        slots = topk - count_ge(thr + 1)

        def tied_before(bound):
            return count_rows(lambda c, kk: jnp.where(kk == thr, jnp.where(col + c * tk < bound, 1.0, 0.0), 0.0))

        def index_pass(i, j_last):
            cand = j_last + lax.shift_left(jnp.int32(1), jnp.int32(INDEX_BITS - 1) - i)
            return jnp.where(tied_before(cand) < slots, cand, j_last)

        j_last = lax.fori_loop(0, INDEX_BITS, index_pass, jnp.zeros((tq, 1), I32))

        def demote(c, carry):
            kk = keys_sc[c]
            keys_sc[c] = jnp.where(kk == thr, jnp.where(col + c * tk > j_last, thr - 1, kk), kk)
            return carry

        lax.fori_loop(0, n_all, demote, 0)

    rows = rep * tq
    tok = jnp.bitwise_and(lax.broadcasted_iota(I32, (rows, LANES), 0), tq - 1)
    eye = jnp.where(tok == lax.broadcasted_iota(I32, (rows, LANES), 1), 1.0, 0.0).astype(BF16)

    gb = m_sc.shape[0]
    for g0 in range(0, n_kv, gb):
        m_sc[...] = jnp.full(m_sc.shape, NEG, F32)
        acc_sc[...] = jnp.zeros(acc_sc.shape, F32)

        def mask_bias(c):
            bias = jnp.where(keys_sc[c] >= thr, 0.0, NEG).astype(BF16)
            if tq < LANES:
                bias = jnp.concatenate([bias, jnp.zeros((LANES - tq, tk), BF16)], axis=0)
            return bias

        def scores(c, g, bias):
            qs = jnp.concatenate([qb_ref[:, (g * rep + r) * LANES:(g * rep + r + 1) * LANES]
                                  for r in range(rep)], axis=0)
            return jnp.dot(jnp.concatenate([qs, eye], axis=1), jnp.concatenate([kt_ref[g, c], bias], axis=0),
                           preferred_element_type=F32)

        bias0 = mask_bias(0)
        for j in range(gb):
            s_sc[j] = scores(0, g0 + j, bias0)

        def attend_chunk(c, carry, g0=g0, scores=scores, mask_bias=mask_bias):
            start = pl.multiple_of(c * tk, tk)
            nxt = jnp.minimum(c + 1, n_all - 1)
            bias = mask_bias(nxt)
            for j in range(gb):
                g = g0 + j
                _softmax_step(s_sc[j], v_ref[pl.ds(start, tk), g * LANES:(g + 1) * LANES],
                              m_sc.at[j], acc_sc.at[j])
                s_sc[j] = scores(nxt, g, bias)
            return carry

        lax.fori_loop(0, n_all, attend_chunk, 0)
        for j in range(gb):
            o = _softmax_result(acc_sc.at[j])
            for r in range(rep):
                h = (g0 + j) * rep + r
                o_ref[:, h * LANES:(h + 1) * LANES] = o[r * tq:(r + 1) * tq].astype(o_ref.dtype)


def _dsa_attention(qb, qi, wi, ki, k, v, *, n_kv, rep, n_idx, tq, tk, q0, l_valid, topk):
    b, t, _ = qb.shape
    lp = k.shape[1]
    kvw = n_kv * LANES
    assert tq <= LANES and tq & (tq - 1) == 0, "the token one-hot needs a power-of-two block of at most 128 rows"
    kt = k.reshape(b, lp // tk, tk, n_kv, LANES).transpose(0, 3, 1, 4, 2)
    kv_mode = dict(pipeline_mode=pl.Buffered(1)) if t // tq > 2 else {}
    stat = pltpu.VMEM((min(DSA_GROUPS_PER_PASS, n_kv), rep * tq, LANES), F32)
    return pl.pallas_call(
        functools.partial(_dsa_body, tq=tq, tk=tk, q0=q0, l_valid=l_valid, n_idx=n_idx, n_kv=n_kv, rep=rep,
                          topk=topk),
        grid=(b, t // tq),
        in_specs=[pl.BlockSpec((None, tq, n_kv * rep * LANES), lambda bi, qb_: (bi, qb_, 0)),
                  pl.BlockSpec((None, tq, n_idx * LANES), lambda bi, qb_: (bi, qb_, 0)),
                  pl.BlockSpec((None, tq, LANES), lambda bi, qb_: (bi, qb_, 0)),
                  pl.BlockSpec((None, lp, LANES), lambda bi, qb_: (bi, 0, 0), **kv_mode),
                  pl.BlockSpec((None, n_kv, lp // tk, LANES, tk), lambda bi, qb_: (bi, 0, 0, 0, 0), **kv_mode),
                  pl.BlockSpec((None, lp, kvw), lambda bi, qb_: (bi, 0, 0), **kv_mode)],
        out_specs=pl.BlockSpec((None, tq, n_kv * rep * LANES), lambda bi, qb_: (bi, qb_, 0)),
        out_shape=jax.ShapeDtypeStruct((b, t, n_kv * rep * LANES), BF16),
        scratch_shapes=[pltpu.VMEM((lp // tk, tq, tk), I32), stat,
                        pltpu.VMEM((stat.shape[0], rep * tq, 2 * LANES), F32),
                        pltpu.VMEM((stat.shape[0], rep * tq, tk), F32)],
        name="dsa_attention",
        compiler_params=pltpu.CompilerParams(dimension_semantics=("parallel", "arbitrary"),
                                             vmem_limit_bytes=VMEM_LIMIT_DSA),
    )(qb, qi, wi, ki, kt, v)


def _pad_cols(w, width):
    return jnp.pad(w, ((0, 0), (0, width - w.shape[1])))


def _prep_weights(w_in, w_uq, w_uk, w_uv, w_out, w_up, w_down, dims):
    d, q_lora, c_lora, r_mla, n_heads, n_kv, n_idx = dims
    splits = (q_lora, c_lora, r_mla, n_heads * HEAD_DIM, n_kv * HEAD_DIM, n_kv * HEAD_DIM,
              n_idx * HEAD_DIM, HEAD_DIM, n_idx, d, d)
    offs = np.concatenate([[0], np.cumsum(splits)])
    sec = [w_in[:, int(offs[i]):int(offs[i + 1])] for i in range(len(splits))]
    small = jnp.concatenate([sec[1], _pad_cols(sec[2], LANES), sec[4], sec[5], sec[7],
                             _pad_cols(sec[8], LANES)], axis=1)
    small = _pad_cols(small, -(-small.shape[1] // 512) * 512)
    parts = [("q", sec[0]), ("small", small), ("qb", sec[3]), ("qi", sec[6]),
             ("gates", jnp.concatenate([sec[9], sec[10]], axis=1))]
    windows, off = {}, 0
    for i, (key, p) in enumerate(parts):
        windows[key] = (off, p.shape[1])
        parts[i] = (key, _pad_cols(p, -(-p.shape[1] // TN) * TN))
        off += parts[i][1].shape[1]
    wq = w_uq.reshape(q_lora, n_heads, HEAD_DIM + r_mla)
    wq = jnp.pad(wq, ((0, 0), (0, 0), (0, 2 * LANES - HEAD_DIM - r_mla)))
    return dict(
        w_in=jnp.concatenate([p for _, p in parts], axis=1).astype(BF16), win=windows,
        w_uq=wq.reshape(q_lora, n_heads * 2 * LANES).astype(BF16),
        w_uk=w_uk.reshape(c_lora, n_heads * HEAD_DIM).astype(BF16),
        w_uv=w_uv.reshape(c_lora, n_heads * HEAD_DIM).astype(BF16),
        w_out=w_out.astype(BF16), w_up=w_up.astype(BF16), w_down=w_down.astype(BF16))


def _layer(x, ada, past, q0, pw, g_norm1, g_q_lora, g_kv_lora, g_norm2, dims, tq_mla, tq_dsa, tk):
    d, q_lora, c_lora, r_mla, n_heads, n_kv, n_idx = dims
    b, t, _ = x.shape
    m = b * t
    xf = x.reshape(m, d)
    if ada.shape[0] == 1:
        mods = [ada[:, i * d:(i + 1) * d] for i in range(N_ADA)]
    else:
        mods = [jnp.repeat(ada[:, i * d:(i + 1) * d], t, axis=0) for i in range(N_ADA)]
    sh1, sc1, gt1, sh2, sc2, gt2 = mods
    per_row = ada.shape[0] != 1

    pos = q0 + jnp.arange(t, dtype=I32)
    tab_m = jnp.tile(_rope_table(pos, MLA_THETA, r_mla), (b, 1))
    tab_d = jnp.tile(_rope_table(pos, ROPE_THETA, DSA_ROT), (b, 1))

    h = _norm(xf, g_norm1, sc1, sh1, BF16)

    def tab_extra(tab, tm):
        return (tab, (tm, 3 * LANES), lambda i, j: (i, 0))

    tm = _pick(m, 512)
    qn = _proj("proj_qlat", h, pw["w_in"], _ep_rmsnorm, BF16, cols=pw["win"]["q"], tn=q_lora,
                     extras=[(g_q_lora.reshape(1, q_lora), (1, q_lora), lambda i, j: (0, 0))])
    z_small = _proj("proj_small", h, pw["w_in"], _ep_plain, F32, cols=pw["win"]["small"], tn=512)
    q_dsa = _proj("proj_qdsa", h, pw["w_in"],
                  functools.partial(_ep_rope, half=DSA_ROT // 2, period=1, roped=(0,),
                                    scale=HEAD_DIM ** -0.5 * LOG2E),
                  BF16, cols=pw["win"]["qb"], extras=[tab_extra(tab_d, tm)])
    q_idx = _proj("proj_qidx", h, pw["w_in"],
                  functools.partial(_ep_rope, half=IDX_ROT // 2, period=1, roped=(0,), scale=1.0),
                  BF16, cols=pw["win"]["qi"], extras=[tab_extra(tab_d, tm)])
    gates = _proj("proj_gates", h, pw["w_in"], _ep_sigmoid, BF16, cols=pw["win"]["gates"])

    (ckv, ckv_b, krope, krope_b, k_new, k_new_b, v_new, v_new_b, ki_new, ki_new_b, wi) = _finish(
        z_small, g_kv_lora, tab_m, tab_d, c_lora=c_lora, r_mla=r_mla, n_kv=n_kv, idx_scale=n_idx ** -0.5)

    kvw = n_kv * HEAD_DIM
    if past is None:
        l_valid = t
        lp = t
        ckv_all, krope_all, k_all, v_all, ki_all = ckv_b, krope_b, k_new_b, v_new_b, ki_new_b
    else:
        p_ckv, p_krope, p_k, p_v, p_ki = past
        l_valid = p_ckv.shape[1] + t
        lp = -(-l_valid // tk) * tk

        def cat(p, new, width):
            p = p.reshape(b, p.shape[1], -1).astype(BF16)
            if p.shape[2] < width:
                p = jnp.pad(p, ((0, 0), (0, 0), (0, width - p.shape[2])))
            full = jnp.concatenate([p, new.reshape(b, t, width)], axis=1)
            return jnp.pad(full, ((0, 0), (0, lp - l_valid), (0, 0))).reshape(b * lp, width)

        ckv_all = cat(p_ckv, ckv_b, c_lora)
        krope_all = cat(p_krope, krope_b, LANES)
        k_all = cat(p_k, k_new_b, kvw)
        v_all = cat(p_v, v_new_b, kvw)
        ki_all = cat(p_ki, ki_new_b, LANES)

    out_b = _dsa_attention(q_dsa.reshape(b, t, -1), q_idx.reshape(b, t, -1), wi.reshape(b, t, LANES),
                           ki_all.reshape(b, lp, LANES), k_all.reshape(b, lp, kvw), v_all.reshape(b, lp, kvw),
                           n_kv=n_kv, rep=n_heads // n_kv, n_idx=n_idx, tq=min(tq_dsa, t), tk=min(tk, lp),
                           q0=q0, l_valid=l_valid, topk=min(TOPK_MAX, l_valid // 4))

    q_full = _proj("mla_q_up", qn, pw["w_uq"],
                         functools.partial(_ep_rope, half=r_mla // 2, period=2, roped=(1,),
                                           scale=(HEAD_DIM + r_mla) ** -0.5 * LOG2E),
                         BF16, extras=[tab_extra(tab_m, tm)])
    if past is not None and t * n_heads <= 1024 and n_heads % 8 == 0:
        kc = jnp.concatenate([ckv_all.reshape(b, lp, c_lora), krope_all.reshape(b, lp, LANES)], axis=2)
        q_abs = _absorb_q(q_full, pw["w_uk"], n_heads=n_heads, c_lora=c_lora)
        o_lat = _mla_abs_attention(q_abs.reshape(b, t * n_heads, c_lora + LANES), kc, t=t, n_heads=n_heads,
                                   c_lora=c_lora, tk=min(tk, lp), q0=q0, l_valid=l_valid)
        merged = _absorb_o(o_lat.reshape(m, n_heads * c_lora), pw["w_uv"], gates, out_b.reshape(m, d),
                           n_heads=n_heads, c_lora=c_lora)
    else:
        tml = _pick(b * lp, 512)
        k_full = _proj("mla_k_up", ckv_all, pw["w_uk"], _ep_kfull, BF16, out_cols=2 * n_heads * HEAD_DIM,
                       extras=[(krope_all, (tml, LANES), lambda i, j: (i, 0))])
        v_mla = _proj("mla_v_up", ckv_all, pw["w_uv"], _ep_plain, BF16)
        merged = _mla_attention(q_full.reshape(b, t, -1), k_full.reshape(b, lp, -1), v_mla.reshape(b, lp, -1),
                                gates.reshape(b, t, 2 * d), out_b,
                                n_heads=n_heads, hb=min(MLA_HEADS_PER_STEP, n_heads), tq=min(tq_mla, t),
                                tk=min(tk, lp), q0=q0, l_valid=l_valid).reshape(m, d)

    def gate_extra(gate, tm_, tn_):
        if per_row:
            return (gate, (tm_, tn_), lambda i, j: (i, j))
        return (gate, (1, tn_), lambda i, j: (0, j))

    tn = _pick(d, 1024)
    x1 = _proj("out_proj", merged, pw["w_out"], _ep_residual, F32,
                     extras=[(xf, (tm, tn), lambda i, j: (i, j)), gate_extra(gt1, tm, tn)])
    h2 = _norm(x1, g_norm2, sc2, sh2, BF16)
    u = _proj("mlp_up", h2, pw["w_up"], _ep_relu2, BF16)
    tm2 = _pick(m, 1024)
    x2 = _proj("mlp_down", u, pw["w_down"], _ep_residual, F32, tm=tm2, tk=2048,
               extras=[(x1, (tm2, tn), lambda i, j: (i, j)), gate_extra(gt2, tm2, tn)])
    rows = (ckv.reshape(b, t, c_lora), krope.reshape(b, t, r_mla), k_new.reshape(b, t, n_kv, HEAD_DIM),
            v_new.reshape(b, t, n_kv, HEAD_DIM), ki_new.reshape(b, t, HEAD_DIM))
    return x2, rows


def kernel(x_prompt, x_sample, c_prompt, c_sample, cache_mla_ckv, cache_mla_krope, cache_dsa_k, cache_dsa_v, cache_idx_k, w_ada, b_ada, g_norm1, w_in, g_q_lora, w_uq, g_kv_lora, w_uk, w_uv, w_out, g_norm2, w_up, w_down, g_final):
    depth = w_in.shape[0]
    bp, tp, d = x_prompt.shape
    bs, ts, _ = x_sample.shape
    n_heads = d // HEAD_DIM
    dims = (d, w_uq.shape[1], cache_mla_ckv.shape[-1], cache_mla_krope.shape[-1], n_heads,
            cache_dsa_k.shape[3], n_heads)
    past_len = cache_mla_ckv.shape[2]

    c_all = jnp.concatenate([c_prompt, c_sample], axis=0)
    n_c = c_all.shape[0]
    c_all = jnp.pad(c_all, ((0, -(-n_c // 16) * 16 - n_c), (0, 0)))

    xp, xs = x_prompt, x_sample
    rows_p, rows_s = [], []
    for l in range(depth):
        ada = _ada(c_all, w_ada[l], b_ada[l])
        pw = _prep_weights(w_in[l], w_uq[l], w_uk[l], w_uv[l], w_out[l], w_up[l], w_down[l], dims)
        norms = (g_norm1[l], g_q_lora[l], g_kv_lora[l], g_norm2[l])
        xp2, rp = _layer(xp, ada[0:bp], None, 0, pw, *norms, dims, tq_mla=TQ_MLA, tq_dsa=TQ_DSA, tk=TK_PROMPT)
        past = (cache_mla_ckv[l], cache_mla_krope[l], cache_dsa_k[l], cache_dsa_v[l], cache_idx_k[l])
        xs2, rs = _layer(xs, ada[bp:bp + bs], past, past_len, pw, *norms, dims, tq_mla=TQ_MLA, tq_dsa=TQ_DSA,
                         tk=TK_SAMPLE)
        xp, xs = xp2.reshape(bp, tp, d), xs2.reshape(bs, ts, d)
        rows_p.append(rp)
        rows_s.append(rs)

    y_prompt = _norm(xp.reshape(bp * tp, d), g_final, None, None, F32).reshape(bp, tp, d)
    y_sample = _norm(xs.reshape(bs * ts, d), g_final, None, None, F32).reshape(bs, ts, d)
    stack = lambda rows, i: jnp.stack([r[i] for r in rows])
    return (y_prompt, y_sample,
            stack(rows_p, 0), stack(rows_p, 1), stack(rows_p, 2), stack(rows_p, 3), stack(rows_p, 4),
            stack(rows_s, 0), stack(rows_s, 1), stack(rows_s, 2), stack(rows_s, 3), stack(rows_s, 4))
```

```python
import functools

import numpy as np
import jax
import jax.numpy as jnp
from jax import lax
from jax.experimental import pallas as pl
from jax.experimental.pallas import tpu as pltpu

F32 = jnp.float32
BF16 = jnp.bfloat16
I32 = jnp.int32

LANES = 128
CHUNK = 64
CHUNK_LOG2 = 6
EPS = 1e-6
NEG = -2.0 ** 100
INT_MIN = -2 ** 31
INDEX_BITS = 24
MLA_THETA = 10000.0
ROPE_THETA = 500000.0
HEAD_DIM = 128
DSA_ROT = HEAD_DIM // 4
IDX_ROT = HEAD_DIM // 4
TOPK_MAX = 256
N_ADA = 6
VMEM_LIMIT = 56 * 1024 * 1024
VMEM_LIMIT_DSA = 60 * 1024 * 1024
TQ_MLA, TQ_DSA = 512, 128
MLA_HEADS_PER_STEP = 4
DSA_GROUPS_PER_PASS = 4
TK_PROMPT, TK_SAMPLE = 512, 384


def _cparams(sem):
    return pltpu.CompilerParams(dimension_semantics=sem, vmem_limit_bytes=VMEM_LIMIT)


def _pick(n, pref):
    if n <= pref:
        return n
    t = pref
    while n % t:
        t //= 2
    return t


def _rope_table(pos, theta, rot_dim):
    half = rot_dim // 2
    inv = theta ** (-jnp.arange(half, dtype=F32) / half)
    ang = pos.astype(F32)[:, None] * inv[None, :]
    cos, sin = jnp.cos(ang), jnp.sin(ang)
    n = pos.shape[0]
    ones = jnp.ones((n, LANES - rot_dim), F32)
    zeros = jnp.zeros((n, LANES - rot_dim), F32)
    zh = jnp.zeros((n, half), F32)
    a = jnp.concatenate([cos, cos, ones], axis=1)
    b = jnp.concatenate([-sin, zh, zeros], axis=1)
    c = jnp.concatenate([zh, sin, zeros], axis=1)
    return jnp.concatenate([a, b, c], axis=1)


def _rope_slab(x, tab, half):
    a = tab[:, 0:LANES]
    b = tab[:, LANES:2 * LANES]
    c = tab[:, 2 * LANES:3 * LANES]
    return x * a + pltpu.roll(x, LANES - half, 1) * b + pltpu.roll(x, half, 1) * c


def _mm_body(*refs, nk, n_extra, n_out, epilogue, cache_w):
    a_ref, w_ref = refs[0], refs[1]
    extra = refs[2:2 + n_extra]
    outs = refs[2 + n_extra:2 + n_extra + n_out]
    a = a_ref[...].astype(BF16)
    if cache_w:
        w_sc = refs[-1]

        @pl.when(pl.program_id(1) == 0)
        def _():
            w_sc[...] = w_ref[...].astype(BF16)

        w = w_sc[...]
    else:
        w = w_ref[...].astype(BF16)
    part = jnp.dot(a, w, preferred_element_type=F32)
    if nk == 1:
        epilogue(part, extra, outs)
        return
    acc_ref = refs[2 + n_extra + n_out]
    k = pl.program_id(2)

    @pl.when(k == 0)
    def _():
        acc_ref[...] = part

    @pl.when(k > 0)
    def _():
        acc_ref[...] += part

    @pl.when(k == nk - 1)
    def _():
        epilogue(acc_ref[...], extra, outs)


def _matmul(a, w, *, n, col0, tm, tn, tk, extras, outs, epilogue, name, order="nm"):
    m, kdim = a.shape
    assert col0 % tn == 0
    jb = col0 // tn
    gi, gj, gk = m // tm, n // tn, kdim // tk
    if order == "nm":
        grid = (gj, gi, gk)
        wrap = lambda f: (lambda pj, pi, pk: f(pi, pj, pk))
    else:
        grid = (gi, gj, gk)
        wrap = lambda f: (lambda pi, pj, pk: f(pi, pj, pk))
    in_specs = [
        pl.BlockSpec((tm, tk), wrap(lambda i, j, k: (i, k))),
        pl.BlockSpec((tk, tn), wrap(lambda i, j, k: (k, j + jb))),
    ]
    args = [a, w]
    for arr, blk, fn in extras:
        in_specs.append(pl.BlockSpec(blk, wrap(lambda i, j, k, fn=fn: fn(i, j))))
        args.append(arr)
    out_specs = [pl.BlockSpec(blk, wrap(lambda i, j, k, fn=fn: fn(i, j))) for _, blk, fn in outs]
    out_shape = [s for s, _, _ in outs]
    scratch = [pltpu.VMEM((tm, tn), F32)] if gk > 1 else []
    cache_w = w.dtype != BF16 and order == "nm" and gk == 1 and gi > 1
    if cache_w:
        scratch.append(pltpu.VMEM((tk, tn), BF16))
    body = functools.partial(_mm_body, nk=gk, n_extra=len(extras), n_out=len(outs), epilogue=epilogue,
                             cache_w=cache_w)
    res = pl.pallas_call(
        body, grid=grid, in_specs=in_specs, out_specs=out_specs, out_shape=out_shape,
        scratch_shapes=scratch, name=name,
        compiler_params=_cparams(("parallel", "arbitrary" if cache_w else "parallel", "arbitrary")),
    )(*args)
    return res


def _ep_plain(acc, extra, outs):
    outs[0][...] = acc.astype(outs[0].dtype)


def _ep_rmsnorm(acc, extra, outs):
    g = extra[0][...]
    y = acc * lax.rsqrt(jnp.mean(acc * acc, axis=-1, keepdims=True) + EPS) * g
    outs[0][...] = y.astype(outs[0].dtype)


def _ep_sigmoid(acc, extra, outs):
    outs[0][...] = jax.nn.sigmoid(acc).astype(outs[0].dtype)


def _ep_relu2(acc, extra, outs):
    r = jnp.maximum(acc, 0.0)
    outs[0][...] = (r * r).astype(outs[0].dtype)


def _ep_residual(acc, extra, outs):
    x = extra[0][...]
    gate = extra[1][...]
    outs[0][...] = x + gate * acc


def _ep_rope(acc, extra, outs, *, half, period, roped, scale):
    tab = extra[0][...]
    o = outs[0]
    for s in range(acc.shape[1] // LANES):
        x = acc[:, s * LANES:(s + 1) * LANES]
        if (s % period) in roped:
            x = _rope_slab(x, tab, half)
        if scale != 1.0:
            x = x * scale
        o[:, s * LANES:(s + 1) * LANES] = x.astype(o.dtype)


def _ep_kfull(acc, extra, outs):
    kr = extra[0][...]
    o = outs[0]
    for h in range(acc.shape[1] // LANES):
        o[:, (2 * h) * LANES:(2 * h + 1) * LANES] = acc[:, h * LANES:(h + 1) * LANES].astype(o.dtype)
        o[:, (2 * h + 1) * LANES:(2 * h + 2) * LANES] = kr


TM, TN, TK = 512, 1024, 4096


def _proj(name, a, w, ep, out_dtype, *, cols=None, tm=TM, tn=TN, tk=TK, extras=(), out_cols=None, order="nm"):
    m, k = a.shape
    col0, n = (0, w.shape[1]) if cols is None else cols
    tm, tn, tk = _pick(m, tm), _pick(n, tn), _pick(k, tk)
    oc = n if out_cols is None else out_cols
    outs = [(jax.ShapeDtypeStruct((m, oc), out_dtype), (tm, tn * oc // n), lambda i, j: (i, j))]
    return _matmul(a, w, n=n, col0=col0, tm=tm, tn=tn, tk=tk, extras=list(extras), outs=outs, epilogue=ep,
                   name=name, order=order)[0]


def _ada_body(c_ref, w_ref, b_ref, o_ref):
    c = c_ref[...]
    a = (c * jax.nn.sigmoid(c)).astype(BF16)
    o_ref[...] = jnp.dot(a, w_ref[...].astype(BF16), preferred_element_type=F32) + b_ref[...]


def _ada(c_all, w_ada, b_ada):
    r, d = c_all.shape
    n = w_ada.shape[1]
    tn = _pick(n, 512)
    return pl.pallas_call(
        _ada_body, grid=(n // tn,),
        in_specs=[pl.BlockSpec((r, d), lambda j: (0, 0)),
                  pl.BlockSpec((d, tn), lambda j: (0, j)),
                  pl.BlockSpec((1, tn), lambda j: (0, j))],
        out_specs=pl.BlockSpec((r, tn), lambda j: (0, j)),
        out_shape=jax.ShapeDtypeStruct((r, n), F32), name="adaln",
        compiler_params=_cparams(("parallel",)),
    )(c_all, w_ada, b_ada.reshape(1, n))


def _norm_body(*refs, modulate):
    x = refs[0][...]
    g = refs[1][...]
    y = x * lax.rsqrt(jnp.mean(x * x, axis=-1, keepdims=True) + EPS) * g
    if modulate:
        y = y * (1.0 + refs[2][...]) + refs[3][...]
        o = refs[4]
    else:
        o = refs[2]
    o[...] = y.astype(o.dtype)


def _norm(x, g, sc, sh, out_dtype):
    m, d = x.shape
    tm = _pick(m, 256)
    in_specs = [pl.BlockSpec((tm, d), lambda i: (i, 0)), pl.BlockSpec((1, d), lambda i: (0, 0))]
    args = [x, g.reshape(1, d)]
    if sc is not None:
        per_row = sc.shape[0] == m
        blk = (tm, d) if per_row else (1, d)
        fn = (lambda i: (i, 0)) if per_row else (lambda i: (0, 0))
        in_specs += [pl.BlockSpec(blk, fn), pl.BlockSpec(blk, fn)]
        args += [sc, sh]
    return pl.pallas_call(
        functools.partial(_norm_body, modulate=sc is not None), grid=(m // tm,),
        in_specs=in_specs, out_specs=pl.BlockSpec((tm, d), lambda i: (i, 0)),
        out_shape=jax.ShapeDtypeStruct((m, d), out_dtype), name="rmsnorm",
        compiler_params=_cparams(("parallel",)),
    )(*args)


def _finish_body(z_ref, g_ref, tm_ref, td_ref,
                 ckv_o, ckvb_o, kr_o, krb_o, k_o, kb_o, v_o, vb_o, ki_o, kib_o, wi_o,
                 *, c_lora, r_mla, n_kv, idx_scale):
    tab_m = tm_ref[...]
    tab_d = td_ref[...]
    off = 0
    ckv = z_ref[:, off:off + c_lora]
    ckv = ckv * lax.rsqrt(jnp.mean(ckv * ckv, axis=-1, keepdims=True) + EPS) * g_ref[...]
    ckv_o[...] = ckv
    ckvb_o[...] = ckv.astype(BF16)
    off += c_lora
    kr = _rope_slab(z_ref[:, off:off + LANES], tab_m, r_mla // 2)
    kr_o[...] = kr[:, 0:r_mla]
    krb_o[...] = kr.astype(BF16)
    off += LANES
    for h in range(n_kv):
        kh = _rope_slab(z_ref[:, off + h * LANES:off + (h + 1) * LANES], tab_d, DSA_ROT // 2)
        k_o[:, h * LANES:(h + 1) * LANES] = kh
        kb_o[:, h * LANES:(h + 1) * LANES] = kh.astype(BF16)
    off += n_kv * LANES
    v = z_ref[:, off:off + n_kv * LANES]
    v_o[...] = v
    vb_o[...] = v.astype(BF16)
    off += n_kv * LANES
    ki = _rope_slab(z_ref[:, off:off + LANES], tab_d, IDX_ROT // 2)
    ki_o[...] = ki
    kib_o[...] = ki.astype(BF16)
    off += LANES
    wi_o[...] = z_ref[:, off:off + LANES] * idx_scale


def _finish(z, g_kv, tab_m, tab_d, *, c_lora, r_mla, n_kv, idx_scale):
    m, zc = z.shape
    tm = _pick(m, 256)
    kvw = n_kv * LANES
    row = lambda i: (i, 0)
    shapes = [(c_lora, F32), (c_lora, BF16), (r_mla, F32), (LANES, BF16), (kvw, F32), (kvw, BF16),
              (kvw, F32), (kvw, BF16), (LANES, F32), (LANES, BF16), (LANES, F32)]
    return pl.pallas_call(
        functools.partial(_finish_body, c_lora=c_lora, r_mla=r_mla, n_kv=n_kv, idx_scale=idx_scale),
        grid=(m // tm,),
        in_specs=[pl.BlockSpec((tm, zc), row), pl.BlockSpec((1, c_lora), lambda i: (0, 0)),
                  pl.BlockSpec((tm, 3 * LANES), row), pl.BlockSpec((tm, 3 * LANES), row)],
        out_specs=[pl.BlockSpec((tm, w), row) for w, _ in shapes],
        out_shape=[jax.ShapeDtypeStruct((m, w), dt) for w, dt in shapes], name="finish_small",
        compiler_params=_cparams(("parallel",)),
    )(z, g_kv.reshape(1, c_lora), tab_m, tab_d)


def _row_limits(q_first, tq, l_valid):
    rows = q_first + lax.broadcasted_iota(I32, (tq, 1), 0)
    chunk_end = lambda p: (lax.shift_right_logical(p, CHUNK_LOG2) + 1) * CHUNK
    lim = jnp.minimum(chunk_end(rows), l_valid)
    lim_min = jnp.minimum(chunk_end(q_first), l_valid)
    lim_max = jnp.minimum(chunk_end(q_first + tq - 1), l_valid)
    return lim, lim_min, lim_max


LOG2E = 1.4426950408889634


def _lane_tile(x, width):
    return x if width == LANES else jnp.concatenate([x] * (width // LANES), axis=1)


def _softmax_update(s, v, m_prev, l_prev, acc_prev):
    m_new = jnp.maximum(m_prev, jnp.max(s, axis=1, keepdims=True))
    p = jnp.exp2(s - _lane_tile(m_new, s.shape[1]))
    alpha = jnp.exp2(m_prev - m_new)
    l_new = alpha * l_prev + jnp.sum(p, axis=1, keepdims=True)
    acc_new = _lane_tile(alpha, acc_prev.shape[1]) * acc_prev + jnp.dot(p.astype(BF16), v,
                                                                        preferred_element_type=F32)
    return m_new, l_new, acc_new


def _softmax_step(s, v, m_sc, acc_sc):
    sb = s.astype(BF16)
    m_prev = m_sc[...]
    m_new = jnp.maximum(m_prev, jnp.max(sb, axis=1, keepdims=True).astype(F32))
    p = jnp.exp2(sb - _lane_tile(m_new.astype(BF16), s.shape[1]))
    alpha = jnp.exp2(m_prev - m_new)
    v_ones = jnp.concatenate([v, jnp.ones(v.shape, BF16)], axis=1)
    acc_sc[...] = _lane_tile(alpha, 2 * LANES) * acc_sc[...] + jnp.dot(p, v_ones, preferred_element_type=F32)
    m_sc[...] = m_new


def _softmax_result(acc_sc):
    acc = acc_sc[...]
    return acc[:, 0:LANES] * (1.0 / acc[:, LANES:2 * LANES])


_NT = (((1,), (1,)), ((), ()))


def _gate_mix(out_a, ga, gb, out_b):
    return (ga.astype(F32) * out_a + gb.astype(F32) * out_b.astype(F32)).astype(BF16)


def _mla_body(q_ref, k_ref, v_ref, ga_ref, gb_ref, ob_ref, o_ref, m_sc, acc_sc, s_sc,
              *, hb, tq, tk, q0, l_valid):
    q_first = q0 + pl.program_id(2) * tq
    lim, lim_min, lim_max = _row_limits(q_first, tq, l_valid)
    n_full = lim_min // tk
    n_all = (lim_max + tk - 1) // tk
    col = lax.broadcasted_iota(I32, (1, tk), 1)
    m_sc[...] = jnp.full(m_sc.shape, NEG, F32)
    acc_sc[...] = jnp.zeros(acc_sc.shape, F32)

    def scores(c, j):
        start = pl.multiple_of(c * tk, tk)
        return lax.dot_general(q_ref[:, j * 2 * LANES:(j + 1) * 2 * LANES],
                               k_ref[pl.ds(start, tk), j * 2 * LANES:(j + 1) * 2 * LANES], _NT,
                               preferred_element_type=F32)

    for j in range(hb):
        s_sc[j] = scores(0, j)

    def step(c, masked):
        start = pl.multiple_of(c * tk, tk)
        nxt = jnp.minimum(c + 1, n_all - 1)
        for j in range(hb):
            s = s_sc[j]
            if masked:
                s = jnp.where(col + start < lim, s, NEG)
            _softmax_step(s, v_ref[pl.ds(start, tk), j * LANES:(j + 1) * LANES], m_sc.at[j], acc_sc.at[j])
            s_sc[j] = scores(nxt, j)

    def full_step(c, carry):
        step(c, False)
        return carry

    def masked_step(c, carry):
        step(c, True)
        return carry

    lax.fori_loop(0, n_full, full_step, 0)
    lax.fori_loop(n_full, n_all, masked_step, 0)
    for j in range(hb):
        cols = slice(j * LANES, (j + 1) * LANES)
        o_ref[:, cols] = _gate_mix(_softmax_result(acc_sc.at[j]), ga_ref[:, cols], gb_ref[:, cols], ob_ref[:, cols])


def _mla_attention(q_full, k_full, v, gates, out_b, *, n_heads, hb, tq, tk, q0, l_valid):
    b, t, _ = q_full.shape
    lp = k_full.shape[1]
    kv_mode = dict(pipeline_mode=pl.Buffered(1)) if t // tq > 2 else {}
    stat = pltpu.VMEM((hb, tq, LANES), F32)
    return pl.pallas_call(
        functools.partial(_mla_body, hb=hb, tq=tq, tk=tk, q0=q0, l_valid=l_valid),
        grid=(b, n_heads // hb, t // tq),
        in_specs=[pl.BlockSpec((None, tq, hb * 2 * LANES), lambda bi, h, qi: (bi, qi, h)),
                  pl.BlockSpec((None, lp, hb * 2 * LANES), lambda bi, h, qi: (bi, 0, h), **kv_mode),
                  pl.BlockSpec((None, lp, hb * LANES), lambda bi, h, qi: (bi, 0, h), **kv_mode),
                  pl.BlockSpec((None, tq, hb * LANES), lambda bi, h, qi: (bi, qi, h)),
                  pl.BlockSpec((None, tq, hb * LANES), lambda bi, h, qi: (bi, qi, n_heads // hb + h)),
                  pl.BlockSpec((None, tq, hb * LANES), lambda bi, h, qi: (bi, qi, h))],
        out_specs=pl.BlockSpec((None, tq, hb * LANES), lambda bi, h, qi: (bi, qi, h)),
        out_shape=jax.ShapeDtypeStruct((b, t, n_heads * LANES), BF16),
        scratch_shapes=[stat, pltpu.VMEM((hb, tq, 2 * LANES), F32), pltpu.VMEM((hb, tq, tk), F32)],
        name="mla_attention",
        compiler_params=_cparams(("parallel", "parallel", "arbitrary")),
    )(q_full, k_full, v, gates, gates, out_b)


def _absorb_q_body(q_ref, w_ref, o_ref, *, c_lora):
    lat = lax.dot_general(q_ref[:, 0:LANES], w_ref[...], _NT, preferred_element_type=F32)
    o_ref[:, 0:c_lora] = lat.astype(o_ref.dtype)
    o_ref[:, c_lora:c_lora + LANES] = q_ref[:, LANES:2 * LANES]


def _absorb_q(q_full, w_uk, *, n_heads, c_lora):
    m = q_full.shape[0]
    return pl.pallas_call(
        functools.partial(_absorb_q_body, c_lora=c_lora), grid=(n_heads,),
        in_specs=[pl.BlockSpec((m, 2 * LANES), lambda h: (0, h)), pl.BlockSpec((c_lora, LANES), lambda h: (0, h))],
        out_specs=pl.BlockSpec((m, c_lora + LANES), lambda h: (0, h)),
        out_shape=jax.ShapeDtypeStruct((m, n_heads * (c_lora + LANES)), BF16), name="mla_absorb_q",
        compiler_params=_cparams(("parallel",)),
    )(q_full, w_uk)


def _absorb_o_body(o_lat_ref, w_ref, ga_ref, gb_ref, ob_ref, o_ref):
    out_a = jnp.dot(o_lat_ref[...], w_ref[...], preferred_element_type=F32)
    o_ref[...] = _gate_mix(out_a, ga_ref[...], gb_ref[...], ob_ref[...])


def _absorb_o(o_lat, w_uv, gates, out_b, *, n_heads, c_lora):
    m = o_lat.shape[0]
    head = lambda h: (0, h)
    return pl.pallas_call(
        _absorb_o_body, grid=(n_heads,),
        in_specs=[pl.BlockSpec((m, c_lora), head), pl.BlockSpec((c_lora, LANES), head),
                  pl.BlockSpec((m, LANES), head), pl.BlockSpec((m, LANES), lambda h: (0, n_heads + h)),
                  pl.BlockSpec((m, LANES), head)],
        out_specs=pl.BlockSpec((m, LANES), head),
        out_shape=jax.ShapeDtypeStruct((m, n_heads * LANES), BF16), name="mla_absorb_o",
        compiler_params=_cparams(("parallel",)),
    )(o_lat, w_uv, gates, gates, out_b)


def _mla_abs_body(q_ref, kc_ref, o_ref, *, bb, t, n_heads, c_lora, tk, q0, l_valid):
    rows = t * n_heads
    tok = lax.broadcasted_iota(I32, (t, n_heads, 1), 0).reshape(rows, 1)
    lim = jnp.minimum((lax.shift_right_logical(q0 + tok, CHUNK_LOG2) + 1) * CHUNK, l_valid)
    lim_min = min((q0 // CHUNK + 1) * CHUNK, l_valid)
    lim_max = min(((q0 + t - 1) // CHUNK + 1) * CHUNK, l_valid)
    col = lax.broadcasted_iota(I32, (1, tk), 1)
    for j in range(bb):
        q = q_ref[j]
        m = jnp.full((rows, LANES), NEG, F32)
        l = jnp.zeros((rows, LANES), F32)
        acc = jnp.zeros((rows, c_lora), F32)
        for c in range(-(-lim_max // tk)):
            kc = kc_ref[j, c * tk:(c + 1) * tk, :]
            s = lax.dot_general(q, kc, _NT, preferred_element_type=F32)
            if (c + 1) * tk > lim_min:
                s = jnp.where(col + c * tk < lim, s, NEG)
            m, l, acc = _softmax_update(s, kc[:, 0:c_lora], m, l, acc)
        o_ref[j] = (acc * _lane_tile(1.0 / l, c_lora)).astype(o_ref.dtype)


def _mla_abs_attention(q_abs, kc, *, t, n_heads, c_lora, tk, q0, l_valid):
    b, rows, dk = q_abs.shape
    lp = kc.shape[1]
    bb = _pick(b, 4)
    return pl.pallas_call(
        functools.partial(_mla_abs_body, bb=bb, t=t, n_heads=n_heads, c_lora=c_lora, tk=tk, q0=q0, l_valid=l_valid),
        grid=(b // bb,),
        in_specs=[pl.BlockSpec((bb, rows, dk), lambda i: (i, 0, 0)), pl.BlockSpec((bb, lp, dk), lambda i: (i, 0, 0))],
        out_specs=pl.BlockSpec((bb, rows, c_lora), lambda i: (i, 0, 0)),
        out_shape=jax.ShapeDtypeStruct((b, rows, c_lora), BF16), name="mla_absorbed_attention",
        compiler_params=_cparams(("parallel",)),
    )(q_abs, kc)


def _dsa_body(qb_ref, qi_ref, wi_ref, ki_ref, kt_ref, v_ref, o_ref,
              keys_sc, m_sc, acc_sc, s_sc, *, tq, tk, q0, l_valid, n_idx, n_kv, rep, topk):
    q_first = q0 + pl.program_id(1) * tq
    lim, _, lim_max = _row_limits(q_first, tq, l_valid)
    n_all = (lim_max + tk - 1) // tk
    col = lax.broadcasted_iota(I32, (1, tk), 1)
    w = wi_ref[...]

    def score_chunk(c, carry):
        start = pl.multiple_of(c * tk, tk)
        kic = ki_ref[pl.ds(start, tk), :]
        s = jnp.zeros((tq, tk), F32)
        for h in range(n_idx):
            y = lax.dot_general(qi_ref[:, h * LANES:(h + 1) * LANES], kic, _NT,
                                preferred_element_type=F32)
            s = s + w[:, h:h + 1] * jnp.maximum(y, 0.0)
        s = jnp.where(s == 0.0, 0.0, s)
        bits = lax.bitcast_convert_type(s, I32)
        key = jnp.where(bits >= 0, bits, bits ^ jnp.int32(0x7FFFFFFF))
        keys_sc[c] = jnp.where(col + start < lim, key, jnp.int32(INT_MIN))
        return carry

    lax.fori_loop(0, n_all, score_chunk, 0)

    def count_rows(indicator):
        def count_chunk(c, acc):
            ind = indicator(c, keys_sc[c])
            part = ind[:, 0:LANES]
            for j in range(1, tk // LANES):
                part = part + ind[:, j * LANES:(j + 1) * LANES]
            return acc + part

        acc = lax.fori_loop(0, n_all, count_chunk, jnp.zeros((tq, LANES), F32))
        return jnp.sum(acc, axis=1, keepdims=True)

    def count_ge(cand):
        return count_rows(lambda c, kk: jnp.where(kk >= cand, 1.0, 0.0))

    def bit_pass(i, carry):
        t, n_at_t = carry
        cand = t + lax.shift_left(jnp.int32(1), jnp.int32(31) - i)
        n = count_ge(cand)
        keep = n >= topk
        return jnp.where(keep, cand, t), jnp.where(keep, n, n_at_t)

    t, n_at_t = lax.fori_loop(0, 32, bit_pass, (jnp.full((tq, 1), INT_MIN, I32), jnp.zeros((tq, 1), F32)))
    thr = jnp.maximum(t, jnp.int32(INT_MIN + 1))

    @pl.when(jnp.max(n_at_t) > topk)
    def _():
        slots = topk - count_ge(thr + 1)

        def tied_before(bound):
            return count_rows(lambda c, kk: jnp.where(kk == thr, jnp.where(col + c * tk < bound, 1.0, 0.0), 0.0))

        def index_pass(i, j_last):
            cand = j_last + lax.shift_left(jnp.int32(1), jnp.int32(INDEX_BITS - 1) - i)
            return jnp.where(tied_before(cand) < slots, cand, j_last)

        j_last = lax.fori_loop(0, INDEX_BITS, index_pass, jnp.zeros((tq, 1), I32))

        def demote(c, carry):
            kk = keys_sc[c]
            keys_sc[c] = jnp.where(kk == thr, jnp.where(col + c * tk > j_last, thr - 1, kk), kk)
            return carry

        lax.fori_loop(0, n_all, demote, 0)

    rows = rep * tq
    tok = jnp.bitwise_and(lax.broadcasted_iota(I32, (rows, LANES), 0), tq - 1)
    eye = jnp.where(tok == lax.broadcasted_iota(I32, (rows, LANES), 1), 1.0, 0.0).astype(BF16)

    gb = m_sc.shape[0]
    for g0 in range(0, n_kv, gb):
        m_sc[...] = jnp.full(m_sc.shape, NEG, F32)
        acc_sc[...] = jnp.zeros(acc_sc.shape, F32)

        def mask_bias(c):
            bias = jnp.where(keys_sc[c] >= thr, 0.0, NEG).astype(BF16)
            if tq < LANES:
                bias = jnp.concatenate([bias, jnp.zeros((LANES - tq, tk), BF16)], axis=0)
            return bias

        def scores(c, g, bias):
            qs = jnp.concatenate([qb_ref[:, (g * rep + r) * LANES:(g * rep + r + 1) * LANES]
                                  for r in range(rep)], axis=0)
            return jnp.dot(jnp.concatenate([qs, eye], axis=1), jnp.concatenate([kt_ref[g, c], bias], axis=0),
                           preferred_element_type=F32)

        bias0 = mask_bias(0)
        for j in range(gb):
            s_sc[j] = scores(0, g0 + j, bias0)

        def attend_chunk(c, carry, g0=g0, scores=scores, mask_bias=mask_bias):
            start = pl.multiple_of(c * tk, tk)
            nxt = jnp.minimum(c + 1, n_all - 1)
            bias = mask_bias(nxt)
            for j in range(gb):
                g = g0 + j
                _softmax_step(s_sc[j], v_ref[pl.ds(start, tk), g * LANES:(g + 1) * LANES],
                              m_sc.at[j], acc_sc.at[j])
                s_sc[j] = scores(nxt, g, bias)
            return carry

        lax.fori_loop(0, n_all, attend_chunk, 0)
        for j in range(gb):
            o = _softmax_result(acc_sc.at[j])
            for r in range(rep):
                h = (g0 + j) * rep + r
                o_ref[:, h * LANES:(h + 1) * LANES] = o[r * tq:(r + 1) * tq].astype(o_ref.dtype)


def _dsa_attention(qb, qi, wi, ki, k, v, *, n_kv, rep, n_idx, tq, tk, q0, l_valid, topk):
    b, t, _ = qb.shape
    lp = k.shape[1]
    kvw = n_kv * LANES
    assert tq <= LANES and tq & (tq - 1) == 0, "the token one-hot needs a power-of-two block of at most 128 rows"
    kt = k.reshape(b, lp // tk, tk, n_kv, LANES).transpose(0, 3, 1, 4, 2)
    kv_mode = dict(pipeline_mode=pl.Buffered(1)) if t // tq > 2 else {}
    stat = pltpu.VMEM((min(DSA_GROUPS_PER_PASS, n_kv), rep * tq, LANES), F32)
    return pl.pallas_call(
        functools.partial(_dsa_body, tq=tq, tk=tk, q0=q0, l_valid=l_valid, n_idx=n_idx, n_kv=n_kv, rep=rep,
                          topk=topk),
        grid=(b, t // tq),
        in_specs=[pl.BlockSpec((None, tq, n_kv * rep * LANES), lambda bi, qb_: (bi, qb_, 0)),
                  pl.BlockSpec((None, tq, n_idx * LANES), lambda bi, qb_: (bi, qb_, 0)),
                  pl.BlockSpec((None, tq, LANES), lambda bi, qb_: (bi, qb_, 0)),
                  pl.BlockSpec((None, lp, LANES), lambda bi, qb_: (bi, 0, 0), **kv_mode),
                  pl.BlockSpec((None, n_kv, lp // tk, LANES, tk), lambda bi, qb_: (bi, 0, 0, 0, 0), **kv_mode),
                  pl.BlockSpec((None, lp, kvw), lambda bi, qb_: (bi, 0, 0), **kv_mode)],
        out_specs=pl.BlockSpec((None, tq, n_kv * rep * LANES), lambda bi, qb_: (bi, qb_, 0)),
        out_shape=jax.ShapeDtypeStruct((b, t, n_kv * rep * LANES), BF16),
        scratch_shapes=[pltpu.VMEM((lp // tk, tq, tk), I32), stat,
                        pltpu.VMEM((stat.shape[0], rep * tq, 2 * LANES), F32),
                        pltpu.VMEM((stat.shape[0], rep * tq, tk), F32)],
        name="dsa_attention",
        compiler_params=pltpu.CompilerParams(dimension_semantics=("parallel", "arbitrary"),
                                             vmem_limit_bytes=VMEM_LIMIT_DSA),
    )(qb, qi, wi, ki, kt, v)


def _pad_cols(w, width):
    return jnp.pad(w, ((0, 0), (0, width - w.shape[1])))


def _prep_weights(w_in, w_uq, w_uk, w_uv, w_out, w_up, w_down, dims):
    d, q_lora, c_lora, r_mla, n_heads, n_kv, n_idx = dims
    splits = (q_lora, c_lora, r_mla, n_heads * HEAD_DIM, n_kv * HEAD_DIM, n_kv * HEAD_DIM,
              n_idx * HEAD_DIM, HEAD_DIM, n_idx, d, d)
    offs = np.concatenate([[0], np.cumsum(splits)])
    sec = [w_in[:, int(offs[i]):int(offs[i + 1])] for i in range(len(splits))]
    small = jnp.concatenate([sec[1], _pad_cols(sec[2], LANES), sec[4], sec[5], sec[7],
                             _pad_cols(sec[8], LANES)], axis=1)
    small = _pad_cols(small, -(-small.shape[1] // 512) * 512)
    parts = [("q", sec[0]), ("small", small), ("qb", sec[3]), ("qi", sec[6]),
             ("gates", jnp.concatenate([sec[9], sec[10]], axis=1))]
    windows, off = {}, 0
    for i, (key, p) in enumerate(parts):
        windows[key] = (off, p.shape[1])
        parts[i] = (key, _pad_cols(p, -(-p.shape[1] // TN) * TN))
        off += parts[i][1].shape[1]
    wq = w_uq.reshape(q_lora, n_heads, HEAD_DIM + r_mla)
    wq = jnp.pad(wq, ((0, 0), (0, 0), (0, 2 * LANES - HEAD_DIM - r_mla)))
    return dict(
        w_in=jnp.concatenate([p for _, p in parts], axis=1).astype(BF16), win=windows,
        w_uq=wq.reshape(q_lora, n_heads * 2 * LANES).astype(BF16),
        w_uk=w_uk.reshape(c_lora, n_heads * HEAD_DIM).astype(BF16),
        w_uv=w_uv.reshape(c_lora, n_heads * HEAD_DIM).astype(BF16),
        w_out=w_out.astype(BF16), w_up=w_up, w_down=w_down.astype(BF16))


def _layer(x, ada, past, q0, pw, g_norm1, g_q_lora, g_kv_lora, g_norm2, dims, tq_mla, tq_dsa, tk):
    d, q_lora, c_lora, r_mla, n_heads, n_kv, n_idx = dims
    b, t, _ = x.shape
    m = b * t
    xf = x.reshape(m, d)
    if ada.shape[0] == 1:
        mods = [ada[:, i * d:(i + 1) * d] for i in range(N_ADA)]
    else:
        mods = [jnp.repeat(ada[:, i * d:(i + 1) * d], t, axis=0) for i in range(N_ADA)]
    sh1, sc1, gt1, sh2, sc2, gt2 = mods
    per_row = ada.shape[0] != 1

    pos = q0 + jnp.arange(t, dtype=I32)
    tab_m = jnp.tile(_rope_table(pos, MLA_THETA, r_mla), (b, 1))
    tab_d = jnp.tile(_rope_table(pos, ROPE_THETA, DSA_ROT), (b, 1))

    h = _norm(xf, g_norm1, sc1, sh1, BF16)

    def tab_extra(tab, tm):
        return (tab, (tm, 3 * LANES), lambda i, j: (i, 0))

    tm = _pick(m, 512)
    qn = _proj("proj_qlat", h, pw["w_in"], _ep_rmsnorm, BF16, cols=pw["win"]["q"], tn=q_lora,
                     extras=[(g_q_lora.reshape(1, q_lora), (1, q_lora), lambda i, j: (0, 0))])
    z_small = _proj("proj_small", h, pw["w_in"], _ep_plain, F32, cols=pw["win"]["small"],
                    tn=TN if pw["win"]["small"][1] % TN == 0 else 512)
    q_dsa = _proj("proj_qdsa", h, pw["w_in"],
                  functools.partial(_ep_rope, half=DSA_ROT // 2, period=1, roped=(0,),
                                    scale=HEAD_DIM ** -0.5 * LOG2E),
                  BF16, cols=pw["win"]["qb"], extras=[tab_extra(tab_d, tm)])
    q_idx = _proj("proj_qidx", h, pw["w_in"],
                  functools.partial(_ep_rope, half=IDX_ROT // 2, period=1, roped=(0,), scale=1.0),
                  BF16, cols=pw["win"]["qi"], extras=[tab_extra(tab_d, tm)])
    gates = _proj("proj_gates", h, pw["w_in"], _ep_sigmoid, BF16, cols=pw["win"]["gates"])

    (ckv, ckv_b, krope, krope_b, k_new, k_new_b, v_new, v_new_b, ki_new, ki_new_b, wi) = _finish(
        z_small, g_kv_lora, tab_m, tab_d, c_lora=c_lora, r_mla=r_mla, n_kv=n_kv, idx_scale=n_idx ** -0.5)

    kvw = n_kv * HEAD_DIM
    if past is None:
        l_valid = t
        lp = t
        ckv_all, krope_all, k_all, v_all, ki_all = ckv_b, krope_b, k_new_b, v_new_b, ki_new_b
    else:
        p_ckv, p_krope, p_k, p_v, p_ki = past
        l_valid = p_ckv.shape[1] + t
        lp = -(-l_valid // tk) * tk

        def cat(p, new, width):
            p = p.reshape(b, p.shape[1], -1).astype(BF16)
            if p.shape[2] < width:
                p = jnp.pad(p, ((0, 0), (0, 0), (0, width - p.shape[2])))
            full = jnp.concatenate([p, new.reshape(b, t, width)], axis=1)
            return jnp.pad(full, ((0, 0), (0, lp - l_valid), (0, 0))).reshape(b * lp, width)

        ckv_all = cat(p_ckv, ckv_b, c_lora)
        krope_all = cat(p_krope, krope_b, LANES)
        k_all = cat(p_k, k_new_b, kvw)
        v_all = cat(p_v, v_new_b, kvw)
        ki_all = cat(p_ki, ki_new_b, LANES)

    out_b = _dsa_attention(q_dsa.reshape(b, t, -1), q_idx.reshape(b, t, -1), wi.reshape(b, t, LANES),
                           ki_all.reshape(b, lp, LANES), k_all.reshape(b, lp, kvw), v_all.reshape(b, lp, kvw),
                           n_kv=n_kv, rep=n_heads // n_kv, n_idx=n_idx, tq=min(tq_dsa, t), tk=min(tk, lp),
                           q0=q0, l_valid=l_valid, topk=min(TOPK_MAX, l_valid // 4))

    tms = _pick(m, 2 * TM)
    q_full = _proj("mla_q_up", qn, pw["w_uq"],
                   functools.partial(_ep_rope, half=r_mla // 2, period=2, roped=(1,),
                                     scale=(HEAD_DIM + r_mla) ** -0.5 * LOG2E),
                   BF16, tm=tms, tn=2 * TN, extras=[tab_extra(tab_m, tms)])
    if past is not None and t * n_heads <= 1024 and n_heads % 8 == 0:
        kc = jnp.concatenate([ckv_all.reshape(b, lp, c_lora), krope_all.reshape(b, lp, LANES)], axis=2)
        q_abs = _absorb_q(q_full, pw["w_uk"], n_heads=n_heads, c_lora=c_lora)
        o_lat = _mla_abs_attention(q_abs.reshape(b, t * n_heads, c_lora + LANES), kc, t=t, n_heads=n_heads,
                                   c_lora=c_lora, tk=min(tk, lp), q0=q0, l_valid=l_valid)
        merged = _absorb_o(o_lat.reshape(m, n_heads * c_lora), pw["w_uv"], gates, out_b.reshape(m, d),
                           n_heads=n_heads, c_lora=c_lora)
    else:
        tml = _pick(b * lp, 2 * TM)
        k_full = _proj("mla_k_up", ckv_all, pw["w_uk"], _ep_kfull, BF16, out_cols=2 * n_heads * HEAD_DIM,
                       tm=tml, tn=2 * TN, extras=[(krope_all, (tml, LANES), lambda i, j: (i, 0))])
        v_mla = _proj("mla_v_up", ckv_all, pw["w_uv"], _ep_plain, BF16, tm=tml, tn=2 * TN)
        merged = _mla_attention(q_full.reshape(b, t, -1), k_full.reshape(b, lp, -1), v_mla.reshape(b, lp, -1),
                                gates.reshape(b, t, 2 * d), out_b,
                                n_heads=n_heads, hb=min(MLA_HEADS_PER_STEP, n_heads), tq=min(tq_mla, t),
                                tk=min(tk, lp), q0=q0, l_valid=l_valid).reshape(m, d)

    def gate_extra(gate, tm_, tn_):
        if per_row:
            return (gate, (tm_, tn_), lambda i, j: (i, j))
        return (gate, (1, tn_), lambda i, j: (0, j))

    tn = _pick(d, 1024)
    x1 = _proj("out_proj", merged, pw["w_out"], _ep_residual, F32,
                     extras=[(xf, (tm, tn), lambda i, j: (i, j)), gate_extra(gt1, tm, tn)])
    h2 = _norm(x1, g_norm2, sc2, sh2, BF16)
    u = _proj("mlp_up", h2, pw["w_up"], _ep_relu2, BF16)
    tm2 = _pick(m, 1024)
    x2 = _proj("mlp_down", u, pw["w_down"], _ep_residual, F32, tm=tm2, tk=2048,
               extras=[(x1, (tm2, tn), lambda i, j: (i, j)), gate_extra(gt2, tm2, tn)])
    rows = (ckv.reshape(b, t, c_lora), krope.reshape(b, t, r_mla), k_new.reshape(b, t, n_kv, HEAD_DIM),
            v_new.reshape(b, t, n_kv, HEAD_DIM), ki_new.reshape(b, t, HEAD_DIM))
    return x2, rows


def kernel(x_prompt, x_sample, c_prompt, c_sample, cache_mla_ckv, cache_mla_krope, cache_dsa_k, cache_dsa_v, cache_idx_k, w_ada, b_ada, g_norm1, w_in, g_q_lora, w_uq, g_kv_lora, w_uk, w_uv, w_out, g_norm2, w_up, w_down, g_final):
    depth = w_in.shape[0]
    bp, tp, d = x_prompt.shape
    bs, ts, _ = x_sample.shape
    n_heads = d // HEAD_DIM
    dims = (d, w_uq.shape[1], cache_mla_ckv.shape[-1], cache_mla_krope.shape[-1], n_heads,
            cache_dsa_k.shape[3], n_heads)
    past_len = cache_mla_ckv.shape[2]

    c_all = jnp.concatenate([c_prompt, c_sample], axis=0)
    n_c = c_all.shape[0]
    c_all = jnp.pad(c_all, ((0, -(-n_c // 16) * 16 - n_c), (0, 0)))

    xp, xs = x_prompt, x_sample
    rows_p, rows_s = [], []
    for l in range(depth):
        ada = _ada(c_all, w_ada[l], b_ada[l])
        pw = _prep_weights(w_in[l], w_uq[l], w_uk[l], w_uv[l], w_out[l], w_up[l], w_down[l], dims)
        norms = (g_norm1[l], g_q_lora[l], g_kv_lora[l], g_norm2[l])
        xp2, rp = _layer(xp, ada[0:bp], None, 0, pw, *norms, dims, tq_mla=TQ_MLA, tq_dsa=TQ_DSA, tk=TK_PROMPT)
        past = (cache_mla_ckv[l], cache_mla_krope[l], cache_dsa_k[l], cache_dsa_v[l], cache_idx_k[l])
        xs2, rs = _layer(xs, ada[bp:bp + bs], past, past_len, pw, *norms, dims, tq_mla=TQ_MLA, tq_dsa=TQ_DSA,
                         tk=TK_SAMPLE)
        xp, xs = xp2.reshape(bp, tp, d), xs2.reshape(bs, ts, d)
        rows_p.append(rp)
        rows_s.append(rs)

    y_prompt = _norm(xp.reshape(bp * tp, d), g_final, None, None, F32).reshape(bp, tp, d)
    y_sample = _norm(xs.reshape(bs * ts, d), g_final, None, None, F32).reshape(bs, ts, d)
    stack = lambda rows, i: jnp.stack([r[i] for r in rows])
    return (y_prompt, y_sample,
            stack(rows_p, 0), stack(rows_p, 1), stack(rows_p, 2), stack(rows_p, 3), stack(rows_p, 4),
            stack(rows_s, 0), stack(rows_s, 1), stack(rows_s, 2), stack(rows_s, 3), stack(rows_s, 4))
```

```python
import functools

import numpy as np
import jax
import jax.numpy as jnp
from jax import lax
from jax.experimental import pallas as pl
from jax.experimental.pallas import tpu as pltpu

F32 = jnp.float32
BF16 = jnp.bfloat16
I32 = jnp.int32

LANES = 128
CHUNK = 64
CHUNK_LOG2 = 6
EPS = 1e-6
NEG = -2.0 ** 100
INT_MIN = -2 ** 31
INDEX_BITS = 24
MLA_THETA = 10000.0
ROPE_THETA = 500000.0
HEAD_DIM = 128
DSA_ROT = HEAD_DIM // 4
IDX_ROT = HEAD_DIM // 4
TOPK_MAX = 256
N_ADA = 6
VMEM_LIMIT = 56 * 1024 * 1024
VMEM_LIMIT_DSA = 60 * 1024 * 1024
TQ_MLA, TQ_DSA = 512, 128
MLA_HEADS_PER_STEP = 4
DSA_GROUPS_PER_PASS = 4
TK_PROMPT, TK_SAMPLE = 512, 384


def _cparams(sem):
    return pltpu.CompilerParams(dimension_semantics=sem, vmem_limit_bytes=VMEM_LIMIT)


def _pick(n, pref):
    if n <= pref:
        return n
    t = pref
    while n % t:
        t //= 2
    return t


def _rope_table(pos, theta, rot_dim):
    half = rot_dim // 2
    inv = theta ** (-jnp.arange(half, dtype=F32) / half)
    ang = pos.astype(F32)[:, None] * inv[None, :]
    cos, sin = jnp.cos(ang), jnp.sin(ang)
    n = pos.shape[0]
    ones = jnp.ones((n, LANES - rot_dim), F32)
    zeros = jnp.zeros((n, LANES - rot_dim), F32)
    zh = jnp.zeros((n, half), F32)
    a = jnp.concatenate([cos, cos, ones], axis=1)
    b = jnp.concatenate([-sin, zh, zeros], axis=1)
    c = jnp.concatenate([zh, sin, zeros], axis=1)
    return jnp.concatenate([a, b, c], axis=1)


def _rope_slab(x, tab, half):
    a = tab[:, 0:LANES]
    b = tab[:, LANES:2 * LANES]
    c = tab[:, 2 * LANES:3 * LANES]
    return x * a + pltpu.roll(x, LANES - half, 1) * b + pltpu.roll(x, half, 1) * c


def _mm_body(*refs, nk, n_extra, n_out, epilogue):
    a_ref, w_ref = refs[0], refs[1]
    extra = refs[2:2 + n_extra]
    outs = refs[2 + n_extra:2 + n_extra + n_out]
    a = a_ref[...].astype(BF16)
    w = w_ref[...].astype(BF16)
    part = jnp.dot(a, w, preferred_element_type=F32)
    if nk == 1:
        epilogue(part, extra, outs)
        return
    acc_ref = refs[2 + n_extra + n_out]
    k = pl.program_id(2)

    @pl.when(k == 0)
    def _():
        acc_ref[...] = part

    @pl.when(k > 0)
    def _():
        acc_ref[...] += part

    @pl.when(k == nk - 1)
    def _():
        epilogue(acc_ref[...], extra, outs)


def _matmul(a, w, *, n, col0, tm, tn, tk, extras, outs, epilogue, name, order="nm"):
    m, kdim = a.shape
    assert col0 % tn == 0
    jb = col0 // tn
    gi, gj, gk = m // tm, n // tn, kdim // tk
    if order == "nm":
        grid = (gj, gi, gk)
        wrap = lambda f: (lambda pj, pi, pk: f(pi, pj, pk))
    else:
        grid = (gi, gj, gk)
        wrap = lambda f: (lambda pi, pj, pk: f(pi, pj, pk))
    in_specs = [
        pl.BlockSpec((tm, tk), wrap(lambda i, j, k: (i, k))),
        pl.BlockSpec((tk, tn), wrap(lambda i, j, k: (k, j + jb))),
    ]
    args = [a, w]
    for arr, blk, fn in extras:
        in_specs.append(pl.BlockSpec(blk, wrap(lambda i, j, k, fn=fn: fn(i, j))))
        args.append(arr)
    out_specs = [pl.BlockSpec(blk, wrap(lambda i, j, k, fn=fn: fn(i, j))) for _, blk, fn in outs]
    out_shape = [s for s, _, _ in outs]
    scratch = [pltpu.VMEM((tm, tn), F32)] if gk > 1 else []
    body = functools.partial(_mm_body, nk=gk, n_extra=len(extras), n_out=len(outs), epilogue=epilogue)
    res = pl.pallas_call(
        body, grid=grid, in_specs=in_specs, out_specs=out_specs, out_shape=out_shape,
        scratch_shapes=scratch, name=name,
        compiler_params=_cparams(("parallel", "parallel", "arbitrary")),
    )(*args)
    return res


def _ep_plain(acc, extra, outs):
    outs[0][...] = acc.astype(outs[0].dtype)


def _ep_rmsnorm(acc, extra, outs):
    g = extra[0][...]
    y = acc * lax.rsqrt(jnp.mean(acc * acc, axis=-1, keepdims=True) + EPS) * g
    outs[0][...] = y.astype(outs[0].dtype)


def _ep_sigmoid(acc, extra, outs):
    outs[0][...] = jax.nn.sigmoid(acc).astype(outs[0].dtype)


def _ep_relu2(acc, extra, outs):
    r = jnp.maximum(acc, 0.0)
    outs[0][...] = (r * r).astype(outs[0].dtype)


def _ep_residual(acc, extra, outs):
    x = extra[0][...]
    gate = extra[1][...]
    outs[0][...] = x + gate * acc


def _ep_rope(acc, extra, outs, *, half, period, roped, scale):
    tab = extra[0][...]
    o = outs[0]
    for s in range(acc.shape[1] // LANES):
        x = acc[:, s * LANES:(s + 1) * LANES]
        if (s % period) in roped:
            x = _rope_slab(x, tab, half)
        if scale != 1.0:
            x = x * scale
        o[:, s * LANES:(s + 1) * LANES] = x.astype(o.dtype)


def _ep_kfull(acc, extra, outs):
    kr = extra[0][...]
    o = outs[0]
    for h in range(acc.shape[1] // LANES):
        o[:, (2 * h) * LANES:(2 * h + 1) * LANES] = acc[:, h * LANES:(h + 1) * LANES].astype(o.dtype)
        o[:, (2 * h + 1) * LANES:(2 * h + 2) * LANES] = kr


TM, TN, TK = 512, 1024, 4096


def _proj(name, a, w, ep, out_dtype, *, cols=None, tm=TM, tn=TN, tk=TK, extras=(), out_cols=None, order="nm"):
    m, k = a.shape
    col0, n = (0, w.shape[1]) if cols is None else cols
    tm, tn, tk = _pick(m, tm), _pick(n, tn), _pick(k, tk)
    oc = n if out_cols is None else out_cols
    outs = [(jax.ShapeDtypeStruct((m, oc), out_dtype), (tm, tn * oc // n), lambda i, j: (i, j))]
    return _matmul(a, w, n=n, col0=col0, tm=tm, tn=tn, tk=tk, extras=list(extras), outs=outs, epilogue=ep,
                   name=name, order=order)[0]


def _ada_body(c_ref, w_ref, b_ref, o_ref):
    c = c_ref[...]
    a = (c * jax.nn.sigmoid(c)).astype(BF16)
    o_ref[...] = jnp.dot(a, w_ref[...].astype(BF16), preferred_element_type=F32) + b_ref[...]


def _ada(c_all, w_ada, b_ada):
    r, d = c_all.shape
    n = w_ada.shape[1]
    tn = _pick(n, 512)
    return pl.pallas_call(
        _ada_body, grid=(n // tn,),
        in_specs=[pl.BlockSpec((r, d), lambda j: (0, 0)),
                  pl.BlockSpec((d, tn), lambda j: (0, j)),
                  pl.BlockSpec((1, tn), lambda j: (0, j))],
        out_specs=pl.BlockSpec((r, tn), lambda j: (0, j)),
        out_shape=jax.ShapeDtypeStruct((r, n), F32), name="adaln",
        compiler_params=_cparams(("parallel",)),
    )(c_all, w_ada, b_ada.reshape(1, n))


def _norm_body(*refs, modulate):
    x = refs[0][...]
    g = refs[1][...]
    y = x * lax.rsqrt(jnp.mean(x * x, axis=-1, keepdims=True) + EPS) * g
    if modulate:
        y = y * (1.0 + refs[2][...]) + refs[3][...]
        o = refs[4]
    else:
        o = refs[2]
    o[...] = y.astype(o.dtype)


def _norm(x, g, sc, sh, out_dtype):
    m, d = x.shape
    tm = _pick(m, 256)
    in_specs = [pl.BlockSpec((tm, d), lambda i: (i, 0)), pl.BlockSpec((1, d), lambda i: (0, 0))]
    args = [x, g.reshape(1, d)]
    if sc is not None:
        per_row = sc.shape[0] == m
        blk = (tm, d) if per_row else (1, d)
        fn = (lambda i: (i, 0)) if per_row else (lambda i: (0, 0))
        in_specs += [pl.BlockSpec(blk, fn), pl.BlockSpec(blk, fn)]
        args += [sc, sh]
    return pl.pallas_call(
        functools.partial(_norm_body, modulate=sc is not None), grid=(m // tm,),
        in_specs=in_specs, out_specs=pl.BlockSpec((tm, d), lambda i: (i, 0)),
        out_shape=jax.ShapeDtypeStruct((m, d), out_dtype), name="rmsnorm",
        compiler_params=_cparams(("parallel",)),
    )(*args)


def _finish_body(z_ref, g_ref, tm_ref, td_ref,
                 ckv_o, ckvb_o, kr_o, krb_o, k_o, kb_o, v_o, vb_o, ki_o, kib_o, wi_o,
                 *, c_lora, r_mla, n_kv, idx_scale):
    tab_m = tm_ref[...]
    tab_d = td_ref[...]
    off = 0
    ckv = z_ref[:, off:off + c_lora]
    ckv = ckv * lax.rsqrt(jnp.mean(ckv * ckv, axis=-1, keepdims=True) + EPS) * g_ref[...]
    ckv_o[...] = ckv
    ckvb_o[...] = ckv.astype(BF16)
    off += c_lora
    kr = _rope_slab(z_ref[:, off:off + LANES], tab_m, r_mla // 2)
    kr_o[...] = kr[:, 0:r_mla]
    krb_o[...] = kr.astype(BF16)
    off += LANES
    for h in range(n_kv):
        kh = _rope_slab(z_ref[:, off + h * LANES:off + (h + 1) * LANES], tab_d, DSA_ROT // 2)
        k_o[:, h * LANES:(h + 1) * LANES] = kh
        kb_o[:, h * LANES:(h + 1) * LANES] = kh.astype(BF16)
    off += n_kv * LANES
    v = z_ref[:, off:off + n_kv * LANES]
    v_o[...] = v
    vb_o[...] = v.astype(BF16)
    off += n_kv * LANES
    ki = _rope_slab(z_ref[:, off:off + LANES], tab_d, IDX_ROT // 2)
    ki_o[...] = ki
    kib_o[...] = ki.astype(BF16)
    off += LANES
    wi_o[...] = z_ref[:, off:off + LANES] * idx_scale


def _finish(z, g_kv, tab_m, tab_d, *, c_lora, r_mla, n_kv, idx_scale):
    m, zc = z.shape
    tm = _pick(m, 256)
    kvw = n_kv * LANES
    row = lambda i: (i, 0)
    shapes = [(c_lora, F32), (c_lora, BF16), (r_mla, F32), (LANES, BF16), (kvw, F32), (kvw, BF16),
              (kvw, F32), (kvw, BF16), (LANES, F32), (LANES, BF16), (LANES, F32)]
    return pl.pallas_call(
        functools.partial(_finish_body, c_lora=c_lora, r_mla=r_mla, n_kv=n_kv, idx_scale=idx_scale),
        grid=(m // tm,),
        in_specs=[pl.BlockSpec((tm, zc), row), pl.BlockSpec((1, c_lora), lambda i: (0, 0)),
                  pl.BlockSpec((tm, 3 * LANES), row), pl.BlockSpec((tm, 3 * LANES), row)],
        out_specs=[pl.BlockSpec((tm, w), row) for w, _ in shapes],
        out_shape=[jax.ShapeDtypeStruct((m, w), dt) for w, dt in shapes], name="finish_small",
        compiler_params=_cparams(("parallel",)),
    )(z, g_kv.reshape(1, c_lora), tab_m, tab_d)


def _row_limits(q_first, tq, l_valid):
    rows = q_first + lax.broadcasted_iota(I32, (tq, 1), 0)
    chunk_end = lambda p: (lax.shift_right_logical(p, CHUNK_LOG2) + 1) * CHUNK
    lim = jnp.minimum(chunk_end(rows), l_valid)
    lim_min = jnp.minimum(chunk_end(q_first), l_valid)
    lim_max = jnp.minimum(chunk_end(q_first + tq - 1), l_valid)
    return lim, lim_min, lim_max


LOG2E = 1.4426950408889634


def _lane_tile(x, width):
    return x if width == LANES else jnp.concatenate([x] * (width // LANES), axis=1)


def _softmax_update(s, v, m_prev, l_prev, acc_prev):
    m_new = jnp.maximum(m_prev, jnp.max(s, axis=1, keepdims=True))
    p = jnp.exp2(s - _lane_tile(m_new, s.shape[1]))
    alpha = jnp.exp2(m_prev - m_new)
    l_new = alpha * l_prev + jnp.sum(p, axis=1, keepdims=True)
    acc_new = _lane_tile(alpha, acc_prev.shape[1]) * acc_prev + jnp.dot(p.astype(BF16), v,
                                                                        preferred_element_type=F32)
    return m_new, l_new, acc_new


def _softmax_step(s, v, m_sc, acc_sc):
    sb = s.astype(BF16)
    m_prev = m_sc[...]
    m_new = jnp.maximum(m_prev, jnp.max(sb, axis=1, keepdims=True).astype(F32))
    p = jnp.exp2(sb - _lane_tile(m_new.astype(BF16), s.shape[1]))
    alpha = jnp.exp2(m_prev - m_new)
    v_ones = jnp.concatenate([v, jnp.ones(v.shape, BF16)], axis=1)
    acc_sc[...] = _lane_tile(alpha, 2 * LANES) * acc_sc[...] + jnp.dot(p, v_ones, preferred_element_type=F32)
    m_sc[...] = m_new


def _softmax_result(acc_sc):
    acc = acc_sc[...]
    return acc[:, 0:LANES] * (1.0 / acc[:, LANES:2 * LANES])


_NT = (((1,), (1,)), ((), ()))


def _gate_mix(out_a, ga, gb, out_b):
    return (ga.astype(F32) * out_a + gb.astype(F32) * out_b.astype(F32)).astype(BF16)


def _mla_body(q_ref, k_ref, v_ref, ga_ref, gb_ref, ob_ref, o_ref, m_sc, acc_sc, s_sc,
              *, hb, tq, tk, q0, l_valid):
    q_first = q0 + pl.program_id(2) * tq
    lim, lim_min, lim_max = _row_limits(q_first, tq, l_valid)
    n_full = lim_min // tk
    n_all = (lim_max + tk - 1) // tk
    col = lax.broadcasted_iota(I32, (1, tk), 1)
    m_sc[...] = jnp.full(m_sc.shape, NEG, F32)
    acc_sc[...] = jnp.zeros(acc_sc.shape, F32)

    def scores(c, j):
        start = pl.multiple_of(c * tk, tk)
        return lax.dot_general(q_ref[:, j * 2 * LANES:(j + 1) * 2 * LANES],
                               k_ref[pl.ds(start, tk), j * 2 * LANES:(j + 1) * 2 * LANES], _NT,
                               preferred_element_type=F32)

    for j in range(hb):
        s_sc[j] = scores(0, j)

    def step(c, masked, last):
        start = pl.multiple_of(c * tk, tk)
        for j in range(hb):
            s = s_sc[j]
            if masked:
                s = jnp.where(col + start < lim, s, NEG)
            _softmax_step(s, v_ref[pl.ds(start, tk), j * LANES:(j + 1) * LANES], m_sc.at[j], acc_sc.at[j])
            if not last:
                s_sc[j] = scores(c + 1, j)

    def full_step(c, carry):
        step(c, False, False)
        return carry

    def masked_step(c, carry):
        step(c, True, False)
        return carry

    n_rot = n_all - 1
    n_rot_full = jnp.minimum(n_full, n_rot)
    lax.fori_loop(0, n_rot_full, full_step, 0)
    lax.fori_loop(n_rot_full, n_rot, masked_step, 0)
    step(n_rot, True, True)
    for j in range(hb):
        cols = slice(j * LANES, (j + 1) * LANES)
        o_ref[:, cols] = _gate_mix(_softmax_result(acc_sc.at[j]), ga_ref[:, cols], gb_ref[:, cols], ob_ref[:, cols])


def _mla_attention(q_full, k_full, v, gates, out_b, *, n_heads, hb, tq, tk, q0, l_valid):
    b, t, _ = q_full.shape
    lp = k_full.shape[1]
    kv_mode = dict(pipeline_mode=pl.Buffered(1)) if t // tq > 2 else {}
    stat = pltpu.VMEM((hb, tq, LANES), F32)
    return pl.pallas_call(
        functools.partial(_mla_body, hb=hb, tq=tq, tk=tk, q0=q0, l_valid=l_valid),
        grid=(b, n_heads // hb, t // tq),
        in_specs=[pl.BlockSpec((None, tq, hb * 2 * LANES), lambda bi, h, qi: (bi, qi, h)),
                  pl.BlockSpec((None, lp, hb * 2 * LANES), lambda bi, h, qi: (bi, 0, h), **kv_mode),
                  pl.BlockSpec((None, lp, hb * LANES), lambda bi, h, qi: (bi, 0, h), **kv_mode),
                  pl.BlockSpec((None, tq, hb * LANES), lambda bi, h, qi: (bi, qi, h)),
                  pl.BlockSpec((None, tq, hb * LANES), lambda bi, h, qi: (bi, qi, n_heads // hb + h)),
                  pl.BlockSpec((None, tq, hb * LANES), lambda bi, h, qi: (bi, qi, h))],
        out_specs=pl.BlockSpec((None, tq, hb * LANES), lambda bi, h, qi: (bi, qi, h)),
        out_shape=jax.ShapeDtypeStruct((b, t, n_heads * LANES), BF16),
        scratch_shapes=[stat, pltpu.VMEM((hb, tq, 2 * LANES), F32), pltpu.VMEM((hb, tq, tk), F32)],
        name="mla_attention",
        compiler_params=_cparams(("parallel", "parallel", "arbitrary")),
    )(q_full, k_full, v, gates, gates, out_b)


def _absorb_q_body(q_ref, w_ref, o_ref, *, c_lora):
    lat = lax.dot_general(q_ref[:, 0:LANES], w_ref[...], _NT, preferred_element_type=F32)
    o_ref[:, 0:c_lora] = lat.astype(o_ref.dtype)
    o_ref[:, c_lora:c_lora + LANES] = q_ref[:, LANES:2 * LANES]


def _absorb_q(q_full, w_uk, *, n_heads, c_lora):
    m = q_full.shape[0]
    return pl.pallas_call(
        functools.partial(_absorb_q_body, c_lora=c_lora), grid=(n_heads,),
        in_specs=[pl.BlockSpec((m, 2 * LANES), lambda h: (0, h)), pl.BlockSpec((c_lora, LANES), lambda h: (0, h))],
        out_specs=pl.BlockSpec((m, c_lora + LANES), lambda h: (0, h)),
        out_shape=jax.ShapeDtypeStruct((m, n_heads * (c_lora + LANES)), BF16), name="mla_absorb_q",
        compiler_params=_cparams(("parallel",)),
    )(q_full, w_uk)


def _absorb_o_body(o_lat_ref, w_ref, ga_ref, gb_ref, ob_ref, o_ref):
    out_a = jnp.dot(o_lat_ref[...], w_ref[...], preferred_element_type=F32)
    o_ref[...] = _gate_mix(out_a, ga_ref[...], gb_ref[...], ob_ref[...])


def _absorb_o(o_lat, w_uv, gates, out_b, *, n_heads, c_lora):
    m = o_lat.shape[0]
    head = lambda h: (0, h)
    return pl.pallas_call(
        _absorb_o_body, grid=(n_heads,),
        in_specs=[pl.BlockSpec((m, c_lora), head), pl.BlockSpec((c_lora, LANES), head),
                  pl.BlockSpec((m, LANES), head), pl.BlockSpec((m, LANES), lambda h: (0, n_heads + h)),
                  pl.BlockSpec((m, LANES), head)],
        out_specs=pl.BlockSpec((m, LANES), head),
        out_shape=jax.ShapeDtypeStruct((m, n_heads * LANES), BF16), name="mla_absorb_o",
        compiler_params=_cparams(("parallel",)),
    )(o_lat, w_uv, gates, gates, out_b)


def _mla_abs_body(q_ref, kc_ref, o_ref, *, bb, t, n_heads, c_lora, tk, q0, l_valid):
    rows = t * n_heads
    tok = lax.broadcasted_iota(I32, (t, n_heads, 1), 0).reshape(rows, 1)
    lim = jnp.minimum((lax.shift_right_logical(q0 + tok, CHUNK_LOG2) + 1) * CHUNK, l_valid)
    lim_min = min((q0 // CHUNK + 1) * CHUNK, l_valid)
    lim_max = min(((q0 + t - 1) // CHUNK + 1) * CHUNK, l_valid)
    col = lax.broadcasted_iota(I32, (1, tk), 1)
    for j in range(bb):
        q = q_ref[j]
        m = jnp.full((rows, LANES), NEG, F32)
        l = jnp.zeros((rows, LANES), F32)
        acc = jnp.zeros((rows, c_lora), F32)
        for c in range(-(-lim_max // tk)):
            kc = kc_ref[j, c * tk:(c + 1) * tk, :]
            s = lax.dot_general(q, kc, _NT, preferred_element_type=F32)
            if (c + 1) * tk > lim_min:
                s = jnp.where(col + c * tk < lim, s, NEG)
            m, l, acc = _softmax_update(s, kc[:, 0:c_lora], m, l, acc)
        o_ref[j] = (acc * _lane_tile(1.0 / l, c_lora)).astype(o_ref.dtype)


def _mla_abs_attention(q_abs, kc, *, t, n_heads, c_lora, tk, q0, l_valid):
    b, rows, dk = q_abs.shape
    lp = kc.shape[1]
    bb = _pick(b, 4)
    return pl.pallas_call(
        functools.partial(_mla_abs_body, bb=bb, t=t, n_heads=n_heads, c_lora=c_lora, tk=tk, q0=q0, l_valid=l_valid),
        grid=(b // bb,),
        in_specs=[pl.BlockSpec((bb, rows, dk), lambda i: (i, 0, 0)), pl.BlockSpec((bb, lp, dk), lambda i: (i, 0, 0))],
        out_specs=pl.BlockSpec((bb, rows, c_lora), lambda i: (i, 0, 0)),
        out_shape=jax.ShapeDtypeStruct((b, rows, c_lora), BF16), name="mla_absorbed_attention",
        compiler_params=_cparams(("parallel",)),
    )(q_abs, kc)


def _dsa_body(qb_ref, qi_ref, wi_ref, ki_ref, kt_ref, v_ref, o_ref,
              keys_sc, m_sc, acc_sc, s_sc, *, tq, tk, q0, l_valid, n_idx, n_kv, rep, topk):
    q_first = q0 + pl.program_id(1) * tq
    lim, _, lim_max = _row_limits(q_first, tq, l_valid)
    n_all = (lim_max + tk - 1) // tk
    col = lax.broadcasted_iota(I32, (1, tk), 1)
    w = wi_ref[...]

    def score_chunk(c, carry):
        start = pl.multiple_of(c * tk, tk)
        kic = ki_ref[pl.ds(start, tk), :]
        s = jnp.zeros((tq, tk), F32)
        for h in range(n_idx):
            y = lax.dot_general(qi_ref[:, h * LANES:(h + 1) * LANES], kic, _NT,
                                preferred_element_type=F32)
            s = s + w[:, h:h + 1] * jnp.maximum(y, 0.0)
        s = jnp.where(s == 0.0, 0.0, s)
        bits = lax.bitcast_convert_type(s, I32)
        key = jnp.where(bits >= 0, bits, bits ^ jnp.int32(0x7FFFFFFF))
        keys_sc[c] = jnp.where(col + start < lim, key, jnp.int32(INT_MIN))
        return carry

    lax.fori_loop(0, n_all, score_chunk, 0)

    def count_rows(indicator):
        def count_chunk(c, acc):
            ind = indicator(c, keys_sc[c])
            part = ind[:, 0:LANES]
            for j in range(1, tk // LANES):
                part = part + ind[:, j * LANES:(j + 1) * LANES]
            return acc + part

        acc = lax.fori_loop(0, n_all, count_chunk, jnp.zeros((tq, LANES), F32))
        return jnp.sum(acc, axis=1, keepdims=True)

    def count_ge(cand):
        return count_rows(lambda c, kk: jnp.where(kk >= cand, 1.0, 0.0))

    def bit_pass(i, carry):
        t, n_at_t = carry
        cand = t + lax.shift_left(jnp.int32(1), jnp.int32(31) - i)
        n = count_ge(cand)
        keep = n >= topk
        return jnp.where(keep, cand, t), jnp.where(keep, n, n_at_t)

    t, n_at_t = lax.fori_loop(0, 32, bit_pass, (jnp.full((tq, 1), INT_MIN, I32), jnp.zeros((tq, 1), F32)))
    thr = jnp.maximum(t, jnp.int32(INT_MIN + 1))

    @pl.when(jnp.max(n_at_t) > topk)
    def _():
        slots = topk - count_ge(thr + 1)

        def tied_before(bound):
            return count_rows(lambda c, kk: jnp.where(kk == thr, jnp.where(col + c * tk < bound, 1.0, 0.0), 0.0))

        def index_pass(i, j_last):
            cand = j_last + lax.shift_left(jnp.int32(1), jnp.int32(INDEX_BITS - 1) - i)
            return jnp.where(tied_before(cand) < slots, cand, j_last)

        j_last = lax.fori_loop(0, INDEX_BITS, index_pass, jnp.zeros((tq, 1), I32))

        def demote(c, carry):
            kk = keys_sc[c]
            keys_sc[c] = jnp.where(kk == thr, jnp.where(col + c * tk > j_last, thr - 1, kk), kk)
            return carry

        lax.fori_loop(0, n_all, demote, 0)

    rows = rep * tq
    tok = jnp.bitwise_and(lax.broadcasted_iota(I32, (rows, LANES), 0), tq - 1)
    eye = jnp.where(tok == lax.broadcasted_iota(I32, (rows, LANES), 1), 1.0, 0.0).astype(BF16)

    gb = m_sc.shape[0]

    def mask_bias(c):
        bias = jnp.where(keys_sc[c] >= thr, 0.0, NEG).astype(BF16)
        if tq < LANES:
            bias = jnp.concatenate([bias, jnp.zeros((LANES - tq, tk), BF16)], axis=0)
        return bias

    def scores(c, g, bias):
        qs = jnp.concatenate([qb_ref[:, (g * rep + r) * LANES:(g * rep + r + 1) * LANES]
                              for r in range(rep)], axis=0)
        return jnp.dot(jnp.concatenate([qs, eye], axis=1), jnp.concatenate([kt_ref[g, c], bias], axis=0),
                       preferred_element_type=F32)

    def attend(c, g0, nxt_c, nxt_g0):
        start = pl.multiple_of(c * tk, tk)
        bias = None if nxt_g0 is None else mask_bias(nxt_c)
        for j in range(gb):
            g = g0 + j
            _softmax_step(s_sc[j], v_ref[pl.ds(start, tk), g * LANES:(g + 1) * LANES], m_sc.at[j], acc_sc.at[j])
            if nxt_g0 is not None:
                s_sc[j] = scores(nxt_c, nxt_g0 + j, bias)

    bias0 = mask_bias(0)
    for j in range(gb):
        s_sc[j] = scores(0, j, bias0)
    for g0 in range(0, n_kv, gb):
        m_sc[...] = jnp.full(m_sc.shape, NEG, F32)
        acc_sc[...] = jnp.zeros(acc_sc.shape, F32)

        def attend_chunk(c, carry, g0=g0):
            attend(c, g0, c + 1, g0)
            return carry

        lax.fori_loop(0, n_all - 1, attend_chunk, 0)
        attend(n_all - 1, g0, 0, g0 + gb if g0 + gb < n_kv else None)
        for j in range(gb):
            o = _softmax_result(acc_sc.at[j])
            for r in range(rep):
                h = (g0 + j) * rep + r
                o_ref[:, h * LANES:(h + 1) * LANES] = o[r * tq:(r + 1) * tq].astype(o_ref.dtype)


def _dsa_attention(qb, qi, wi, ki, k, v, *, n_kv, rep, n_idx, tq, tk, q0, l_valid, topk):
    b, t, _ = qb.shape
    lp = k.shape[1]
    kvw = n_kv * LANES
    assert tq <= LANES and tq & (tq - 1) == 0, "the token one-hot needs a power-of-two block of at most 128 rows"
    kt = k.reshape(b, lp // tk, tk, n_kv, LANES).transpose(0, 3, 1, 4, 2)
    kv_mode = dict(pipeline_mode=pl.Buffered(1)) if t // tq > 2 else {}
    stat = pltpu.VMEM((min(DSA_GROUPS_PER_PASS, n_kv), rep * tq, LANES), F32)
    return pl.pallas_call(
        functools.partial(_dsa_body, tq=tq, tk=tk, q0=q0, l_valid=l_valid, n_idx=n_idx, n_kv=n_kv, rep=rep,
                          topk=topk),
        grid=(b, t // tq),
        in_specs=[pl.BlockSpec((None, tq, n_kv * rep * LANES), lambda bi, qb_: (bi, qb_, 0)),
                  pl.BlockSpec((None, tq, n_idx * LANES), lambda bi, qb_: (bi, qb_, 0)),
                  pl.BlockSpec((None, tq, LANES), lambda bi, qb_: (bi, qb_, 0)),
                  pl.BlockSpec((None, lp, LANES), lambda bi, qb_: (bi, 0, 0), **kv_mode),
                  pl.BlockSpec((None, n_kv, lp // tk, LANES, tk), lambda bi, qb_: (bi, 0, 0, 0, 0), **kv_mode),
                  pl.BlockSpec((None, lp, kvw), lambda bi, qb_: (bi, 0, 0), **kv_mode)],
        out_specs=pl.BlockSpec((None, tq, n_kv * rep * LANES), lambda bi, qb_: (bi, qb_, 0)),
        out_shape=jax.ShapeDtypeStruct((b, t, n_kv * rep * LANES), BF16),
        scratch_shapes=[pltpu.VMEM((lp // tk, tq, tk), I32), stat,
                        pltpu.VMEM((stat.shape[0], rep * tq, 2 * LANES), F32),
                        pltpu.VMEM((stat.shape[0], rep * tq, tk), F32)],
        name="dsa_attention",
        compiler_params=pltpu.CompilerParams(dimension_semantics=("parallel", "arbitrary"),
                                             vmem_limit_bytes=VMEM_LIMIT_DSA),
    )(qb, qi, wi, ki, kt, v)


def _pad_cols(w, width):
    return jnp.pad(w, ((0, 0), (0, width - w.shape[1])))


def _prep_weights(w_in, w_uq, w_uk, w_uv, w_out, w_up, w_down, dims):
    d, q_lora, c_lora, r_mla, n_heads, n_kv, n_idx = dims
    splits = (q_lora, c_lora, r_mla, n_heads * HEAD_DIM, n_kv * HEAD_DIM, n_kv * HEAD_DIM,
              n_idx * HEAD_DIM, HEAD_DIM, n_idx, d, d)
    offs = np.concatenate([[0], np.cumsum(splits)])
    sec = [w_in[:, int(offs[i]):int(offs[i + 1])] for i in range(len(splits))]
    small = jnp.concatenate([sec[1], _pad_cols(sec[2], LANES), sec[4], sec[5], sec[7],
                             _pad_cols(sec[8], LANES)], axis=1)
    small = _pad_cols(small, -(-small.shape[1] // 512) * 512)
    parts = [("q", sec[0]), ("small", small), ("qb", sec[3]), ("qi", sec[6]),
             ("gates", jnp.concatenate([sec[9], sec[10]], axis=1))]
    windows, off = {}, 0
    for i, (key, p) in enumerate(parts):
        windows[key] = (off, p.shape[1])
        parts[i] = (key, _pad_cols(p, -(-p.shape[1] // TN) * TN))
        off += parts[i][1].shape[1]
    wq = w_uq.reshape(q_lora, n_heads, HEAD_DIM + r_mla)
    wq = jnp.pad(wq, ((0, 0), (0, 0), (0, 2 * LANES - HEAD_DIM - r_mla)))
    return dict(
        w_in=jnp.concatenate([p for _, p in parts], axis=1).astype(BF16), win=windows,
        w_uq=wq.reshape(q_lora, n_heads * 2 * LANES).astype(BF16),
        w_uk=w_uk.reshape(c_lora, n_heads * HEAD_DIM).astype(BF16),
        w_uv=w_uv.reshape(c_lora, n_heads * HEAD_DIM).astype(BF16),
        w_out=w_out.astype(BF16), w_up=w_up.astype(BF16), w_down=w_down.astype(BF16))


def _layer(x, ada, past, q0, pw, g_norm1, g_q_lora, g_kv_lora, g_norm2, dims, tq_mla, tq_dsa, tk):
    d, q_lora, c_lora, r_mla, n_heads, n_kv, n_idx = dims
    b, t, _ = x.shape
    m = b * t
    xf = x.reshape(m, d)
    if ada.shape[0] == 1:
        mods = [ada[:, i * d:(i + 1) * d] for i in range(N_ADA)]
    else:
        mods = [jnp.repeat(ada[:, i * d:(i + 1) * d], t, axis=0) for i in range(N_ADA)]
    sh1, sc1, gt1, sh2, sc2, gt2 = mods
    per_row = ada.shape[0] != 1

    pos = q0 + jnp.arange(t, dtype=I32)
    tab_m = jnp.tile(_rope_table(pos, MLA_THETA, r_mla), (b, 1))
    tab_d = jnp.tile(_rope_table(pos, ROPE_THETA, DSA_ROT), (b, 1))

    h = _norm(xf, g_norm1, sc1, sh1, BF16)

    def tab_extra(tab, tm):
        return (tab, (tm, 3 * LANES), lambda i, j: (i, 0))

    tm = _pick(m, 512)
    qn = _proj("proj_qlat", h, pw["w_in"], _ep_rmsnorm, BF16, cols=pw["win"]["q"], tn=q_lora,
                     extras=[(g_q_lora.reshape(1, q_lora), (1, q_lora), lambda i, j: (0, 0))])
    z_small = _proj("proj_small", h, pw["w_in"], _ep_plain, F32, cols=pw["win"]["small"],
                    tn=TN if pw["win"]["small"][1] % TN == 0 else 512)
    q_dsa = _proj("proj_qdsa", h, pw["w_in"],
                  functools.partial(_ep_rope, half=DSA_ROT // 2, period=1, roped=(0,),
                                    scale=HEAD_DIM ** -0.5 * LOG2E),
                  BF16, cols=pw["win"]["qb"], extras=[tab_extra(tab_d, tm)])
    q_idx = _proj("proj_qidx", h, pw["w_in"],
                  functools.partial(_ep_rope, half=IDX_ROT // 2, period=1, roped=(0,), scale=1.0),
                  BF16, cols=pw["win"]["qi"], extras=[tab_extra(tab_d, tm)])
    gates = _proj("proj_gates", h, pw["w_in"], _ep_sigmoid, BF16, cols=pw["win"]["gates"])

    (ckv, ckv_b, krope, krope_b, k_new, k_new_b, v_new, v_new_b, ki_new, ki_new_b, wi) = _finish(
        z_small, g_kv_lora, tab_m, tab_d, c_lora=c_lora, r_mla=r_mla, n_kv=n_kv, idx_scale=n_idx ** -0.5)

    kvw = n_kv * HEAD_DIM
    if past is None:
        l_valid = t
        lp = t
        ckv_all, krope_all, k_all, v_all, ki_all = ckv_b, krope_b, k_new_b, v_new_b, ki_new_b
    else:
        p_ckv, p_krope, p_k, p_v, p_ki = past
        l_valid = p_ckv.shape[1] + t
        lp = -(-l_valid // tk) * tk

        def cat(p, new, width):
            p = p.reshape(b, p.shape[1], -1).astype(BF16)
            if p.shape[2] < width:
                p = jnp.pad(p, ((0, 0), (0, 0), (0, width - p.shape[2])))
            full = jnp.concatenate([p, new.reshape(b, t, width)], axis=1)
            return jnp.pad(full, ((0, 0), (0, lp - l_valid), (0, 0))).reshape(b * lp, width)

        ckv_all = cat(p_ckv, ckv_b, c_lora)
        krope_all = cat(p_krope, krope_b, LANES)
        k_all = cat(p_k, k_new_b, kvw)
        v_all = cat(p_v, v_new_b, kvw)
        ki_all = cat(p_ki, ki_new_b, LANES)

    out_b = _dsa_attention(q_dsa.reshape(b, t, -1), q_idx.reshape(b, t, -1), wi.reshape(b, t, LANES),
                           ki_all.reshape(b, lp, LANES), k_all.reshape(b, lp, kvw), v_all.reshape(b, lp, kvw),
                           n_kv=n_kv, rep=n_heads // n_kv, n_idx=n_idx, tq=min(tq_dsa, t), tk=min(tk, lp),
                           q0=q0, l_valid=l_valid, topk=min(TOPK_MAX, l_valid // 4))

    tms = _pick(m, 2 * TM)
    q_full = _proj("mla_q_up", qn, pw["w_uq"],
                   functools.partial(_ep_rope, half=r_mla // 2, period=2, roped=(1,),
                                     scale=(HEAD_DIM + r_mla) ** -0.5 * LOG2E),
                   BF16, tm=tms, tn=2 * TN, extras=[tab_extra(tab_m, tms)])
    if past is not None and t * n_heads <= 1024 and n_heads % 8 == 0:
        kc = jnp.concatenate([ckv_all.reshape(b, lp, c_lora), krope_all.reshape(b, lp, LANES)], axis=2)
        q_abs = _absorb_q(q_full, pw["w_uk"], n_heads=n_heads, c_lora=c_lora)
        o_lat = _mla_abs_attention(q_abs.reshape(b, t * n_heads, c_lora + LANES), kc, t=t, n_heads=n_heads,
                                   c_lora=c_lora, tk=min(tk, lp), q0=q0, l_valid=l_valid)
        merged = _absorb_o(o_lat.reshape(m, n_heads * c_lora), pw["w_uv"], gates, out_b.reshape(m, d),
                           n_heads=n_heads, c_lora=c_lora)
    else:
        tml = _pick(b * lp, 2 * TM)
        k_full = _proj("mla_k_up", ckv_all, pw["w_uk"], _ep_kfull, BF16, out_cols=2 * n_heads * HEAD_DIM,
                       tm=tml, tn=2 * TN, extras=[(krope_all, (tml, LANES), lambda i, j: (i, 0))])
        v_mla = _proj("mla_v_up", ckv_all, pw["w_uv"], _ep_plain, BF16, tm=tml, tn=2 * TN)
        merged = _mla_attention(q_full.reshape(b, t, -1), k_full.reshape(b, lp, -1), v_mla.reshape(b, lp, -1),
                                gates.reshape(b, t, 2 * d), out_b,
                                n_heads=n_heads, hb=min(MLA_HEADS_PER_STEP, n_heads), tq=min(tq_mla, t),
                                tk=min(tk, lp), q0=q0, l_valid=l_valid).reshape(m, d)

    def gate_extra(gate, tm_, tn_):
        if per_row:
            return (gate, (tm_, tn_), lambda i, j: (i, j))
        return (gate, (1, tn_), lambda i, j: (0, j))

    tn = _pick(d, 1024)
    x1 = _proj("out_proj", merged, pw["w_out"], _ep_residual, F32,
                     extras=[(xf, (tm, tn), lambda i, j: (i, j)), gate_extra(gt1, tm, tn)])
    h2 = _norm(x1, g_norm2, sc2, sh2, BF16)
    u = _proj("mlp_up", h2, pw["w_up"], _ep_relu2, BF16)
    tm2 = _pick(m, 1024)
    x2 = _proj("mlp_down", u, pw["w_down"], _ep_residual, F32, tm=tm2, tk=2048,
               extras=[(x1, (tm2, tn), lambda i, j: (i, j)), gate_extra(gt2, tm2, tn)])
    rows = (ckv.reshape(b, t, c_lora), krope.reshape(b, t, r_mla), k_new.reshape(b, t, n_kv, HEAD_DIM),
            v_new.reshape(b, t, n_kv, HEAD_DIM), ki_new.reshape(b, t, HEAD_DIM))
    return x2, rows


def kernel(x_prompt, x_sample, c_prompt, c_sample, cache_mla_ckv, cache_mla_krope, cache_dsa_k, cache_dsa_v, cache_idx_k, w_ada, b_ada, g_norm1, w_in, g_q_lora, w_uq, g_kv_lora, w_uk, w_uv, w_out, g_norm2, w_up, w_down, g_final):
    depth = w_in.shape[0]
    bp, tp, d = x_prompt.shape
    bs, ts, _ = x_sample.shape
    n_heads = d // HEAD_DIM
    dims = (d, w_uq.shape[1], cache_mla_ckv.shape[-1], cache_mla_krope.shape[-1], n_heads,
            cache_dsa_k.shape[3], n_heads)
    past_len = cache_mla_ckv.shape[2]

    c_all = jnp.concatenate([c_prompt, c_sample], axis=0)
    n_c = c_all.shape[0]
    c_all = jnp.pad(c_all, ((0, -(-n_c // 16) * 16 - n_c), (0, 0)))

    xp, xs = x_prompt, x_sample
    rows_p, rows_s = [], []
    for l in range(depth):
        ada = _ada(c_all, w_ada[l], b_ada[l])
        pw = _prep_weights(w_in[l], w_uq[l], w_uk[l], w_uv[l], w_out[l], w_up[l], w_down[l], dims)
        norms = (g_norm1[l], g_q_lora[l], g_kv_lora[l], g_norm2[l])
        xp2, rp = _layer(xp, ada[0:bp], None, 0, pw, *norms, dims, tq_mla=TQ_MLA, tq_dsa=TQ_DSA, tk=TK_PROMPT)
        past = (cache_mla_ckv[l], cache_mla_krope[l], cache_dsa_k[l], cache_dsa_v[l], cache_idx_k[l])
        xs2, rs = _layer(xs, ada[bp:bp + bs], past, past_len, pw, *norms, dims, tq_mla=TQ_MLA, tq_dsa=TQ_DSA,
                         tk=TK_SAMPLE)
        xp, xs = xp2.reshape(bp, tp, d), xs2.reshape(bs, ts, d)
        rows_p.append(rp)
        rows_s.append(rs)

    y_prompt = _norm(xp.reshape(bp * tp, d), g_final, None, None, F32).reshape(bp, tp, d)
    y_sample = _norm(xs.reshape(bs * ts, d), g_final, None, None, F32).reshape(bs, ts, d)
    stack = lambda rows, i: jnp.stack([r[i] for r in rows])
    return (y_prompt, y_sample,
            stack(rows_p, 0), stack(rows_p, 1), stack(rows_p, 2), stack(rows_p, 3), stack(rows_p, 4),
            stack(rows_s, 0), stack(rows_s, 1), stack(rows_s, 2), stack(rows_s, 3), stack(rows_s, 4))
```

```python
import functools

import numpy as np
import jax
import jax.numpy as jnp
from jax import lax
from jax.experimental import pallas as pl
from jax.experimental.pallas import tpu as pltpu

F32 = jnp.float32
BF16 = jnp.bfloat16
I32 = jnp.int32
I16 = jnp.int16
HALF_BITS = 16
I16_MIN, I16_MAX = -2 ** (HALF_BITS - 1), 2 ** (HALF_BITS - 1) - 1
BF16_EXACT_INT = 256

LANES = 128
CHUNK = 64
CHUNK_LOG2 = 6
EPS = 1e-6
NEG = -2.0 ** 100
INT_MIN = -2 ** 31
INDEX_BITS = 24
MLA_THETA = 10000.0
ROPE_THETA = 500000.0
HEAD_DIM = 128
DSA_ROT = HEAD_DIM // 4
IDX_ROT = HEAD_DIM // 4
TOPK_MAX = 256
N_ADA = 6
VMEM_LIMIT = 56 * 1024 * 1024
VMEM_LIMIT_DSA = 60 * 1024 * 1024
TQ_MLA, TQ_DSA = 512, 128
MLA_HEADS_PER_STEP = 4
DSA_GROUPS_PER_PASS = 4
TK_PROMPT, TK_SAMPLE = 512, 384


def _cparams(sem):
    return pltpu.CompilerParams(dimension_semantics=sem, vmem_limit_bytes=VMEM_LIMIT)


def _pick(n, pref):
    if n <= pref:
        return n
    t = pref
    while n % t:
        t //= 2
    return t


def _rope_table(pos, theta, rot_dim):
    half = rot_dim // 2
    inv = theta ** (-jnp.arange(half, dtype=F32) / half)
    ang = pos.astype(F32)[:, None] * inv[None, :]
    cos, sin = jnp.cos(ang), jnp.sin(ang)
    n = pos.shape[0]
    ones = jnp.ones((n, LANES - rot_dim), F32)
    zeros = jnp.zeros((n, LANES - rot_dim), F32)
    zh = jnp.zeros((n, half), F32)
    a = jnp.concatenate([cos, cos, ones], axis=1)
    b = jnp.concatenate([-sin, zh, zeros], axis=1)
    c = jnp.concatenate([zh, sin, zeros], axis=1)
    return jnp.concatenate([a, b, c], axis=1)


def _rope_slab(x, tab, half):
    a = tab[:, 0:LANES]
    b = tab[:, LANES:2 * LANES]
    c = tab[:, 2 * LANES:3 * LANES]
    return x * a + pltpu.roll(x, LANES - half, 1) * b + pltpu.roll(x, half, 1) * c


def _mm_body(*refs, nk, n_extra, n_out, epilogue):
    a_ref, w_ref = refs[0], refs[1]
    extra = refs[2:2 + n_extra]
    outs = refs[2 + n_extra:2 + n_extra + n_out]
    a = a_ref[...].astype(BF16)
    w = w_ref[...].astype(BF16)
    part = jnp.dot(a, w, preferred_element_type=F32)
    if nk == 1:
        epilogue(part, extra, outs)
        return
    acc_ref = refs[2 + n_extra + n_out]
    k = pl.program_id(2)

    @pl.when(k == 0)
    def _():
        acc_ref[...] = part

    @pl.when(k > 0)
    def _():
        acc_ref[...] += part

    @pl.when(k == nk - 1)
    def _():
        epilogue(acc_ref[...], extra, outs)


def _matmul(a, w, *, n, col0, tm, tn, tk, extras, outs, epilogue, name, order="nm"):
    m, kdim = a.shape
    assert col0 % tn == 0
    jb = col0 // tn
    gi, gj, gk = m // tm, n // tn, kdim // tk
    if order == "nm":
        grid = (gj, gi, gk)
        wrap = lambda f: (lambda pj, pi, pk: f(pi, pj, pk))
    else:
        grid = (gi, gj, gk)
        wrap = lambda f: (lambda pi, pj, pk: f(pi, pj, pk))
    in_specs = [
        pl.BlockSpec((tm, tk), wrap(lambda i, j, k: (i, k))),
        pl.BlockSpec((tk, tn), wrap(lambda i, j, k: (k, j + jb))),
    ]
    args = [a, w]
    for arr, blk, fn in extras:
        in_specs.append(pl.BlockSpec(blk, wrap(lambda i, j, k, fn=fn: fn(i, j))))
        args.append(arr)
    out_specs = [pl.BlockSpec(blk, wrap(lambda i, j, k, fn=fn: fn(i, j))) for _, blk, fn in outs]
    out_shape = [s for s, _, _ in outs]
    scratch = [pltpu.VMEM((tm, tn), F32)] if gk > 1 else []
    body = functools.partial(_mm_body, nk=gk, n_extra=len(extras), n_out=len(outs), epilogue=epilogue)
    res = pl.pallas_call(
        body, grid=grid, in_specs=in_specs, out_specs=out_specs, out_shape=out_shape,
        scratch_shapes=scratch, name=name,
        compiler_params=_cparams(("parallel", "parallel", "arbitrary")),
    )(*args)
    return res


def _ep_plain(acc, extra, outs):
    outs[0][...] = acc.astype(outs[0].dtype)


def _ep_rmsnorm(acc, extra, outs):
    g = extra[0][...]
    y = acc * lax.rsqrt(jnp.mean(acc * acc, axis=-1, keepdims=True) + EPS) * g
    outs[0][...] = y.astype(outs[0].dtype)


def _ep_sigmoid(acc, extra, outs):
    outs[0][...] = jax.nn.sigmoid(acc).astype(outs[0].dtype)


def _ep_relu2(acc, extra, outs):
    r = jnp.maximum(acc, 0.0)
    outs[0][...] = (r * r).astype(outs[0].dtype)


def _ep_residual(acc, extra, outs):
    x = extra[0][...]
    gate = extra[1][...]
    outs[0][...] = x + gate * acc


def _ep_rope(acc, extra, outs, *, half, period, roped, scale):
    tab = extra[0][...]
    o = outs[0]
    for s in range(acc.shape[1] // LANES):
        x = acc[:, s * LANES:(s + 1) * LANES]
        if (s % period) in roped:
            x = _rope_slab(x, tab, half)
        if scale != 1.0:
            x = x * scale
        o[:, s * LANES:(s + 1) * LANES] = x.astype(o.dtype)


def _ep_kfull(acc, extra, outs):
    kr = extra[0][...]
    o = outs[0]
    for h in range(acc.shape[1] // LANES):
        o[:, (2 * h) * LANES:(2 * h + 1) * LANES] = acc[:, h * LANES:(h + 1) * LANES].astype(o.dtype)
        o[:, (2 * h + 1) * LANES:(2 * h + 2) * LANES] = kr


TM, TN, TK = 512, 1024, 4096


def _proj(name, a, w, ep, out_dtype, *, cols=None, tm=TM, tn=TN, tk=TK, extras=(), out_cols=None, order="nm"):
    m, k = a.shape
    col0, n = (0, w.shape[1]) if cols is None else cols
    tm, tn, tk = _pick(m, tm), _pick(n, tn), _pick(k, tk)
    oc = n if out_cols is None else out_cols
    outs = [(jax.ShapeDtypeStruct((m, oc), out_dtype), (tm, tn * oc // n), lambda i, j: (i, j))]
    return _matmul(a, w, n=n, col0=col0, tm=tm, tn=tn, tk=tk, extras=list(extras), outs=outs, epilogue=ep,
                   name=name, order=order)[0]


def _ada_body(c_ref, w_ref, b_ref, o_ref):
    c = c_ref[...]
    a = (c * jax.nn.sigmoid(c)).astype(BF16)
    o_ref[...] = jnp.dot(a, w_ref[...].astype(BF16), preferred_element_type=F32) + b_ref[...]


def _ada(c_all, w_ada, b_ada):
    r, d = c_all.shape
    n = w_ada.shape[1]
    tn = _pick(n, 512)
    return pl.pallas_call(
        _ada_body, grid=(n // tn,),
        in_specs=[pl.BlockSpec((r, d), lambda j: (0, 0)),
                  pl.BlockSpec((d, tn), lambda j: (0, j)),
                  pl.BlockSpec((1, tn), lambda j: (0, j))],
        out_specs=pl.BlockSpec((r, tn), lambda j: (0, j)),
        out_shape=jax.ShapeDtypeStruct((r, n), F32), name="adaln",
        compiler_params=_cparams(("parallel",)),
    )(c_all, w_ada, b_ada.reshape(1, n))


def _norm_body(*refs, modulate):
    x = refs[0][...]
    g = refs[1][...]
    y = x * lax.rsqrt(jnp.mean(x * x, axis=-1, keepdims=True) + EPS) * g
    if modulate:
        y = y * (1.0 + refs[2][...]) + refs[3][...]
        o = refs[4]
    else:
        o = refs[2]
    o[...] = y.astype(o.dtype)


def _norm(x, g, sc, sh, out_dtype):
    m, d = x.shape
    tm = _pick(m, TM)
    in_specs = [pl.BlockSpec((tm, d), lambda i: (i, 0)), pl.BlockSpec((1, d), lambda i: (0, 0))]
    args = [x, g.reshape(1, d)]
    if sc is not None:
        per_row = sc.shape[0] == m
        blk = (tm, d) if per_row else (1, d)
        fn = (lambda i: (i, 0)) if per_row else (lambda i: (0, 0))
        in_specs += [pl.BlockSpec(blk, fn), pl.BlockSpec(blk, fn)]
        args += [sc, sh]
    return pl.pallas_call(
        functools.partial(_norm_body, modulate=sc is not None), grid=(m // tm,),
        in_specs=in_specs, out_specs=pl.BlockSpec((tm, d), lambda i: (i, 0)),
        out_shape=jax.ShapeDtypeStruct((m, d), out_dtype), name="rmsnorm",
        compiler_params=_cparams(("parallel",)),
    )(*args)


def _finish_body(z_ref, g_ref, tm_ref, td_ref,
                 ckv_o, ckvb_o, kr_o, krb_o, k_o, kb_o, v_o, vb_o, ki_o, kib_o, wi_o,
                 *, c_lora, r_mla, n_kv, idx_scale):
    tab_m = tm_ref[...]
    tab_d = td_ref[...]
    off = 0
    ckv = z_ref[:, off:off + c_lora]
    ckv = ckv * lax.rsqrt(jnp.mean(ckv * ckv, axis=-1, keepdims=True) + EPS) * g_ref[...]
    ckv_o[...] = ckv
    ckvb_o[...] = ckv.astype(BF16)
    off += c_lora
    kr = _rope_slab(z_ref[:, off:off + LANES], tab_m, r_mla // 2)
    kr_o[...] = kr[:, 0:r_mla]
    krb_o[...] = kr.astype(BF16)
    off += LANES
    for h in range(n_kv):
        kh = _rope_slab(z_ref[:, off + h * LANES:off + (h + 1) * LANES], tab_d, DSA_ROT // 2)
        k_o[:, h * LANES:(h + 1) * LANES] = kh
        kb_o[:, h * LANES:(h + 1) * LANES] = kh.astype(BF16)
    off += n_kv * LANES
    v = z_ref[:, off:off + n_kv * LANES]
    v_o[...] = v
    vb_o[...] = v.astype(BF16)
    off += n_kv * LANES
    ki = _rope_slab(z_ref[:, off:off + LANES], tab_d, IDX_ROT // 2)
    ki_o[...] = ki
    kib_o[...] = ki.astype(BF16)
    off += LANES
    wi_o[...] = z_ref[:, off:off + LANES] * idx_scale


def _finish(z, g_kv, tab_m, tab_d, *, c_lora, r_mla, n_kv, idx_scale):
    m, zc = z.shape
    tm = _pick(m, TM)
    kvw = n_kv * LANES
    row = lambda i: (i, 0)
    shapes = [(c_lora, F32), (c_lora, BF16), (r_mla, F32), (LANES, BF16), (kvw, F32), (kvw, BF16),
              (kvw, F32), (kvw, BF16), (LANES, F32), (LANES, BF16), (LANES, F32)]
    return pl.pallas_call(
        functools.partial(_finish_body, c_lora=c_lora, r_mla=r_mla, n_kv=n_kv, idx_scale=idx_scale),
        grid=(m // tm,),
        in_specs=[pl.BlockSpec((tm, zc), row), pl.BlockSpec((1, c_lora), lambda i: (0, 0)),
                  pl.BlockSpec((tm, 3 * LANES), row), pl.BlockSpec((tm, 3 * LANES), row)],
        out_specs=[pl.BlockSpec((tm, w), row) for w, _ in shapes],
        out_shape=[jax.ShapeDtypeStruct((m, w), dt) for w, dt in shapes], name="finish_small",
        compiler_params=_cparams(("parallel",)),
    )(z, g_kv.reshape(1, c_lora), tab_m, tab_d)


def _row_limits(q_first, tq, l_valid):
    rows = q_first + lax.broadcasted_iota(I32, (tq, 1), 0)
    chunk_end = lambda p: (lax.shift_right_logical(p, CHUNK_LOG2) + 1) * CHUNK
    lim = jnp.minimum(chunk_end(rows), l_valid)
    lim_min = jnp.minimum(chunk_end(q_first), l_valid)
    lim_max = jnp.minimum(chunk_end(q_first + tq - 1), l_valid)
    return lim, lim_min, lim_max


LOG2E = 1.4426950408889634


def _lane_tile(x, width):
    return x if width == LANES else jnp.concatenate([x] * (width // LANES), axis=1)


def _softmax_update(s, v, m_prev, l_prev, acc_prev):
    m_new = jnp.maximum(m_prev, jnp.max(s, axis=1, keepdims=True))
    p = jnp.exp2(s - _lane_tile(m_new, s.shape[1]))
    alpha = jnp.exp2(m_prev - m_new)
    l_new = alpha * l_prev + jnp.sum(p, axis=1, keepdims=True)
    acc_new = _lane_tile(alpha, acc_prev.shape[1]) * acc_prev + jnp.dot(p.astype(BF16), v,
                                                                        preferred_element_type=F32)
    return m_new, l_new, acc_new


def _softmax_step(s, v, m_sc, acc_sc):
    sb = s.astype(BF16)
    m_prev = m_sc[...]
    m_new = jnp.maximum(m_prev, jnp.max(sb, axis=1, keepdims=True).astype(F32))
    p = jnp.exp2(sb - _lane_tile(m_new.astype(BF16), s.shape[1]))
    alpha = jnp.exp2(m_prev - m_new)
    v_ones = jnp.concatenate([v, jnp.ones(v.shape, BF16)], axis=1)
    acc_sc[...] = _lane_tile(alpha, 2 * LANES) * acc_sc[...] + jnp.dot(p, v_ones, preferred_element_type=F32)
    m_sc[...] = m_new


def _softmax_result(acc_sc):
    acc = acc_sc[...]
    return acc[:, 0:LANES] * (1.0 / acc[:, LANES:2 * LANES])


_NT = (((1,), (1,)), ((), ()))


def _gate_mix(out_a, ga, gb, out_b):
    return (ga.astype(F32) * out_a + gb.astype(F32) * out_b.astype(F32)).astype(BF16)


def _mla_body(q_ref, k_ref, v_ref, ga_ref, gb_ref, ob_ref, o_ref, m_sc, acc_sc, s_sc,
              *, hb, tq, tk, q0, l_valid):
    q_first = q0 + pl.program_id(2) * tq
    lim, lim_min, lim_max = _row_limits(q_first, tq, l_valid)
    n_full = lim_min // tk
    n_all = (lim_max + tk - 1) // tk
    col = lax.broadcasted_iota(I32, (1, tk), 1)
    m_sc[...] = jnp.full(m_sc.shape, NEG, F32)
    acc_sc[...] = jnp.zeros(acc_sc.shape, F32)

    def scores(c, j):
        start = pl.multiple_of(c * tk, tk)
        return lax.dot_general(q_ref[:, j * 2 * LANES:(j + 1) * 2 * LANES],
                               k_ref[pl.ds(start, tk), j * 2 * LANES:(j + 1) * 2 * LANES], _NT,
                               preferred_element_type=F32)

    for j in range(hb):
        s_sc[j] = scores(0, j)

    def step(c, masked, last):
        start = pl.multiple_of(c * tk, tk)
        for j in range(hb):
            s = s_sc[j]
            if masked:
                s = jnp.where(col + start < lim, s, NEG)
            _softmax_step(s, v_ref[pl.ds(start, tk), j * LANES:(j + 1) * LANES], m_sc.at[j], acc_sc.at[j])
            if not last:
                s_sc[j] = scores(c + 1, j)

    def full_step(c, carry):
        step(c, False, False)
        return carry

    def masked_step(c, carry):
        step(c, True, False)
        return carry

    n_rot = n_all - 1
    n_rot_full = jnp.minimum(n_full, n_rot)
    lax.fori_loop(0, n_rot_full, full_step, 0)
    lax.fori_loop(n_rot_full, n_rot, masked_step, 0)
    step(n_rot, True, True)
    for j in range(hb):
        cols = slice(j * LANES, (j + 1) * LANES)
        o_ref[:, cols] = _gate_mix(_softmax_result(acc_sc.at[j]), ga_ref[:, cols], gb_ref[:, cols], ob_ref[:, cols])


def _mla_attention(q_full, k_full, v, gates, out_b, *, n_heads, hb, tq, tk, q0, l_valid):
    b, t, _ = q_full.shape
    lp = k_full.shape[1]
    kv_mode = dict(pipeline_mode=pl.Buffered(1)) if t // tq > 2 else {}
    stat = pltpu.VMEM((hb, tq, LANES), F32)
    return pl.pallas_call(
        functools.partial(_mla_body, hb=hb, tq=tq, tk=tk, q0=q0, l_valid=l_valid),
        grid=(b, n_heads // hb, t // tq),
        in_specs=[pl.BlockSpec((None, tq, hb * 2 * LANES), lambda bi, h, qi: (bi, qi, h)),
                  pl.BlockSpec((None, lp, hb * 2 * LANES), lambda bi, h, qi: (bi, 0, h), **kv_mode),
                  pl.BlockSpec((None, lp, hb * LANES), lambda bi, h, qi: (bi, 0, h)),
                  pl.BlockSpec((None, tq, hb * LANES), lambda bi, h, qi: (bi, qi, h)),
                  pl.BlockSpec((None, tq, hb * LANES), lambda bi, h, qi: (bi, qi, n_heads // hb + h)),
                  pl.BlockSpec((None, tq, hb * LANES), lambda bi, h, qi: (bi, qi, h))],
        out_specs=pl.BlockSpec((None, tq, hb * LANES), lambda bi, h, qi: (bi, qi, h)),
        out_shape=jax.ShapeDtypeStruct((b, t, n_heads * LANES), BF16),
        scratch_shapes=[stat, pltpu.VMEM((hb, tq, 2 * LANES), F32), pltpu.VMEM((hb, tq, tk), F32)],
        name="mla_attention",
        compiler_params=_cparams(("parallel", "parallel", "arbitrary")),
    )(q_full, k_full, v, gates, gates, out_b)


def _absorb_q_body(q_ref, w_ref, o_ref, *, c_lora):
    lat = lax.dot_general(q_ref[:, 0:LANES], w_ref[...], _NT, preferred_element_type=F32)
    o_ref[:, 0:c_lora] = lat.astype(o_ref.dtype)
    o_ref[:, c_lora:c_lora + LANES] = q_ref[:, LANES:2 * LANES]


def _absorb_q(q_full, w_uk, *, n_heads, c_lora):
    m = q_full.shape[0]
    return pl.pallas_call(
        functools.partial(_absorb_q_body, c_lora=c_lora), grid=(n_heads,),
        in_specs=[pl.BlockSpec((m, 2 * LANES), lambda h: (0, h)), pl.BlockSpec((c_lora, LANES), lambda h: (0, h))],
        out_specs=pl.BlockSpec((m, c_lora + LANES), lambda h: (0, h)),
        out_shape=jax.ShapeDtypeStruct((m, n_heads * (c_lora + LANES)), BF16), name="mla_absorb_q",
        compiler_params=_cparams(("parallel",)),
    )(q_full, w_uk)


def _absorb_o_body(o_lat_ref, w_ref, ga_ref, gb_ref, ob_ref, o_ref):
    out_a = jnp.dot(o_lat_ref[...], w_ref[...], preferred_element_type=F32)
    o_ref[...] = _gate_mix(out_a, ga_ref[...], gb_ref[...], ob_ref[...])


def _absorb_o(o_lat, w_uv, gates, out_b, *, n_heads, c_lora):
    m = o_lat.shape[0]
    head = lambda h: (0, h)
    return pl.pallas_call(
        _absorb_o_body, grid=(n_heads,),
        in_specs=[pl.BlockSpec((m, c_lora), head), pl.BlockSpec((c_lora, LANES), head),
                  pl.BlockSpec((m, LANES), head), pl.BlockSpec((m, LANES), lambda h: (0, n_heads + h)),
                  pl.BlockSpec((m, LANES), head)],
        out_specs=pl.BlockSpec((m, LANES), head),
        out_shape=jax.ShapeDtypeStruct((m, n_heads * LANES), BF16), name="mla_absorb_o",
        compiler_params=_cparams(("parallel",)),
    )(o_lat, w_uv, gates, gates, out_b)


def _mla_abs_body(q_ref, kc_ref, o_ref, *, bb, t, n_heads, c_lora, tk, q0, l_valid):
    rows = t * n_heads
    tok = lax.broadcasted_iota(I32, (t, n_heads, 1), 0).reshape(rows, 1)
    lim = jnp.minimum((lax.shift_right_logical(q0 + tok, CHUNK_LOG2) + 1) * CHUNK, l_valid)
    lim_min = min((q0 // CHUNK + 1) * CHUNK, l_valid)
    lim_max = min(((q0 + t - 1) // CHUNK + 1) * CHUNK, l_valid)
    col = lax.broadcasted_iota(I32, (1, tk), 1)
    for j in range(bb):
        q = q_ref[j]
        m = jnp.full((rows, LANES), NEG, F32)
        l = jnp.zeros((rows, LANES), F32)
        acc = jnp.zeros((rows, c_lora), F32)
        for c in range(-(-lim_max // tk)):
            kc = kc_ref[j, c * tk:(c + 1) * tk, :]
            s = lax.dot_general(q, kc, _NT, preferred_element_type=F32)
            if (c + 1) * tk > lim_min:
                s = jnp.where(col + c * tk < lim, s, NEG)
            m, l, acc = _softmax_update(s, kc[:, 0:c_lora], m, l, acc)
        o_ref[j] = (acc * _lane_tile(1.0 / l, c_lora)).astype(o_ref.dtype)


def _mla_abs_attention(q_abs, kc, *, t, n_heads, c_lora, tk, q0, l_valid):
    b, rows, dk = q_abs.shape
    lp = kc.shape[1]
    bb = _pick(b, 4)
    return pl.pallas_call(
        functools.partial(_mla_abs_body, bb=bb, t=t, n_heads=n_heads, c_lora=c_lora, tk=tk, q0=q0, l_valid=l_valid),
        grid=(b // bb,),
        in_specs=[pl.BlockSpec((bb, rows, dk), lambda i: (i, 0, 0)), pl.BlockSpec((bb, lp, dk), lambda i: (i, 0, 0))],
        out_specs=pl.BlockSpec((bb, rows, c_lora), lambda i: (i, 0, 0)),
        out_shape=jax.ShapeDtypeStruct((b, rows, c_lora), BF16), name="mla_absorbed_attention",
        compiler_params=_cparams(("parallel",)),
    )(q_abs, kc)


def _dsa_body(qb_ref, qi_ref, wi_ref, ki_ref, kt_ref, v_ref, o_ref,
              keys_sc, half_sc, m_sc, acc_sc, s_sc, *, tq, tk, q0, l_valid, n_idx, n_kv, rep, topk):
    q_first = q0 + pl.program_id(1) * tq
    lim, _, lim_max = _row_limits(q_first, tq, l_valid)
    n_all = (lim_max + tk - 1) // tk
    col = lax.broadcasted_iota(I32, (1, tk), 1)
    w = wi_ref[...]

    def score_chunk(c, carry):
        start = pl.multiple_of(c * tk, tk)
        kic = ki_ref[pl.ds(start, tk), :]
        s = jnp.zeros((tq, tk), F32)
        for h in range(n_idx):
            y = lax.dot_general(qi_ref[:, h * LANES:(h + 1) * LANES], kic, _NT,
                                preferred_element_type=F32)
            s = s + w[:, h:h + 1] * jnp.maximum(y, 0.0)
        s = jnp.where(s == 0.0, 0.0, s)
        bits = lax.bitcast_convert_type(s, I32)
        key = jnp.where(bits >= 0, bits, bits ^ jnp.int32(0x7FFFFFFF))
        key = jnp.where(col + start < lim, key, jnp.int32(INT_MIN))
        keys_sc[c] = key
        half_sc[c] = lax.shift_right_arithmetic(key, HALF_BITS).astype(I16)
        return carry

    lax.fori_loop(0, n_all, score_chunk, 0)

    def count_rows(indicator):
        def count_chunk(c, acc):
            ind = indicator(c, keys_sc[c])
            part = ind[:, 0:LANES]
            for j in range(1, tk // LANES):
                part = part + ind[:, j * LANES:(j + 1) * LANES]
            return acc + part

        acc = lax.fori_loop(0, n_all, count_chunk, jnp.zeros((tq, LANES), F32))
        return jnp.sum(acc, axis=1, keepdims=True)

    def count_ge(cand):
        return count_rows(lambda c, kk: jnp.where(kk >= cand, 1.0, 0.0))

    def count_half_ge(cand):
        c16 = _lane_tile(jnp.broadcast_to(cand, (tq, LANES)).astype(I16), tk)

        def count_chunk(c, acc):
            ind = jnp.where(half_sc[c] >= c16, jnp.ones((), BF16), jnp.zeros((), BF16))
            part = ind[:, 0:LANES]
            for j in range(1, tk // LANES):
                part = part + ind[:, j * LANES:(j + 1) * LANES]
            return acc + part

        acc = lax.fori_loop(0, n_all, count_chunk, jnp.zeros((tq, LANES), BF16))
        return jnp.sum(acc.astype(F32), axis=1, keepdims=True)

    def search_half(goal):
        def bit_pass(i, carry):
            t, n_at_t = carry
            cand = t + lax.shift_left(jnp.int32(1), jnp.int32(HALF_BITS - 1) - i)
            n = count_half_ge(cand)
            keep = n >= goal
            return jnp.where(keep, cand, t), jnp.where(keep, n, n_at_t)

        return lax.fori_loop(0, HALF_BITS, bit_pass,
                             (jnp.full((tq, 1), I16_MIN, I32), jnp.full((tq, 1), -1.0, F32)))

    assert keys_sc.shape[0] * (tk // LANES) <= BF16_EXACT_INT, "per-lane partial counts must stay exact in bf16"
    t_hi, n_hi = search_half(float(topk))
    n_above = jnp.where(t_hi < I16_MAX, count_half_ge(jnp.minimum(t_hi + 1, I16_MAX)), 0.0)

    def low_halves(c, carry):
        low = jnp.bitwise_and(keys_sc[c], 2 ** HALF_BITS - 1) + I16_MIN
        half_sc[c] = jnp.where(half_sc[c] == _lane_tile(jnp.broadcast_to(t_hi, (tq, LANES)).astype(I16), tk),
                               low.astype(I16), jnp.full((), I16_MIN, I16))
        return carry

    lax.fori_loop(0, n_all, low_halves, 0)
    t_lo, n_lo = search_half(topk - n_above)
    t = t_hi * 2 ** HALF_BITS + (t_lo - I16_MIN)
    n_at_t = jnp.where(n_lo >= 0, n_above + n_lo, jnp.maximum(n_hi, 0.0))
    thr = jnp.maximum(t, jnp.int32(INT_MIN + 1))

    @pl.when(jnp.max(n_at_t) > topk)
    def _():
        slots = topk - count_ge(thr + 1)

        def tied_before(bound):
            return count_rows(lambda c, kk: jnp.where(kk == thr, jnp.where(col + c * tk < bound, 1.0, 0.0), 0.0))

        def index_pass(i, j_last):
            cand = j_last + lax.shift_left(jnp.int32(1), jnp.int32(INDEX_BITS - 1) - i)
            return jnp.where(tied_before(cand) < slots, cand, j_last)

        j_last = lax.fori_loop(0, INDEX_BITS, index_pass, jnp.zeros((tq, 1), I32))

        def demote(c, carry):
            kk = keys_sc[c]
            keys_sc[c] = jnp.where(kk == thr, jnp.where(col + c * tk > j_last, thr - 1, kk), kk)
            return carry

        lax.fori_loop(0, n_all, demote, 0)

    rows = rep * tq
    tok = jnp.bitwise_and(lax.broadcasted_iota(I32, (rows, LANES), 0), tq - 1)
    eye = jnp.where(tok == lax.broadcasted_iota(I32, (rows, LANES), 1), 1.0, 0.0).astype(BF16)

    gb = m_sc.shape[0]

    def mask_bias(c):
        bias = jnp.where(keys_sc[c] >= thr, 0.0, NEG).astype(BF16)
        if tq < LANES:
            bias = jnp.concatenate([bias, jnp.zeros((LANES - tq, tk), BF16)], axis=0)
        return bias

    def scores(c, g, bias):
        qs = jnp.concatenate([qb_ref[:, (g * rep + r) * LANES:(g * rep + r + 1) * LANES]
                              for r in range(rep)], axis=0)
        return jnp.dot(jnp.concatenate([qs, eye], axis=1), jnp.concatenate([kt_ref[g, c], bias], axis=0),
                       preferred_element_type=F32)

    def attend(c, g0, nxt_c, nxt_g0):
        start = pl.multiple_of(c * tk, tk)
        bias = None if nxt_g0 is None else mask_bias(nxt_c)
        for j in range(gb):
            g = g0 + j
            _softmax_step(s_sc[j], v_ref[pl.ds(start, tk), g * LANES:(g + 1) * LANES], m_sc.at[j], acc_sc.at[j])
            if nxt_g0 is not None:
                s_sc[j] = scores(nxt_c, nxt_g0 + j, bias)

    bias0 = mask_bias(0)
    for j in range(gb):
        s_sc[j] = scores(0, j, bias0)
    for g0 in range(0, n_kv, gb):
        m_sc[...] = jnp.full(m_sc.shape, NEG, F32)
        acc_sc[...] = jnp.zeros(acc_sc.shape, F32)

        def attend_chunk(c, carry, g0=g0):
            attend(c, g0, c + 1, g0)
            return carry

        lax.fori_loop(0, n_all - 1, attend_chunk, 0)
        attend(n_all - 1, g0, 0, g0 + gb if g0 + gb < n_kv else None)
        for j in range(gb):
            o = _softmax_result(acc_sc.at[j])
            for r in range(rep):
                h = (g0 + j) * rep + r
                o_ref[:, h * LANES:(h + 1) * LANES] = o[r * tq:(r + 1) * tq].astype(o_ref.dtype)


def _dsa_attention(qb, qi, wi, ki, k, v, *, n_kv, rep, n_idx, tq, tk, q0, l_valid, topk):
    b, t, _ = qb.shape
    lp = k.shape[1]
    kvw = n_kv * LANES
    assert tq <= LANES and tq & (tq - 1) == 0, "the token one-hot needs a power-of-two block of at most 128 rows"
    kt = k.reshape(b, lp // tk, tk, n_kv, LANES).transpose(0, 3, 1, 4, 2)
    kv_mode = dict(pipeline_mode=pl.Buffered(1)) if t // tq > 2 else {}
    stat = pltpu.VMEM((min(DSA_GROUPS_PER_PASS, n_kv), rep * tq, LANES), F32)
    return pl.pallas_call(
        functools.partial(_dsa_body, tq=tq, tk=tk, q0=q0, l_valid=l_valid, n_idx=n_idx, n_kv=n_kv, rep=rep,
                          topk=topk),
        grid=(b, t // tq),
        in_specs=[pl.BlockSpec((None, tq, n_kv * rep * LANES), lambda bi, qb_: (bi, qb_, 0)),
                  pl.BlockSpec((None, tq, n_idx * LANES), lambda bi, qb_: (bi, qb_, 0)),
                  pl.BlockSpec((None, tq, LANES), lambda bi, qb_: (bi, qb_, 0)),
                  pl.BlockSpec((None, lp, LANES), lambda bi, qb_: (bi, 0, 0), **kv_mode),
                  pl.BlockSpec((None, n_kv, lp // tk, LANES, tk), lambda bi, qb_: (bi, 0, 0, 0, 0), **kv_mode),
                  pl.BlockSpec((None, lp, kvw), lambda bi, qb_: (bi, 0, 0), **kv_mode)],
        out_specs=pl.BlockSpec((None, tq, n_kv * rep * LANES), lambda bi, qb_: (bi, qb_, 0)),
        out_shape=jax.ShapeDtypeStruct((b, t, n_kv * rep * LANES), BF16),
        scratch_shapes=[pltpu.VMEM((lp // tk, tq, tk), I32), pltpu.VMEM((lp // tk, tq, tk), I16), stat,
                        pltpu.VMEM((stat.shape[0], rep * tq, 2 * LANES), F32),
                        pltpu.VMEM((stat.shape[0], rep * tq, tk), F32)],
        name="dsa_attention",
        compiler_params=pltpu.CompilerParams(dimension_semantics=("parallel", "arbitrary"),
                                             vmem_limit_bytes=VMEM_LIMIT_DSA),
    )(qb, qi, wi, ki, kt, v)


def _pad_cols(w, width):
    return jnp.pad(w, ((0, 0), (0, width - w.shape[1])))


def _prep_weights(w_in, w_uq, w_uk, w_uv, w_out, w_up, w_down, dims):
    d, q_lora, c_lora, r_mla, n_heads, n_kv, n_idx = dims
    splits = (q_lora, c_lora, r_mla, n_heads * HEAD_DIM, n_kv * HEAD_DIM, n_kv * HEAD_DIM,
              n_idx * HEAD_DIM, HEAD_DIM, n_idx, d, d)
    offs = np.concatenate([[0], np.cumsum(splits)])
    sec = [w_in[:, int(offs[i]):int(offs[i + 1])] for i in range(len(splits))]
    small = jnp.concatenate([sec[1], _pad_cols(sec[2], LANES), sec[4], sec[5], sec[7],
                             _pad_cols(sec[8], LANES)], axis=1)
    small = _pad_cols(small, -(-small.shape[1] // 512) * 512)
    parts = [("q", sec[0]), ("small", small), ("qb", sec[3]), ("qi", sec[6]),
             ("gates", jnp.concatenate([sec[9], sec[10]], axis=1))]
    windows, off = {}, 0
    for i, (key, p) in enumerate(parts):
        windows[key] = (off, p.shape[1])
        parts[i] = (key, _pad_cols(p, -(-p.shape[1] // TN) * TN))
        off += parts[i][1].shape[1]
    wq = w_uq.reshape(q_lora, n_heads, HEAD_DIM + r_mla)
    wq = jnp.pad(wq, ((0, 0), (0, 0), (0, 2 * LANES - HEAD_DIM - r_mla)))
    return dict(
        w_in=jnp.concatenate([p for _, p in parts], axis=1).astype(BF16), win=windows,
        w_uq=wq.reshape(q_lora, n_heads * 2 * LANES).astype(BF16),
        w_uk=w_uk.reshape(c_lora, n_heads * HEAD_DIM).astype(BF16),
        w_uv=w_uv.reshape(c_lora, n_heads * HEAD_DIM).astype(BF16),
        w_out=w_out.astype(BF16), w_up=w_up.astype(BF16), w_down=w_down.astype(BF16))


def _layer(x, ada, past, q0, pw, g_norm1, g_q_lora, g_kv_lora, g_norm2, dims, tq_mla, tq_dsa, tk):
    d, q_lora, c_lora, r_mla, n_heads, n_kv, n_idx = dims
    b, t, _ = x.shape
    m = b * t
    xf = x.reshape(m, d)
    if ada.shape[0] == 1:
        mods = [ada[:, i * d:(i + 1) * d] for i in range(N_ADA)]
    else:
        mods = [jnp.repeat(ada[:, i * d:(i + 1) * d], t, axis=0) for i in range(N_ADA)]
    sh1, sc1, gt1, sh2, sc2, gt2 = mods
    per_row = ada.shape[0] != 1

    pos = q0 + jnp.arange(t, dtype=I32)
    tab_m = jnp.tile(_rope_table(pos, MLA_THETA, r_mla), (b, 1))
    tab_d = jnp.tile(_rope_table(pos, ROPE_THETA, DSA_ROT), (b, 1))

    h = _norm(xf, g_norm1, sc1, sh1, BF16)

    def tab_extra(tab, tm):
        return (tab, (tm, 3 * LANES), lambda i, j: (i, 0))

    tm = _pick(m, 512)
    qn = _proj("proj_qlat", h, pw["w_in"], _ep_rmsnorm, BF16, cols=pw["win"]["q"], tn=q_lora,
                     extras=[(g_q_lora.reshape(1, q_lora), (1, q_lora), lambda i, j: (0, 0))])
    z_small = _proj("proj_small", h, pw["w_in"], _ep_plain, F32, cols=pw["win"]["small"],
                    tn=TN if pw["win"]["small"][1] % TN == 0 else 512)
    q_dsa = _proj("proj_qdsa", h, pw["w_in"],
                  functools.partial(_ep_rope, half=DSA_ROT // 2, period=1, roped=(0,),
                                    scale=HEAD_DIM ** -0.5 * LOG2E),
                  BF16, cols=pw["win"]["qb"], extras=[tab_extra(tab_d, tm)])
    q_idx = _proj("proj_qidx", h, pw["w_in"],
                  functools.partial(_ep_rope, half=IDX_ROT // 2, period=1, roped=(0,), scale=1.0),
                  BF16, cols=pw["win"]["qi"], extras=[tab_extra(tab_d, tm)])
    gates = _proj("proj_gates", h, pw["w_in"], _ep_sigmoid, BF16, cols=pw["win"]["gates"])

    (ckv, ckv_b, krope, krope_b, k_new, k_new_b, v_new, v_new_b, ki_new, ki_new_b, wi) = _finish(
        z_small, g_kv_lora, tab_m, tab_d, c_lora=c_lora, r_mla=r_mla, n_kv=n_kv, idx_scale=n_idx ** -0.5)

    kvw = n_kv * HEAD_DIM
    if past is None:
        l_valid = t
        lp = t
        ckv_all, krope_all, k_all, v_all, ki_all = ckv_b, krope_b, k_new_b, v_new_b, ki_new_b
    else:
        p_ckv, p_krope, p_k, p_v, p_ki = past
        l_valid = p_ckv.shape[1] + t
        lp = -(-l_valid // tk) * tk

        def cat(p, new, width):
            p = p.reshape(b, p.shape[1], -1).astype(BF16)
            if p.shape[2] < width:
                p = jnp.pad(p, ((0, 0), (0, 0), (0, width - p.shape[2])))
            full = jnp.concatenate([p, new.reshape(b, t, width)], axis=1)
            return jnp.pad(full, ((0, 0), (0, lp - l_valid), (0, 0))).reshape(b * lp, width)

        ckv_all = cat(p_ckv, ckv_b, c_lora)
        krope_all = cat(p_krope, krope_b, LANES)
        k_all = cat(p_k, k_new_b, kvw)
        v_all = cat(p_v, v_new_b, kvw)
        ki_all = cat(p_ki, ki_new_b, LANES)

    out_b = _dsa_attention(q_dsa.reshape(b, t, -1), q_idx.reshape(b, t, -1), wi.reshape(b, t, LANES),
                           ki_all.reshape(b, lp, LANES), k_all.reshape(b, lp, kvw), v_all.reshape(b, lp, kvw),
                           n_kv=n_kv, rep=n_heads // n_kv, n_idx=n_idx, tq=min(tq_dsa, t), tk=min(tk, lp),
                           q0=q0, l_valid=l_valid, topk=min(TOPK_MAX, l_valid // 4))

    tms = _pick(m, 2 * TM)
    q_full = _proj("mla_q_up", qn, pw["w_uq"],
                   functools.partial(_ep_rope, half=r_mla // 2, period=2, roped=(1,),
                                     scale=(HEAD_DIM + r_mla) ** -0.5 * LOG2E),
                   BF16, tm=tms, tn=2 * TN, extras=[tab_extra(tab_m, tms)])
    if past is not None and t * n_heads <= 1024 and n_heads % 8 == 0:
        kc = jnp.concatenate([ckv_all.reshape(b, lp, c_lora), krope_all.reshape(b, lp, LANES)], axis=2)
        q_abs = _absorb_q(q_full, pw["w_uk"], n_heads=n_heads, c_lora=c_lora)
        o_lat = _mla_abs_attention(q_abs.reshape(b, t * n_heads, c_lora + LANES), kc, t=t, n_heads=n_heads,
                                   c_lora=c_lora, tk=min(tk, lp), q0=q0, l_valid=l_valid)
        merged = _absorb_o(o_lat.reshape(m, n_heads * c_lora), pw["w_uv"], gates, out_b.reshape(m, d),
                           n_heads=n_heads, c_lora=c_lora)
    else:
        tml = _pick(b * lp, 2 * TM)
        k_full = _proj("mla_k_up", ckv_all, pw["w_uk"], _ep_kfull, BF16, out_cols=2 * n_heads * HEAD_DIM,
                       tm=tml, tn=2 * TN, extras=[(krope_all, (tml, LANES), lambda i, j: (i, 0))])
        v_mla = _proj("mla_v_up", ckv_all, pw["w_uv"], _ep_plain, BF16, tm=tml, tn=2 * TN)
        merged = _mla_attention(q_full.reshape(b, t, -1), k_full.reshape(b, lp, -1), v_mla.reshape(b, lp, -1),
                                gates.reshape(b, t, 2 * d), out_b,
                                n_heads=n_heads, hb=min(MLA_HEADS_PER_STEP, n_heads), tq=min(tq_mla, t),
                                tk=min(tk, lp), q0=q0, l_valid=l_valid).reshape(m, d)

    def gate_extra(gate, tm_, tn_):
        if per_row:
            return (gate, (tm_, tn_), lambda i, j: (i, j))
        return (gate, (1, tn_), lambda i, j: (0, j))

    tn = _pick(d, 1024)
    x1 = _proj("out_proj", merged, pw["w_out"], _ep_residual, F32,
                     extras=[(xf, (tm, tn), lambda i, j: (i, j)), gate_extra(gt1, tm, tn)])
    h2 = _norm(x1, g_norm2, sc2, sh2, BF16)
    u = _proj("mlp_up", h2, pw["w_up"], _ep_relu2, BF16)
    tm2 = _pick(m, 1024)
    x2 = _proj("mlp_down", u, pw["w_down"], _ep_residual, F32, tm=tm2, tk=2048,
               extras=[(x1, (tm2, tn), lambda i, j: (i, j)), gate_extra(gt2, tm2, tn)])
    rows = (ckv.reshape(b, t, c_lora), krope.reshape(b, t, r_mla), k_new.reshape(b, t, n_kv, HEAD_DIM),
            v_new.reshape(b, t, n_kv, HEAD_DIM), ki_new.reshape(b, t, HEAD_DIM))
    return x2, rows


def kernel(x_prompt, x_sample, c_prompt, c_sample, cache_mla_ckv, cache_mla_krope, cache_dsa_k, cache_dsa_v, cache_idx_k, w_ada, b_ada, g_norm1, w_in, g_q_lora, w_uq, g_kv_lora, w_uk, w_uv, w_out, g_norm2, w_up, w_down, g_final):
    depth = w_in.shape[0]
    bp, tp, d = x_prompt.shape
    bs, ts, _ = x_sample.shape
    n_heads = d // HEAD_DIM
    dims = (d, w_uq.shape[1], cache_mla_ckv.shape[-1], cache_mla_krope.shape[-1], n_heads,
            cache_dsa_k.shape[3], n_heads)
    past_len = cache_mla_ckv.shape[2]

    c_all = jnp.concatenate([c_prompt, c_sample], axis=0)
    n_c = c_all.shape[0]
    c_all = jnp.pad(c_all, ((0, -(-n_c // 16) * 16 - n_c), (0, 0)))

    xp, xs = x_prompt, x_sample
    rows_p, rows_s = [], []
    for l in range(depth):
        ada = _ada(c_all, w_ada[l], b_ada[l])
        pw = _prep_weights(w_in[l], w_uq[l], w_uk[l], w_uv[l], w_out[l], w_up[l], w_down[l], dims)
        norms = (g_norm1[l], g_q_lora[l], g_kv_lora[l], g_norm2[l])
        xp2, rp = _layer(xp, ada[0:bp], None, 0, pw, *norms, dims, tq_mla=TQ_MLA, tq_dsa=TQ_DSA, tk=TK_PROMPT)
        past = (cache_mla_ckv[l], cache_mla_krope[l], cache_dsa_k[l], cache_dsa_v[l], cache_idx_k[l])
        xs2, rs = _layer(xs, ada[bp:bp + bs], past, past_len, pw, *norms, dims, tq_mla=TQ_MLA, tq_dsa=TQ_DSA,
                         tk=TK_SAMPLE)
        xp, xs = xp2.reshape(bp, tp, d), xs2.reshape(bs, ts, d)
        rows_p.append(rp)
        rows_s.append(rs)

    y_prompt = _norm(xp.reshape(bp * tp, d), g_final, None, None, F32).reshape(bp, tp, d)
    y_sample = _norm(xs.reshape(bs * ts, d), g_final, None, None, F32).reshape(bs, ts, d)
    stack = lambda rows, i: jnp.stack([r[i] for r in rows])
    return (y_prompt, y_sample,
            stack(rows_p, 0), stack(rows_p, 1), stack(rows_p, 2), stack(rows_p, 3), stack(rows_p, 4),
            stack(rows_s, 0), stack(rows_s, 1), stack(rows_s, 2), stack(rows_s, 3), stack(rows_s, 4))
```

```python
import functools

import numpy as np
import jax
import jax.numpy as jnp
from jax import lax
from jax.experimental import pallas as pl
from jax.experimental.pallas import tpu as pltpu

F32 = jnp.float32
BF16 = jnp.bfloat16
I32 = jnp.int32
I16 = jnp.int16
HALF_BITS = 16
I16_MIN, I16_MAX = -2 ** (HALF_BITS - 1), 2 ** (HALF_BITS - 1) - 1
BF16_EXACT_INT = 256

LANES = 128
CHUNK = 64
CHUNK_LOG2 = 6
EPS = 1e-6
NEG = -2.0 ** 100
INT_MIN = -2 ** 31
INDEX_BITS = 24
MLA_THETA = 10000.0
ROPE_THETA = 500000.0
HEAD_DIM = 128
DSA_ROT = HEAD_DIM // 4
IDX_ROT = HEAD_DIM // 4
TOPK_MAX = 256
N_ADA = 6
VMEM_LIMIT = 56 * 1024 * 1024
VMEM_LIMIT_DSA = 60 * 1024 * 1024
TQ_MLA, TQ_DSA = 512, 128
MLA_HEADS_PER_STEP = 4
DSA_GROUPS_PER_PASS = 4
TK_PROMPT, TK_SAMPLE = 512, 384


def _cparams(sem):
    return pltpu.CompilerParams(dimension_semantics=sem, vmem_limit_bytes=VMEM_LIMIT)


def _pick(n, pref):
    if n <= pref:
        return n
    t = pref
    while n % t:
        t //= 2
    return t


def _rope_table(pos, theta, rot_dim):
    half = rot_dim // 2
    inv = theta ** (-jnp.arange(half, dtype=F32) / half)
    ang = pos.astype(F32)[:, None] * inv[None, :]
    cos, sin = jnp.cos(ang), jnp.sin(ang)
    n = pos.shape[0]
    ones = jnp.ones((n, LANES - rot_dim), F32)
    zeros = jnp.zeros((n, LANES - rot_dim), F32)
    zh = jnp.zeros((n, half), F32)
    a = jnp.concatenate([cos, cos, ones], axis=1)
    b = jnp.concatenate([-sin, zh, zeros], axis=1)
    c = jnp.concatenate([zh, sin, zeros], axis=1)
    return jnp.concatenate([a, b, c], axis=1)


def _rope_slab(x, tab, half):
    a = tab[:, 0:LANES]
    b = tab[:, LANES:2 * LANES]
    c = tab[:, 2 * LANES:3 * LANES]
    return x * a + pltpu.roll(x, LANES - half, 1) * b + pltpu.roll(x, half, 1) * c


def _mm_body(*refs, nk, n_extra, n_out, epilogue):
    a_ref, w_ref = refs[0], refs[1]
    extra = refs[2:2 + n_extra]
    outs = refs[2 + n_extra:2 + n_extra + n_out]
    a = a_ref[...].astype(BF16)
    w = w_ref[...].astype(BF16)
    part = jnp.dot(a, w, preferred_element_type=F32)
    if nk == 1:
        epilogue(part, extra, outs)
        return
    acc_ref = refs[2 + n_extra + n_out]
    k = pl.program_id(2)

    @pl.when(k == 0)
    def _():
        acc_ref[...] = part

    @pl.when(k > 0)
    def _():
        acc_ref[...] += part

    @pl.when(k == nk - 1)
    def _():
        epilogue(acc_ref[...], extra, outs)


def _matmul(a, w, *, n, col0, tm, tn, tk, extras, outs, epilogue, name, order="nm"):
    m, kdim = a.shape
    assert col0 % tn == 0
    jb = col0 // tn
    gi, gj, gk = m // tm, n // tn, kdim // tk
    if order == "nm":
        grid = (gj, gi, gk)
        wrap = lambda f: (lambda pj, pi, pk: f(pi, pj, pk))
    else:
        grid = (gi, gj, gk)
        wrap = lambda f: (lambda pi, pj, pk: f(pi, pj, pk))
    in_specs = [
        pl.BlockSpec((tm, tk), wrap(lambda i, j, k: (i, k))),
        pl.BlockSpec((tk, tn), wrap(lambda i, j, k: (k, j + jb))),
    ]
    args = [a, w]
    for arr, blk, fn in extras:
        in_specs.append(pl.BlockSpec(blk, wrap(lambda i, j, k, fn=fn: fn(i, j))))
        args.append(arr)
    out_specs = [pl.BlockSpec(blk, wrap(lambda i, j, k, fn=fn: fn(i, j))) for _, blk, fn in outs]
    out_shape = [s for s, _, _ in outs]
    scratch = [pltpu.VMEM((tm, tn), F32)] if gk > 1 else []
    body = functools.partial(_mm_body, nk=gk, n_extra=len(extras), n_out=len(outs), epilogue=epilogue)
    res = pl.pallas_call(
        body, grid=grid, in_specs=in_specs, out_specs=out_specs, out_shape=out_shape,
        scratch_shapes=scratch, name=name,
        compiler_params=_cparams(("parallel", "parallel", "arbitrary")),
    )(*args)
    return res


def _ep_plain(acc, extra, outs):
    outs[0][...] = acc.astype(outs[0].dtype)


def _ep_rmsnorm(acc, extra, outs):
    g = extra[0][...]
    y = acc * lax.rsqrt(jnp.mean(acc * acc, axis=-1, keepdims=True) + EPS) * g
    outs[0][...] = y.astype(outs[0].dtype)


def _ep_sigmoid(acc, extra, outs):
    outs[0][...] = jax.nn.sigmoid(acc).astype(outs[0].dtype)


def _ep_relu2(acc, extra, outs):
    r = jnp.maximum(acc, 0.0)
    outs[0][...] = (r * r).astype(outs[0].dtype)


def _ep_residual(acc, extra, outs):
    x = extra[0][...]
    gate = extra[1][...]
    outs[0][...] = x + gate * acc


def _ep_rope(acc, extra, outs, *, half, period, roped, scale):
    tab = extra[0][...]
    o = outs[0]
    for s in range(acc.shape[1] // LANES):
        x = acc[:, s * LANES:(s + 1) * LANES]
        if (s % period) in roped:
            x = _rope_slab(x, tab, half)
        if scale != 1.0:
            x = x * scale
        o[:, s * LANES:(s + 1) * LANES] = x.astype(o.dtype)


def _ep_kfull(acc, extra, outs):
    kr = extra[0][...]
    o = outs[0]
    for h in range(acc.shape[1] // LANES):
        o[:, (2 * h) * LANES:(2 * h + 1) * LANES] = acc[:, h * LANES:(h + 1) * LANES].astype(o.dtype)
        o[:, (2 * h + 1) * LANES:(2 * h + 2) * LANES] = kr


TM, TN, TK = 512, 1024, 4096


def _proj(name, a, w, ep, out_dtype, *, cols=None, tm=TM, tn=TN, tk=TK, extras=(), out_cols=None, order="nm"):
    m, k = a.shape
    col0, n = (0, w.shape[1]) if cols is None else cols
    tm, tn, tk = _pick(m, tm), _pick(n, tn), _pick(k, tk)
    oc = n if out_cols is None else out_cols
    outs = [(jax.ShapeDtypeStruct((m, oc), out_dtype), (tm, tn * oc // n), lambda i, j: (i, j))]
    return _matmul(a, w, n=n, col0=col0, tm=tm, tn=tn, tk=tk, extras=list(extras), outs=outs, epilogue=ep,
                   name=name, order=order)[0]


def _ada_body(c_ref, w_ref, b_ref, o_ref):
    c = c_ref[...]
    a = (c * jax.nn.sigmoid(c)).astype(BF16)
    o_ref[...] = jnp.dot(a, w_ref[...].astype(BF16), preferred_element_type=F32) + b_ref[...]


def _ada(c_all, w_ada, b_ada):
    r, d = c_all.shape
    n = w_ada.shape[1]
    tn = _pick(n, 512)
    return pl.pallas_call(
        _ada_body, grid=(n // tn,),
        in_specs=[pl.BlockSpec((r, d), lambda j: (0, 0)),
                  pl.BlockSpec((d, tn), lambda j: (0, j)),
                  pl.BlockSpec((1, tn), lambda j: (0, j))],
        out_specs=pl.BlockSpec((r, tn), lambda j: (0, j)),
        out_shape=jax.ShapeDtypeStruct((r, n), F32), name="adaln",
        compiler_params=_cparams(("parallel",)),
    )(c_all, w_ada, b_ada.reshape(1, n))


def _norm_body(*refs, modulate):
    x = refs[0][...]
    g = refs[1][...]
    y = x * lax.rsqrt(jnp.mean(x * x, axis=-1, keepdims=True) + EPS) * g
    if modulate:
        y = y * (1.0 + refs[2][...]) + refs[3][...]
        o = refs[4]
    else:
        o = refs[2]
    o[...] = y.astype(o.dtype)


def _norm(x, g, sc, sh, out_dtype):
    m, d = x.shape
    tm = _pick(m, TM)
    in_specs = [pl.BlockSpec((tm, d), lambda i: (i, 0)), pl.BlockSpec((1, d), lambda i: (0, 0))]
    args = [x, g.reshape(1, d)]
    if sc is not None:
        per_row = sc.shape[0] == m
        blk = (tm, d) if per_row else (1, d)
        fn = (lambda i: (i, 0)) if per_row else (lambda i: (0, 0))
        in_specs += [pl.BlockSpec(blk, fn), pl.BlockSpec(blk, fn)]
        args += [sc, sh]
    return pl.pallas_call(
        functools.partial(_norm_body, modulate=sc is not None), grid=(m // tm,),
        in_specs=in_specs, out_specs=pl.BlockSpec((tm, d), lambda i: (i, 0)),
        out_shape=jax.ShapeDtypeStruct((m, d), out_dtype), name="rmsnorm",
        compiler_params=_cparams(("parallel",)),
    )(*args)


def _finish_body(z_ref, g_ref, tm_ref, td_ref,
                 ckv_o, ckvb_o, kr_o, krb_o, k_o, kb_o, v_o, vb_o, ki_o, kib_o, wi_o,
                 *, c_lora, r_mla, n_kv, idx_scale):
    tab_m = tm_ref[...]
    tab_d = td_ref[...]
    off = 0
    ckv = z_ref[:, off:off + c_lora]
    ckv = ckv * lax.rsqrt(jnp.mean(ckv * ckv, axis=-1, keepdims=True) + EPS) * g_ref[...]
    ckv_o[...] = ckv
    ckvb_o[...] = ckv.astype(BF16)
    off += c_lora
    kr = _rope_slab(z_ref[:, off:off + LANES], tab_m, r_mla // 2)
    kr_o[...] = kr[:, 0:r_mla]
    krb_o[...] = kr.astype(BF16)
    off += LANES
    for h in range(n_kv):
        kh = _rope_slab(z_ref[:, off + h * LANES:off + (h + 1) * LANES], tab_d, DSA_ROT // 2)
        k_o[:, h * LANES:(h + 1) * LANES] = kh
        kb_o[:, h * LANES:(h + 1) * LANES] = kh.astype(BF16)
    off += n_kv * LANES
    v = z_ref[:, off:off + n_kv * LANES]
    v_o[...] = v
    vb_o[...] = v.astype(BF16)
    off += n_kv * LANES
    ki = _rope_slab(z_ref[:, off:off + LANES], tab_d, IDX_ROT // 2)
    ki_o[...] = ki
    kib_o[...] = ki.astype(BF16)
    off += LANES
    wi_o[...] = z_ref[:, off:off + LANES] * idx_scale


def _finish(z, g_kv, tab_m, tab_d, *, c_lora, r_mla, n_kv, idx_scale):
    m, zc = z.shape
    tm = _pick(m, TM)
    kvw = n_kv * LANES
    row = lambda i: (i, 0)
    shapes = [(c_lora, F32), (c_lora, BF16), (r_mla, F32), (LANES, BF16), (kvw, F32), (kvw, BF16),
              (kvw, F32), (kvw, BF16), (LANES, F32), (LANES, BF16), (LANES, F32)]
    return pl.pallas_call(
        functools.partial(_finish_body, c_lora=c_lora, r_mla=r_mla, n_kv=n_kv, idx_scale=idx_scale),
        grid=(m // tm,),
        in_specs=[pl.BlockSpec((tm, zc), row), pl.BlockSpec((1, c_lora), lambda i: (0, 0)),
                  pl.BlockSpec((tm, 3 * LANES), row), pl.BlockSpec((tm, 3 * LANES), row)],
        out_specs=[pl.BlockSpec((tm, w), row) for w, _ in shapes],
        out_shape=[jax.ShapeDtypeStruct((m, w), dt) for w, dt in shapes], name="finish_small",
        compiler_params=_cparams(("parallel",)),
    )(z, g_kv.reshape(1, c_lora), tab_m, tab_d)


def _row_limits(q_first, tq, l_valid):
    rows = q_first + lax.broadcasted_iota(I32, (tq, 1), 0)
    chunk_end = lambda p: (lax.shift_right_logical(p, CHUNK_LOG2) + 1) * CHUNK
    lim = jnp.minimum(chunk_end(rows), l_valid)
    lim_min = jnp.minimum(chunk_end(q_first), l_valid)
    lim_max = jnp.minimum(chunk_end(q_first + tq - 1), l_valid)
    return lim, lim_min, lim_max


LOG2E = 1.4426950408889634


def _lane_tile(x, width):
    return x if width == LANES else jnp.concatenate([x] * (width // LANES), axis=1)


def _softmax_update(s, v, m_prev, l_prev, acc_prev):
    m_new = jnp.maximum(m_prev, jnp.max(s, axis=1, keepdims=True))
    p = jnp.exp2(s - _lane_tile(m_new, s.shape[1]))
    alpha = jnp.exp2(m_prev - m_new)
    l_new = alpha * l_prev + jnp.sum(p, axis=1, keepdims=True)
    acc_new = _lane_tile(alpha, acc_prev.shape[1]) * acc_prev + jnp.dot(p.astype(BF16), v,
                                                                        preferred_element_type=F32)
    return m_new, l_new, acc_new


def _softmax_step(s, v, m_sc, acc_sc):
    sb = s.astype(BF16)
    m_prev = m_sc[...]
    m_new = jnp.maximum(m_prev, jnp.max(sb, axis=1, keepdims=True).astype(F32))
    p = jnp.exp2(sb - _lane_tile(m_new.astype(BF16), s.shape[1]))
    alpha = jnp.exp2(m_prev - m_new)
    v_ones = jnp.concatenate([v, jnp.ones(v.shape, BF16)], axis=1)
    acc_sc[...] = _lane_tile(alpha, 2 * LANES) * acc_sc[...] + jnp.dot(p, v_ones, preferred_element_type=F32)
    m_sc[...] = m_new


def _softmax_result(acc_sc):
    acc = acc_sc[...]
    return acc[:, 0:LANES] * (1.0 / acc[:, LANES:2 * LANES])


_NT = (((1,), (1,)), ((), ()))


def _gate_mix(out_a, ga, gb, out_b):
    return (ga.astype(F32) * out_a + gb.astype(F32) * out_b.astype(F32)).astype(BF16)


def _mla_body(q_ref, k_ref, v_ref, ga_ref, gb_ref, ob_ref, o_ref, m_sc, acc_sc, s_sc,
              *, hb, tq, tk, q0, l_valid):
    q_first = q0 + pl.program_id(2) * tq
    lim, lim_min, lim_max = _row_limits(q_first, tq, l_valid)
    n_full = lim_min // tk
    n_all = (lim_max + tk - 1) // tk
    col = lax.broadcasted_iota(I32, (1, tk), 1)
    m_sc[...] = jnp.full(m_sc.shape, NEG, F32)
    acc_sc[...] = jnp.zeros(acc_sc.shape, F32)

    def scores(c, j):
        start = pl.multiple_of(c * tk, tk)
        return lax.dot_general(q_ref[:, j * 2 * LANES:(j + 1) * 2 * LANES],
                               k_ref[pl.ds(start, tk), j * 2 * LANES:(j + 1) * 2 * LANES], _NT,
                               preferred_element_type=F32)

    for j in range(hb):
        s_sc[j] = scores(0, j)

    def step(c, masked, last):
        start = pl.multiple_of(c * tk, tk)
        for j in range(hb):
            s = s_sc[j]
            if masked:
                s = jnp.where(col + start < lim, s, NEG)
            _softmax_step(s, v_ref[pl.ds(start, tk), j * LANES:(j + 1) * LANES], m_sc.at[j], acc_sc.at[j])
            if not last:
                s_sc[j] = scores(c + 1, j)

    def full_step(c, carry):
        step(c, False, False)
        return carry

    def masked_step(c, carry):
        step(c, True, False)
        return carry

    n_rot = n_all - 1
    n_rot_full = jnp.minimum(n_full, n_rot)
    lax.fori_loop(0, n_rot_full, full_step, 0)
    lax.fori_loop(n_rot_full, n_rot, masked_step, 0)
    step(n_rot, True, True)
    for j in range(hb):
        cols = slice(j * LANES, (j + 1) * LANES)
        o_ref[:, cols] = _gate_mix(_softmax_result(acc_sc.at[j]), ga_ref[:, cols], gb_ref[:, cols], ob_ref[:, cols])


def _mla_attention(q_full, k_full, v, gates, out_b, *, n_heads, hb, tq, tk, q0, l_valid):
    b, t, _ = q_full.shape
    lp = k_full.shape[1]
    kv_mode = dict(pipeline_mode=pl.Buffered(1)) if t // tq > 2 else {}
    stat = pltpu.VMEM((hb, tq, LANES), F32)
    return pl.pallas_call(
        functools.partial(_mla_body, hb=hb, tq=tq, tk=tk, q0=q0, l_valid=l_valid),
        grid=(b, n_heads // hb, t // tq),
        in_specs=[pl.BlockSpec((None, tq, hb * 2 * LANES), lambda bi, h, qi: (bi, qi, h)),
                  pl.BlockSpec((None, lp, hb * 2 * LANES), lambda bi, h, qi: (bi, 0, h), **kv_mode),
                  pl.BlockSpec((None, lp, hb * LANES), lambda bi, h, qi: (bi, 0, h)),
                  pl.BlockSpec((None, tq, hb * LANES), lambda bi, h, qi: (bi, qi, h)),
                  pl.BlockSpec((None, tq, hb * LANES), lambda bi, h, qi: (bi, qi, n_heads // hb + h)),
                  pl.BlockSpec((None, tq, hb * LANES), lambda bi, h, qi: (bi, qi, h))],
        out_specs=pl.BlockSpec((None, tq, hb * LANES), lambda bi, h, qi: (bi, qi, h)),
        out_shape=jax.ShapeDtypeStruct((b, t, n_heads * LANES), BF16),
        scratch_shapes=[stat, pltpu.VMEM((hb, tq, 2 * LANES), F32), pltpu.VMEM((hb, tq, tk), F32)],
        name="mla_attention",
        compiler_params=_cparams(("parallel", "parallel", "arbitrary")),
    )(q_full, k_full, v, gates, gates, out_b)


def _absorb_q_body(q_ref, w_ref, o_ref, *, c_lora):
    lat = lax.dot_general(q_ref[:, 0:LANES], w_ref[...], _NT, preferred_element_type=F32)
    o_ref[:, 0:c_lora] = lat.astype(o_ref.dtype)
    o_ref[:, c_lora:c_lora + LANES] = q_ref[:, LANES:2 * LANES]


def _absorb_q(q_full, w_uk, *, n_heads, c_lora):
    m = q_full.shape[0]
    return pl.pallas_call(
        functools.partial(_absorb_q_body, c_lora=c_lora), grid=(n_heads,),
        in_specs=[pl.BlockSpec((m, 2 * LANES), lambda h: (0, h)), pl.BlockSpec((c_lora, LANES), lambda h: (0, h))],
        out_specs=pl.BlockSpec((m, c_lora + LANES), lambda h: (0, h)),
        out_shape=jax.ShapeDtypeStruct((m, n_heads * (c_lora + LANES)), BF16), name="mla_absorb_q",
        compiler_params=_cparams(("parallel",)),
    )(q_full, w_uk)


def _absorb_o_body(o_lat_ref, w_ref, ga_ref, gb_ref, ob_ref, o_ref):
    out_a = jnp.dot(o_lat_ref[...], w_ref[...], preferred_element_type=F32)
    o_ref[...] = _gate_mix(out_a, ga_ref[...], gb_ref[...], ob_ref[...])


def _absorb_o(o_lat, w_uv, gates, out_b, *, n_heads, c_lora):
    m = o_lat.shape[0]
    head = lambda h: (0, h)
    return pl.pallas_call(
        _absorb_o_body, grid=(n_heads,),
        in_specs=[pl.BlockSpec((m, c_lora), head), pl.BlockSpec((c_lora, LANES), head),
                  pl.BlockSpec((m, LANES), head), pl.BlockSpec((m, LANES), lambda h: (0, n_heads + h)),
                  pl.BlockSpec((m, LANES), head)],
        out_specs=pl.BlockSpec((m, LANES), head),
        out_shape=jax.ShapeDtypeStruct((m, n_heads * LANES), BF16), name="mla_absorb_o",
        compiler_params=_cparams(("parallel",)),
    )(o_lat, w_uv, gates, gates, out_b)


def _mla_abs_body(q_ref, kc_ref, o_ref, *, bb, t, n_heads, c_lora, tk, q0, l_valid):
    rows = t * n_heads
    tok = lax.broadcasted_iota(I32, (t, n_heads, 1), 0).reshape(rows, 1)
    lim = jnp.minimum((lax.shift_right_logical(q0 + tok, CHUNK_LOG2) + 1) * CHUNK, l_valid)
    lim_min = min((q0 // CHUNK + 1) * CHUNK, l_valid)
    lim_max = min(((q0 + t - 1) // CHUNK + 1) * CHUNK, l_valid)
    col = lax.broadcasted_iota(I32, (1, tk), 1)
    for j in range(bb):
        q = q_ref[j]
        m = jnp.full((rows, LANES), NEG, F32)
        l = jnp.zeros((rows, LANES), F32)
        acc = jnp.zeros((rows, c_lora), F32)
        for c in range(-(-lim_max // tk)):
            kc = kc_ref[j, c * tk:(c + 1) * tk, :]
            s = lax.dot_general(q, kc, _NT, preferred_element_type=F32)
            if (c + 1) * tk > lim_min:
                s = jnp.where(col + c * tk < lim, s, NEG)
            m, l, acc = _softmax_update(s, kc[:, 0:c_lora], m, l, acc)
        o_ref[j] = (acc * _lane_tile(1.0 / l, c_lora)).astype(o_ref.dtype)


def _mla_abs_attention(q_abs, kc, *, t, n_heads, c_lora, tk, q0, l_valid):
    b, rows, dk = q_abs.shape
    lp = kc.shape[1]
    bb = _pick(b, 4)
    return pl.pallas_call(
        functools.partial(_mla_abs_body, bb=bb, t=t, n_heads=n_heads, c_lora=c_lora, tk=tk, q0=q0, l_valid=l_valid),
        grid=(b // bb,),
        in_specs=[pl.BlockSpec((bb, rows, dk), lambda i: (i, 0, 0)), pl.BlockSpec((bb, lp, dk), lambda i: (i, 0, 0))],
        out_specs=pl.BlockSpec((bb, rows, c_lora), lambda i: (i, 0, 0)),
        out_shape=jax.ShapeDtypeStruct((b, rows, c_lora), BF16), name="mla_absorbed_attention",
        compiler_params=_cparams(("parallel",)),
    )(q_abs, kc)


def _dsa_body(qb_ref, qi_ref, wi_ref, kit_ref, kt_ref, v_ref, o_ref,
              keys_sc, half_sc, m_sc, acc_sc, s_sc, *, tq, tk, q0, l_valid, n_idx, n_kv, rep, topk):
    q_first = q0 + pl.program_id(1) * tq
    lim, _, lim_max = _row_limits(q_first, tq, l_valid)
    n_all = (lim_max + tk - 1) // tk
    col = lax.broadcasted_iota(I32, (1, tk), 1)
    w = wi_ref[...]

    def score_chunk(c, carry):
        start = pl.multiple_of(c * tk, tk)
        kic = kit_ref[c]
        s = jnp.zeros((tq, tk), F32)
        for h in range(n_idx):
            y = jnp.dot(qi_ref[:, h * LANES:(h + 1) * LANES], kic, preferred_element_type=F32)
            s = s + w[:, h:h + 1] * jnp.maximum(y, 0.0)
        s = jnp.where(s == 0.0, 0.0, s)
        bits = lax.bitcast_convert_type(s, I32)
        key = jnp.where(bits >= 0, bits, bits ^ jnp.int32(0x7FFFFFFF))
        key = jnp.where(col + start < lim, key, jnp.int32(INT_MIN))
        keys_sc[c] = key
        half_sc[c] = lax.shift_right_arithmetic(key, HALF_BITS).astype(I16)
        return carry

    lax.fori_loop(0, n_all, score_chunk, 0)

    def count_rows(indicator):
        def count_chunk(c, acc):
            ind = indicator(c, keys_sc[c])
            part = ind[:, 0:LANES]
            for j in range(1, tk // LANES):
                part = part + ind[:, j * LANES:(j + 1) * LANES]
            return acc + part

        acc = lax.fori_loop(0, n_all, count_chunk, jnp.zeros((tq, LANES), F32))
        return jnp.sum(acc, axis=1, keepdims=True)

    def count_ge(cand):
        return count_rows(lambda c, kk: jnp.where(kk >= cand, 1.0, 0.0))

    def count_half_ge(cand):
        c16 = _lane_tile(jnp.broadcast_to(cand, (tq, LANES)).astype(I16), tk)

        def count_chunk(c, acc):
            ind = jnp.where(half_sc[c] >= c16, jnp.ones((), BF16), jnp.zeros((), BF16))
            part = ind[:, 0:LANES]
            for j in range(1, tk // LANES):
                part = part + ind[:, j * LANES:(j + 1) * LANES]
            return acc + part

        acc = lax.fori_loop(0, n_all, count_chunk, jnp.zeros((tq, LANES), BF16))
        return jnp.sum(acc.astype(F32), axis=1, keepdims=True)

    def search_half(goal):
        def bit_pass(i, carry):
            t, n_at_t = carry
            cand = t + lax.shift_left(jnp.int32(1), jnp.int32(HALF_BITS - 1) - i)
            n = count_half_ge(cand)
            keep = n >= goal
            return jnp.where(keep, cand, t), jnp.where(keep, n, n_at_t)

        return lax.fori_loop(0, HALF_BITS, bit_pass,
                             (jnp.full((tq, 1), I16_MIN, I32), jnp.full((tq, 1), -1.0, F32)))

    assert keys_sc.shape[0] * (tk // LANES) <= BF16_EXACT_INT, "per-lane partial counts must stay exact in bf16"
    t_hi, n_hi = search_half(float(topk))
    n_above = jnp.where(t_hi < I16_MAX, count_half_ge(jnp.minimum(t_hi + 1, I16_MAX)), 0.0)

    def low_halves(c, carry):
        low = jnp.bitwise_and(keys_sc[c], 2 ** HALF_BITS - 1) + I16_MIN
        half_sc[c] = jnp.where(half_sc[c] == _lane_tile(jnp.broadcast_to(t_hi, (tq, LANES)).astype(I16), tk),
                               low.astype(I16), jnp.full((), I16_MIN, I16))
        return carry

    lax.fori_loop(0, n_all, low_halves, 0)
    t_lo, n_lo = search_half(topk - n_above)
    t = t_hi * 2 ** HALF_BITS + (t_lo - I16_MIN)
    n_at_t = jnp.where(n_lo >= 0, n_above + n_lo, jnp.maximum(n_hi, 0.0))
    thr = jnp.maximum(t, jnp.int32(INT_MIN + 1))

    @pl.when(jnp.max(n_at_t) > topk)
    def _():
        slots = topk - count_ge(thr + 1)

        def tied_before(bound):
            return count_rows(lambda c, kk: jnp.where(kk == thr, jnp.where(col + c * tk < bound, 1.0, 0.0), 0.0))

        def index_pass(i, j_last):
            cand = j_last + lax.shift_left(jnp.int32(1), jnp.int32(INDEX_BITS - 1) - i)
            return jnp.where(tied_before(cand) < slots, cand, j_last)

        j_last = lax.fori_loop(0, INDEX_BITS, index_pass, jnp.zeros((tq, 1), I32))

        def demote(c, carry):
            kk = keys_sc[c]
            keys_sc[c] = jnp.where(kk == thr, jnp.where(col + c * tk > j_last, thr - 1, kk), kk)
            return carry

        lax.fori_loop(0, n_all, demote, 0)

    rows = rep * tq
    tok = jnp.bitwise_and(lax.broadcasted_iota(I32, (rows, LANES), 0), tq - 1)
    eye = jnp.where(tok == lax.broadcasted_iota(I32, (rows, LANES), 1), 1.0, 0.0).astype(BF16)

    gb = m_sc.shape[0]

    def mask_bias(c):
        bias = jnp.where(keys_sc[c] >= thr, 0.0, NEG).astype(BF16)
        if tq < LANES:
            bias = jnp.concatenate([bias, jnp.zeros((LANES - tq, tk), BF16)], axis=0)
        return bias

    def scores(c, g, bias):
        qs = jnp.concatenate([qb_ref[:, (g * rep + r) * LANES:(g * rep + r + 1) * LANES]
                              for r in range(rep)], axis=0)
        return jnp.dot(jnp.concatenate([qs, eye], axis=1), jnp.concatenate([kt_ref[g, c], bias], axis=0),
                       preferred_element_type=F32)

    def attend(c, g0, nxt_c, nxt_g0):
        start = pl.multiple_of(c * tk, tk)
        bias = None if nxt_g0 is None else mask_bias(nxt_c)
        for j in range(gb):
            g = g0 + j
            _softmax_step(s_sc[j], v_ref[pl.ds(start, tk), g * LANES:(g + 1) * LANES], m_sc.at[j], acc_sc.at[j])
            if nxt_g0 is not None:
                s_sc[j] = scores(nxt_c, nxt_g0 + j, bias)

    bias0 = mask_bias(0)
    for j in range(gb):
        s_sc[j] = scores(0, j, bias0)
    for g0 in range(0, n_kv, gb):
        m_sc[...] = jnp.full(m_sc.shape, NEG, F32)
        acc_sc[...] = jnp.zeros(acc_sc.shape, F32)

        def attend_chunk(c, carry, g0=g0):
            attend(c, g0, c + 1, g0)
            return carry

        lax.fori_loop(0, n_all - 1, attend_chunk, 0)
        attend(n_all - 1, g0, 0, g0 + gb if g0 + gb < n_kv else None)
        for j in range(gb):
            o = _softmax_result(acc_sc.at[j])
            for r in range(rep):
                h = (g0 + j) * rep + r
                o_ref[:, h * LANES:(h + 1) * LANES] = o[r * tq:(r + 1) * tq].astype(o_ref.dtype)


def _dsa_attention(qb, qi, wi, ki, k, v, *, n_kv, rep, n_idx, tq, tk, q0, l_valid, topk):
    b, t, _ = qb.shape
    lp = k.shape[1]
    kvw = n_kv * LANES
    assert tq <= LANES and tq & (tq - 1) == 0, "the token one-hot needs a power-of-two block of at most 128 rows"
    kt = k.reshape(b, lp // tk, tk, n_kv, LANES).transpose(0, 3, 1, 4, 2)
    kit = ki.reshape(b, lp // tk, tk, LANES).transpose(0, 1, 3, 2)
    kv_mode = dict(pipeline_mode=pl.Buffered(1)) if t // tq > 2 else {}
    stat = pltpu.VMEM((min(DSA_GROUPS_PER_PASS, n_kv), rep * tq, LANES), F32)
    return pl.pallas_call(
        functools.partial(_dsa_body, tq=tq, tk=tk, q0=q0, l_valid=l_valid, n_idx=n_idx, n_kv=n_kv, rep=rep,
                          topk=topk),
        grid=(b, t // tq),
        in_specs=[pl.BlockSpec((None, tq, n_kv * rep * LANES), lambda bi, qb_: (bi, qb_, 0)),
                  pl.BlockSpec((None, tq, n_idx * LANES), lambda bi, qb_: (bi, qb_, 0)),
                  pl.BlockSpec((None, tq, LANES), lambda bi, qb_: (bi, qb_, 0)),
                  pl.BlockSpec((None, lp // tk, LANES, tk), lambda bi, qb_: (bi, 0, 0, 0), **kv_mode),
                  pl.BlockSpec((None, n_kv, lp // tk, LANES, tk), lambda bi, qb_: (bi, 0, 0, 0, 0), **kv_mode),
                  pl.BlockSpec((None, lp, kvw), lambda bi, qb_: (bi, 0, 0), **kv_mode)],
        out_specs=pl.BlockSpec((None, tq, n_kv * rep * LANES), lambda bi, qb_: (bi, qb_, 0)),
        out_shape=jax.ShapeDtypeStruct((b, t, n_kv * rep * LANES), BF16),
        scratch_shapes=[pltpu.VMEM((lp // tk, tq, tk), I32), pltpu.VMEM((lp // tk, tq, tk), I16), stat,
                        pltpu.VMEM((stat.shape[0], rep * tq, 2 * LANES), F32),
                        pltpu.VMEM((stat.shape[0], rep * tq, tk), F32)],
        name="dsa_attention",
        compiler_params=pltpu.CompilerParams(dimension_semantics=("parallel", "arbitrary"),
                                             vmem_limit_bytes=VMEM_LIMIT_DSA),
    )(qb, qi, wi, kit, kt, v)


def _pad_cols(w, width):
    return jnp.pad(w, ((0, 0), (0, width - w.shape[1])))


def _prep_weights(w_in, w_uq, w_uk, w_uv, w_out, w_up, w_down, dims):
    d, q_lora, c_lora, r_mla, n_heads, n_kv, n_idx = dims
    splits = (q_lora, c_lora, r_mla, n_heads * HEAD_DIM, n_kv * HEAD_DIM, n_kv * HEAD_DIM,
              n_idx * HEAD_DIM, HEAD_DIM, n_idx, d, d)
    offs = np.concatenate([[0], np.cumsum(splits)])
    sec = [w_in[:, int(offs[i]):int(offs[i + 1])] for i in range(len(splits))]
    small = jnp.concatenate([sec[1], _pad_cols(sec[2], LANES), sec[4], sec[5], sec[7],
                             _pad_cols(sec[8], LANES)], axis=1)
    small = _pad_cols(small, -(-small.shape[1] // 512) * 512)
    parts = [("q", sec[0]), ("small", small), ("qb", sec[3]), ("qi", sec[6]),
             ("gates", jnp.concatenate([sec[9], sec[10]], axis=1))]
    windows, off = {}, 0
    for i, (key, p) in enumerate(parts):
        windows[key] = (off, p.shape[1])
        parts[i] = (key, _pad_cols(p, -(-p.shape[1] // TN) * TN))
        off += parts[i][1].shape[1]
    wq = w_uq.reshape(q_lora, n_heads, HEAD_DIM + r_mla)
    wq = jnp.pad(wq, ((0, 0), (0, 0), (0, 2 * LANES - HEAD_DIM - r_mla)))
    return dict(
        w_in=jnp.concatenate([p for _, p in parts], axis=1).astype(BF16), win=windows,
        w_uq=wq.reshape(q_lora, n_heads * 2 * LANES).astype(BF16),
        w_uk=w_uk.reshape(c_lora, n_heads * HEAD_DIM).astype(BF16),
        w_uv=w_uv.reshape(c_lora, n_heads * HEAD_DIM).astype(BF16),
        w_out=w_out.astype(BF16), w_up=w_up.astype(BF16), w_down=w_down.astype(BF16))


def _layer(x, ada, past, q0, pw, g_norm1, g_q_lora, g_kv_lora, g_norm2, dims, tq_mla, tq_dsa, tk):
    d, q_lora, c_lora, r_mla, n_heads, n_kv, n_idx = dims
    b, t, _ = x.shape
    m = b * t
    xf = x.reshape(m, d)
    if ada.shape[0] == 1:
        mods = [ada[:, i * d:(i + 1) * d] for i in range(N_ADA)]
    else:
        mods = [jnp.repeat(ada[:, i * d:(i + 1) * d], t, axis=0) for i in range(N_ADA)]
    sh1, sc1, gt1, sh2, sc2, gt2 = mods
    per_row = ada.shape[0] != 1

    pos = q0 + jnp.arange(t, dtype=I32)
    tab_m = jnp.tile(_rope_table(pos, MLA_THETA, r_mla), (b, 1))
    tab_d = jnp.tile(_rope_table(pos, ROPE_THETA, DSA_ROT), (b, 1))

    h = _norm(xf, g_norm1, sc1, sh1, BF16)

    def tab_extra(tab, tm):
        return (tab, (tm, 3 * LANES), lambda i, j: (i, 0))

    tm = _pick(m, 512)
    qn = _proj("proj_qlat", h, pw["w_in"], _ep_rmsnorm, BF16, cols=pw["win"]["q"], tn=q_lora,
                     extras=[(g_q_lora.reshape(1, q_lora), (1, q_lora), lambda i, j: (0, 0))])
    z_small = _proj("proj_small", h, pw["w_in"], _ep_plain, F32, cols=pw["win"]["small"],
                    tn=TN if pw["win"]["small"][1] % TN == 0 else 512)
    q_dsa = _proj("proj_qdsa", h, pw["w_in"],
                  functools.partial(_ep_rope, half=DSA_ROT // 2, period=1, roped=(0,),
                                    scale=HEAD_DIM ** -0.5 * LOG2E),
                  BF16, cols=pw["win"]["qb"], extras=[tab_extra(tab_d, tm)])
    q_idx = _proj("proj_qidx", h, pw["w_in"],
                  functools.partial(_ep_rope, half=IDX_ROT // 2, period=1, roped=(0,), scale=1.0),
                  BF16, cols=pw["win"]["qi"], extras=[tab_extra(tab_d, tm)])
    gates = _proj("proj_gates", h, pw["w_in"], _ep_sigmoid, BF16, cols=pw["win"]["gates"])

    (ckv, ckv_b, krope, krope_b, k_new, k_new_b, v_new, v_new_b, ki_new, ki_new_b, wi) = _finish(
        z_small, g_kv_lora, tab_m, tab_d, c_lora=c_lora, r_mla=r_mla, n_kv=n_kv, idx_scale=n_idx ** -0.5)

    kvw = n_kv * HEAD_DIM
    if past is None:
        l_valid = t
        lp = t
        ckv_all, krope_all, k_all, v_all, ki_all = ckv_b, krope_b, k_new_b, v_new_b, ki_new_b
    else:
        p_ckv, p_krope, p_k, p_v, p_ki = past
        l_valid = p_ckv.shape[1] + t
        lp = -(-l_valid // tk) * tk

        def cat(p, new, width):
            p = p.reshape(b, p.shape[1], -1).astype(BF16)
            if p.shape[2] < width:
                p = jnp.pad(p, ((0, 0), (0, 0), (0, width - p.shape[2])))
            full = jnp.concatenate([p, new.reshape(b, t, width)], axis=1)
            return jnp.pad(full, ((0, 0), (0, lp - l_valid), (0, 0))).reshape(b * lp, width)

        ckv_all = cat(p_ckv, ckv_b, c_lora)
        krope_all = cat(p_krope, krope_b, LANES)
        k_all = cat(p_k, k_new_b, kvw)
        v_all = cat(p_v, v_new_b, kvw)
        ki_all = cat(p_ki, ki_new_b, LANES)

    out_b = _dsa_attention(q_dsa.reshape(b, t, -1), q_idx.reshape(b, t, -1), wi.reshape(b, t, LANES),
                           ki_all.reshape(b, lp, LANES), k_all.reshape(b, lp, kvw), v_all.reshape(b, lp, kvw),
                           n_kv=n_kv, rep=n_heads // n_kv, n_idx=n_idx, tq=min(tq_dsa, t), tk=min(tk, lp),
                           q0=q0, l_valid=l_valid, topk=min(TOPK_MAX, l_valid // 4))

    tms = _pick(m, 2 * TM)
    q_full = _proj("mla_q_up", qn, pw["w_uq"],
                   functools.partial(_ep_rope, half=r_mla // 2, period=2, roped=(1,),
                                     scale=(HEAD_DIM + r_mla) ** -0.5 * LOG2E),
                   BF16, tm=tms, tn=2 * TN, extras=[tab_extra(tab_m, tms)])
    if past is not None and t * n_heads <= 1024 and n_heads % 8 == 0:
        kc = jnp.concatenate([ckv_all.reshape(b, lp, c_lora), krope_all.reshape(b, lp, LANES)], axis=2)
        q_abs = _absorb_q(q_full, pw["w_uk"], n_heads=n_heads, c_lora=c_lora)
        o_lat = _mla_abs_attention(q_abs.reshape(b, t * n_heads, c_lora + LANES), kc, t=t, n_heads=n_heads,
                                   c_lora=c_lora, tk=min(tk, lp), q0=q0, l_valid=l_valid)
        merged = _absorb_o(o_lat.reshape(m, n_heads * c_lora), pw["w_uv"], gates, out_b.reshape(m, d),
                           n_heads=n_heads, c_lora=c_lora)
    else:
        tml = _pick(b * lp, 2 * TM)
        k_full = _proj("mla_k_up", ckv_all, pw["w_uk"], _ep_kfull, BF16, out_cols=2 * n_heads * HEAD_DIM,
                       tm=tml, tn=2 * TN, extras=[(krope_all, (tml, LANES), lambda i, j: (i, 0))])
        v_mla = _proj("mla_v_up", ckv_all, pw["w_uv"], _ep_plain, BF16, tm=tml, tn=2 * TN)
        merged = _mla_attention(q_full.reshape(b, t, -1), k_full.reshape(b, lp, -1), v_mla.reshape(b, lp, -1),
                                gates.reshape(b, t, 2 * d), out_b,
                                n_heads=n_heads, hb=min(MLA_HEADS_PER_STEP, n_heads), tq=min(tq_mla, t),
                                tk=min(tk, lp), q0=q0, l_valid=l_valid).reshape(m, d)

    def gate_extra(gate, tm_, tn_):
        if per_row:
            return (gate, (tm_, tn_), lambda i, j: (i, j))
        return (gate, (1, tn_), lambda i, j: (0, j))

    tn = _pick(d, 1024)
    x1 = _proj("out_proj", merged, pw["w_out"], _ep_residual, F32,
                     extras=[(xf, (tm, tn), lambda i, j: (i, j)), gate_extra(gt1, tm, tn)])
    h2 = _norm(x1, g_norm2, sc2, sh2, BF16)
    u = _proj("mlp_up", h2, pw["w_up"], _ep_relu2, BF16)
    tm2 = _pick(m, 1024)
    x2 = _proj("mlp_down", u, pw["w_down"], _ep_residual, F32, tm=tm2, tk=2048,
               extras=[(x1, (tm2, tn), lambda i, j: (i, j)), gate_extra(gt2, tm2, tn)])
    rows = (ckv.reshape(b, t, c_lora), krope.reshape(b, t, r_mla), k_new.reshape(b, t, n_kv, HEAD_DIM),
            v_new.reshape(b, t, n_kv, HEAD_DIM), ki_new.reshape(b, t, HEAD_DIM))
    return x2, rows


def kernel(x_prompt, x_sample, c_prompt, c_sample, cache_mla_ckv, cache_mla_krope, cache_dsa_k, cache_dsa_v, cache_idx_k, w_ada, b_ada, g_norm1, w_in, g_q_lora, w_uq, g_kv_lora, w_uk, w_uv, w_out, g_norm2, w_up, w_down, g_final):
    depth = w_in.shape[0]
    bp, tp, d = x_prompt.shape
    bs, ts, _ = x_sample.shape
    n_heads = d // HEAD_DIM
    dims = (d, w_uq.shape[1], cache_mla_ckv.shape[-1], cache_mla_krope.shape[-1], n_heads,
            cache_dsa_k.shape[3], n_heads)
    past_len = cache_mla_ckv.shape[2]

    c_all = jnp.concatenate([c_prompt, c_sample], axis=0)
    n_c = c_all.shape[0]
    c_all = jnp.pad(c_all, ((0, -(-n_c // 16) * 16 - n_c), (0, 0)))

    xp, xs = x_prompt, x_sample
    rows_p, rows_s = [], []
    for l in range(depth):
        ada = _ada(c_all, w_ada[l], b_ada[l])
        pw = _prep_weights(w_in[l], w_uq[l], w_uk[l], w_uv[l], w_out[l], w_up[l], w_down[l], dims)
        norms = (g_norm1[l], g_q_lora[l], g_kv_lora[l], g_norm2[l])
        xp2, rp = _layer(xp, ada[0:bp], None, 0, pw, *norms, dims, tq_mla=TQ_MLA, tq_dsa=TQ_DSA, tk=TK_PROMPT)
        past = (cache_mla_ckv[l], cache_mla_krope[l], cache_dsa_k[l], cache_dsa_v[l], cache_idx_k[l])
        xs2, rs = _layer(xs, ada[bp:bp + bs], past, past_len, pw, *norms, dims, tq_mla=TQ_MLA, tq_dsa=TQ_DSA,
                         tk=TK_SAMPLE)
        xp, xs = xp2.reshape(bp, tp, d), xs2.reshape(bs, ts, d)
        rows_p.append(rp)
        rows_s.append(rs)

    y_prompt = _norm(xp.reshape(bp * tp, d), g_final, None, None, F32).reshape(bp, tp, d)
    y_sample = _norm(xs.reshape(bs * ts, d), g_final, None, None, F32).reshape(bs, ts, d)
    stack = lambda rows, i: jnp.stack([r[i] for r in rows])
    return (y_prompt, y_sample,
            stack(rows_p, 0), stack(rows_p, 1), stack(rows_p, 2), stack(rows_p, 3), stack(rows_p, 4),
            stack(rows_s, 0), stack(rows_s, 1), stack(rows_s, 2), stack(rows_s, 3), stack(rows_s, 4))
```

```python
import functools

import numpy as np
import jax
import jax.numpy as jnp
from jax import lax
from jax.experimental import pallas as pl
from jax.experimental.pallas import tpu as pltpu

F32 = jnp.float32
BF16 = jnp.bfloat16
I32 = jnp.int32
I16 = jnp.int16
HALF_BITS = 16
I16_MIN, I16_MAX = -2 ** (HALF_BITS - 1), 2 ** (HALF_BITS - 1) - 1
BF16_EXACT_INT = 256

LANES = 128
CHUNK = 64
CHUNK_LOG2 = 6
EPS = 1e-6
NEG = -2.0 ** 100
INT_MIN = -2 ** 31
INDEX_BITS = 24
MLA_THETA = 10000.0
ROPE_THETA = 500000.0
HEAD_DIM = 128
DSA_ROT = HEAD_DIM // 4
IDX_ROT = HEAD_DIM // 4
TOPK_MAX = 256
N_ADA = 6
VMEM_LIMIT = 56 * 1024 * 1024
VMEM_LIMIT_DSA = 60 * 1024 * 1024
TQ_MLA, TQ_DSA = 512, 128
MLA_HEADS_PER_STEP = 4
DSA_GROUPS_PER_PASS = 4
TK_PROMPT, TK_SAMPLE = 512, 384


def _cparams(sem):
    return pltpu.CompilerParams(dimension_semantics=sem, vmem_limit_bytes=VMEM_LIMIT)


def _pick(n, pref):
    if n <= pref:
        return n
    t = pref
    while n % t:
        t //= 2
    return t


def _rope_table(pos, theta, rot_dim):
    half = rot_dim // 2
    inv = theta ** (-jnp.arange(half, dtype=F32) / half)
    ang = pos.astype(F32)[:, None] * inv[None, :]
    cos, sin = jnp.cos(ang), jnp.sin(ang)
    n = pos.shape[0]
    ones = jnp.ones((n, LANES - rot_dim), F32)
    zeros = jnp.zeros((n, LANES - rot_dim), F32)
    zh = jnp.zeros((n, half), F32)
    a = jnp.concatenate([cos, cos, ones], axis=1)
    b = jnp.concatenate([-sin, zh, zeros], axis=1)
    c = jnp.concatenate([zh, sin, zeros], axis=1)
    return jnp.concatenate([a, b, c], axis=1)


def _rope_slab(x, tab, half):
    a = tab[:, 0:LANES]
    b = tab[:, LANES:2 * LANES]
    c = tab[:, 2 * LANES:3 * LANES]
    return x * a + pltpu.roll(x, LANES - half, 1) * b + pltpu.roll(x, half, 1) * c


def _mm_body(*refs, nk, n_extra, n_out, epilogue):
    a_ref, w_ref = refs[0], refs[1]
    extra = refs[2:2 + n_extra]
    outs = refs[2 + n_extra:2 + n_extra + n_out]
    a = a_ref[...].astype(BF16)
    w = w_ref[...].astype(BF16)
    part = jnp.dot(a, w, preferred_element_type=F32)
    if nk == 1:
        epilogue(part, extra, outs)
        return
    acc_ref = refs[2 + n_extra + n_out]
    k = pl.program_id(2)

    @pl.when(k == 0)
    def _():
        acc_ref[...] = part

    @pl.when(k > 0)
    def _():
        acc_ref[...] += part

    @pl.when(k == nk - 1)
    def _():
        epilogue(acc_ref[...], extra, outs)


def _matmul(a, w, *, n, col0, tm, tn, tk, extras, outs, epilogue, name, order="nm"):
    m, kdim = a.shape
    assert col0 % tn == 0
    jb = col0 // tn
    gi, gj, gk = m // tm, n // tn, kdim // tk
    if order == "nm":
        grid = (gj, gi, gk)
        wrap = lambda f: (lambda pj, pi, pk: f(pi, pj, pk))
    else:
        grid = (gi, gj, gk)
        wrap = lambda f: (lambda pi, pj, pk: f(pi, pj, pk))
    in_specs = [
        pl.BlockSpec((tm, tk), wrap(lambda i, j, k: (i, k))),
        pl.BlockSpec((tk, tn), wrap(lambda i, j, k: (k, j + jb))),
    ]
    args = [a, w]
    for arr, blk, fn in extras:
        in_specs.append(pl.BlockSpec(blk, wrap(lambda i, j, k, fn=fn: fn(i, j))))
        args.append(arr)
    out_specs = [pl.BlockSpec(blk, wrap(lambda i, j, k, fn=fn: fn(i, j))) for _, blk, fn in outs]
    out_shape = [s for s, _, _ in outs]
    scratch = [pltpu.VMEM((tm, tn), F32)] if gk > 1 else []
    body = functools.partial(_mm_body, nk=gk, n_extra=len(extras), n_out=len(outs), epilogue=epilogue)
    res = pl.pallas_call(
        body, grid=grid, in_specs=in_specs, out_specs=out_specs, out_shape=out_shape,
        scratch_shapes=scratch, name=name,
        compiler_params=_cparams(("parallel", "parallel", "arbitrary")),
    )(*args)
    return res


def _ep_plain(acc, extra, outs):
    outs[0][...] = acc.astype(outs[0].dtype)


def _ep_rmsnorm(acc, extra, outs):
    g = extra[0][...]
    y = acc * lax.rsqrt(jnp.mean(acc * acc, axis=-1, keepdims=True) + EPS) * g
    outs[0][...] = y.astype(outs[0].dtype)


def _ep_sigmoid(acc, extra, outs):
    outs[0][...] = jax.nn.sigmoid(acc).astype(outs[0].dtype)


def _ep_relu2(acc, extra, outs):
    r = jnp.maximum(acc, 0.0)
    outs[0][...] = (r * r).astype(outs[0].dtype)


def _ep_residual(acc, extra, outs):
    x = extra[0][...]
    gate = extra[1][...]
    outs[0][...] = x + gate * acc


def _ep_rope(acc, extra, outs, *, half, period, roped, scale):
    tab = extra[0][...]
    o = outs[0]
    for s in range(acc.shape[1] // LANES):
        x = acc[:, s * LANES:(s + 1) * LANES]
        if (s % period) in roped:
            x = _rope_slab(x, tab, half)
        if scale != 1.0:
            x = x * scale
        o[:, s * LANES:(s + 1) * LANES] = x.astype(o.dtype)


def _ep_kfull_t(acc, extra, outs):
    kr = extra[0][...]
    o = outs[0]
    for h in range(acc.shape[0] // LANES):
        o[(2 * h) * LANES:(2 * h + 1) * LANES, :] = acc[h * LANES:(h + 1) * LANES, :].astype(o.dtype)
        o[(2 * h + 1) * LANES:(2 * h + 2) * LANES, :] = kr


TM, TN, TK = 512, 1024, 4096


def _proj(name, a, w, ep, out_dtype, *, cols=None, tm=TM, tn=TN, tk=TK, extras=(), out_cols=None, order="nm"):
    m, k = a.shape
    col0, n = (0, w.shape[1]) if cols is None else cols
    tm, tn, tk = _pick(m, tm), _pick(n, tn), _pick(k, tk)
    oc = n if out_cols is None else out_cols
    outs = [(jax.ShapeDtypeStruct((m, oc), out_dtype), (tm, tn * oc // n), lambda i, j: (i, j))]
    return _matmul(a, w, n=n, col0=col0, tm=tm, tn=tn, tk=tk, extras=list(extras), outs=outs, epilogue=ep,
                   name=name, order=order)[0]


def _ada_body(c_ref, w_ref, b_ref, o_ref):
    c = c_ref[...]
    a = (c * jax.nn.sigmoid(c)).astype(BF16)
    o_ref[...] = jnp.dot(a, w_ref[...].astype(BF16), preferred_element_type=F32) + b_ref[...]


def _ada(c_all, w_ada, b_ada):
    r, d = c_all.shape
    n = w_ada.shape[1]
    tn = _pick(n, 512)
    return pl.pallas_call(
        _ada_body, grid=(n // tn,),
        in_specs=[pl.BlockSpec((r, d), lambda j: (0, 0)),
                  pl.BlockSpec((d, tn), lambda j: (0, j)),
                  pl.BlockSpec((1, tn), lambda j: (0, j))],
        out_specs=pl.BlockSpec((r, tn), lambda j: (0, j)),
        out_shape=jax.ShapeDtypeStruct((r, n), F32), name="adaln",
        compiler_params=_cparams(("parallel",)),
    )(c_all, w_ada, b_ada.reshape(1, n))


def _norm_body(*refs, modulate):
    x = refs[0][...]
    g = refs[1][...]
    y = x * lax.rsqrt(jnp.mean(x * x, axis=-1, keepdims=True) + EPS) * g
    if modulate:
        y = y * (1.0 + refs[2][...]) + refs[3][...]
        o = refs[4]
    else:
        o = refs[2]
    o[...] = y.astype(o.dtype)


def _norm(x, g, sc, sh, out_dtype):
    m, d = x.shape
    tm = _pick(m, TM)
    in_specs = [pl.BlockSpec((tm, d), lambda i: (i, 0)), pl.BlockSpec((1, d), lambda i: (0, 0))]
    args = [x, g.reshape(1, d)]
    if sc is not None:
        per_row = sc.shape[0] == m
        blk = (tm, d) if per_row else (1, d)
        fn = (lambda i: (i, 0)) if per_row else (lambda i: (0, 0))
        in_specs += [pl.BlockSpec(blk, fn), pl.BlockSpec(blk, fn)]
        args += [sc, sh]
    return pl.pallas_call(
        functools.partial(_norm_body, modulate=sc is not None), grid=(m // tm,),
        in_specs=in_specs, out_specs=pl.BlockSpec((tm, d), lambda i: (i, 0)),
        out_shape=jax.ShapeDtypeStruct((m, d), out_dtype), name="rmsnorm",
        compiler_params=_cparams(("parallel",)),
    )(*args)


def _finish_body(z_ref, g_ref, tm_ref, td_ref,
                 ckv_o, ckvb_o, kr_o, krb_o, k_o, kb_o, v_o, vb_o, ki_o, kib_o, wi_o,
                 *, c_lora, r_mla, n_kv, idx_scale):
    tab_m = tm_ref[...]
    tab_d = td_ref[...]
    off = 0
    ckv = z_ref[:, off:off + c_lora]
    ckv = ckv * lax.rsqrt(jnp.mean(ckv * ckv, axis=-1, keepdims=True) + EPS) * g_ref[...]
    ckv_o[...] = ckv
    ckvb_o[...] = ckv.astype(BF16)
    off += c_lora
    kr = _rope_slab(z_ref[:, off:off + LANES], tab_m, r_mla // 2)
    kr_o[...] = kr[:, 0:r_mla]
    krb_o[...] = kr.astype(BF16)
    off += LANES
    for h in range(n_kv):
        kh = _rope_slab(z_ref[:, off + h * LANES:off + (h + 1) * LANES], tab_d, DSA_ROT // 2)
        k_o[:, h * LANES:(h + 1) * LANES] = kh
        kb_o[:, h * LANES:(h + 1) * LANES] = kh.astype(BF16)
    off += n_kv * LANES
    v = z_ref[:, off:off + n_kv * LANES]
    v_o[...] = v
    vb_o[...] = v.astype(BF16)
    off += n_kv * LANES
    ki = _rope_slab(z_ref[:, off:off + LANES], tab_d, IDX_ROT // 2)
    ki_o[...] = ki
    kib_o[...] = ki.astype(BF16)
    off += LANES
    wi_o[...] = z_ref[:, off:off + LANES] * idx_scale


def _finish(z, g_kv, tab_m, tab_d, *, c_lora, r_mla, n_kv, idx_scale):
    m, zc = z.shape
    tm = _pick(m, TM)
    kvw = n_kv * LANES
    row = lambda i: (i, 0)
    shapes = [(c_lora, F32), (c_lora, BF16), (r_mla, F32), (LANES, BF16), (kvw, F32), (kvw, BF16),
              (kvw, F32), (kvw, BF16), (LANES, F32), (LANES, BF16), (LANES, F32)]
    return pl.pallas_call(
        functools.partial(_finish_body, c_lora=c_lora, r_mla=r_mla, n_kv=n_kv, idx_scale=idx_scale),
        grid=(m // tm,),
        in_specs=[pl.BlockSpec((tm, zc), row), pl.BlockSpec((1, c_lora), lambda i: (0, 0)),
                  pl.BlockSpec((tm, 3 * LANES), row), pl.BlockSpec((tm, 3 * LANES), row)],
        out_specs=[pl.BlockSpec((tm, w), row) for w, _ in shapes],
        out_shape=[jax.ShapeDtypeStruct((m, w), dt) for w, dt in shapes], name="finish_small",
        compiler_params=_cparams(("parallel",)),
    )(z, g_kv.reshape(1, c_lora), tab_m, tab_d)


def _row_limits(q_first, tq, l_valid):
    rows = q_first + lax.broadcasted_iota(I32, (tq, 1), 0)
    chunk_end = lambda p: (lax.shift_right_logical(p, CHUNK_LOG2) + 1) * CHUNK
    lim = jnp.minimum(chunk_end(rows), l_valid)
    lim_min = jnp.minimum(chunk_end(q_first), l_valid)
    lim_max = jnp.minimum(chunk_end(q_first + tq - 1), l_valid)
    return lim, lim_min, lim_max


LOG2E = 1.4426950408889634


def _lane_tile(x, width):
    return x if width == LANES else jnp.concatenate([x] * (width // LANES), axis=1)


def _softmax_update(s, v, m_prev, l_prev, acc_prev):
    m_new = jnp.maximum(m_prev, jnp.max(s, axis=1, keepdims=True))
    p = jnp.exp2(s - _lane_tile(m_new, s.shape[1]))
    alpha = jnp.exp2(m_prev - m_new)
    l_new = alpha * l_prev + jnp.sum(p, axis=1, keepdims=True)
    acc_new = _lane_tile(alpha, acc_prev.shape[1]) * acc_prev + jnp.dot(p.astype(BF16), v,
                                                                        preferred_element_type=F32)
    return m_new, l_new, acc_new


def _softmax_step(s, v, m_sc, acc_sc):
    sb = s.astype(BF16)
    m_prev = m_sc[...]
    m_new = jnp.maximum(m_prev, jnp.max(sb, axis=1, keepdims=True).astype(F32))
    p = jnp.exp2(sb - _lane_tile(m_new.astype(BF16), s.shape[1]))
    alpha = jnp.exp2(m_prev - m_new)
    v_ones = jnp.concatenate([v, jnp.ones(v.shape, BF16)], axis=1)
    acc_sc[...] = _lane_tile(alpha, 2 * LANES) * acc_sc[...] + jnp.dot(p, v_ones, preferred_element_type=F32)
    m_sc[...] = m_new


def _softmax_result(acc_sc):
    acc = acc_sc[...]
    return acc[:, 0:LANES] * (1.0 / acc[:, LANES:2 * LANES])


_NT = (((1,), (1,)), ((), ()))


def _gate_mix(out_a, ga, gb, out_b):
    return (ga.astype(F32) * out_a + gb.astype(F32) * out_b.astype(F32)).astype(BF16)


def _mla_body(q_ref, kt_ref, v_ref, ga_ref, gb_ref, ob_ref, o_ref, m_sc, acc_sc, s_sc,
              *, hb, tq, tk, q0, l_valid):
    q_first = q0 + pl.program_id(2) * tq
    lim, lim_min, lim_max = _row_limits(q_first, tq, l_valid)
    n_full = lim_min // tk
    n_all = (lim_max + tk - 1) // tk
    col = lax.broadcasted_iota(I32, (1, tk), 1)
    m_sc[...] = jnp.full(m_sc.shape, NEG, F32)
    acc_sc[...] = jnp.zeros(acc_sc.shape, F32)

    def scores(c, j):
        return jnp.dot(q_ref[:, j * 2 * LANES:(j + 1) * 2 * LANES], kt_ref[c, j * 2 * LANES:(j + 1) * 2 * LANES, :],
                       preferred_element_type=F32)

    for j in range(hb):
        s_sc[j] = scores(0, j)

    def step(c, masked, last):
        start = pl.multiple_of(c * tk, tk)
        for j in range(hb):
            s = s_sc[j]
            if masked:
                s = jnp.where(col + start < lim, s, NEG)
            _softmax_step(s, v_ref[pl.ds(start, tk), j * LANES:(j + 1) * LANES], m_sc.at[j], acc_sc.at[j])
            if not last:
                s_sc[j] = scores(c + 1, j)

    def full_step(c, carry):
        step(c, False, False)
        return carry

    def masked_step(c, carry):
        step(c, True, False)
        return carry

    n_rot = n_all - 1
    n_rot_full = jnp.minimum(n_full, n_rot)
    lax.fori_loop(0, n_rot_full, full_step, 0)
    lax.fori_loop(n_rot_full, n_rot, masked_step, 0)
    step(n_rot, True, True)
    for j in range(hb):
        cols = slice(j * LANES, (j + 1) * LANES)
        o_ref[:, cols] = _gate_mix(_softmax_result(acc_sc.at[j]), ga_ref[:, cols], gb_ref[:, cols], ob_ref[:, cols])


def _mla_attention(q_full, k_t, v, gates, out_b, *, n_heads, hb, tq, tk, q0, l_valid):
    b, t, _ = q_full.shape
    lp = v.shape[1]
    kv_mode = dict(pipeline_mode=pl.Buffered(1)) if t // tq > 2 else {}
    stat = pltpu.VMEM((hb, tq, LANES), F32)
    return pl.pallas_call(
        functools.partial(_mla_body, hb=hb, tq=tq, tk=tk, q0=q0, l_valid=l_valid),
        grid=(b, n_heads // hb, t // tq),
        in_specs=[pl.BlockSpec((None, tq, hb * 2 * LANES), lambda bi, h, qi: (bi, qi, h)),
                  pl.BlockSpec((None, lp // tk, hb * 2 * LANES, tk), lambda bi, h, qi: (bi, 0, h, 0), **kv_mode),
                  pl.BlockSpec((None, lp, hb * LANES), lambda bi, h, qi: (bi, 0, h)),
                  pl.BlockSpec((None, tq, hb * LANES), lambda bi, h, qi: (bi, qi, h)),
                  pl.BlockSpec((None, tq, hb * LANES), lambda bi, h, qi: (bi, qi, n_heads // hb + h)),
                  pl.BlockSpec((None, tq, hb * LANES), lambda bi, h, qi: (bi, qi, h))],
        out_specs=pl.BlockSpec((None, tq, hb * LANES), lambda bi, h, qi: (bi, qi, h)),
        out_shape=jax.ShapeDtypeStruct((b, t, n_heads * LANES), BF16),
        scratch_shapes=[stat, pltpu.VMEM((hb, tq, 2 * LANES), F32), pltpu.VMEM((hb, tq, tk), F32)],
        name="mla_attention",
        compiler_params=_cparams(("parallel", "parallel", "arbitrary")),
    )(q_full, k_t, v, gates, gates, out_b)


def _absorb_q_body(q_ref, w_ref, o_ref, *, c_lora):
    lat = lax.dot_general(q_ref[:, 0:LANES], w_ref[...], _NT, preferred_element_type=F32)
    o_ref[:, 0:c_lora] = lat.astype(o_ref.dtype)
    o_ref[:, c_lora:c_lora + LANES] = q_ref[:, LANES:2 * LANES]


def _absorb_q(q_full, w_uk, *, n_heads, c_lora):
    m = q_full.shape[0]
    return pl.pallas_call(
        functools.partial(_absorb_q_body, c_lora=c_lora), grid=(n_heads,),
        in_specs=[pl.BlockSpec((m, 2 * LANES), lambda h: (0, h)), pl.BlockSpec((c_lora, LANES), lambda h: (0, h))],
        out_specs=pl.BlockSpec((m, c_lora + LANES), lambda h: (0, h)),
        out_shape=jax.ShapeDtypeStruct((m, n_heads * (c_lora + LANES)), BF16), name="mla_absorb_q",
        compiler_params=_cparams(("parallel",)),
    )(q_full, w_uk)


def _absorb_o_body(o_lat_ref, w_ref, ga_ref, gb_ref, ob_ref, o_ref):
    out_a = jnp.dot(o_lat_ref[...], w_ref[...], preferred_element_type=F32)
    o_ref[...] = _gate_mix(out_a, ga_ref[...], gb_ref[...], ob_ref[...])


def _absorb_o(o_lat, w_uv, gates, out_b, *, n_heads, c_lora):
    m = o_lat.shape[0]
    head = lambda h: (0, h)
    return pl.pallas_call(
        _absorb_o_body, grid=(n_heads,),
        in_specs=[pl.BlockSpec((m, c_lora), head), pl.BlockSpec((c_lora, LANES), head),
                  pl.BlockSpec((m, LANES), head), pl.BlockSpec((m, LANES), lambda h: (0, n_heads + h)),
                  pl.BlockSpec((m, LANES), head)],
        out_specs=pl.BlockSpec((m, LANES), head),
        out_shape=jax.ShapeDtypeStruct((m, n_heads * LANES), BF16), name="mla_absorb_o",
        compiler_params=_cparams(("parallel",)),
    )(o_lat, w_uv, gates, gates, out_b)


def _mla_abs_body(q_ref, kc_ref, o_ref, *, bb, t, n_heads, c_lora, tk, q0, l_valid):
    rows = t * n_heads
    tok = lax.broadcasted_iota(I32, (t, n_heads, 1), 0).reshape(rows, 1)
    lim = jnp.minimum((lax.shift_right_logical(q0 + tok, CHUNK_LOG2) + 1) * CHUNK, l_valid)
    lim_min = min((q0 // CHUNK + 1) * CHUNK, l_valid)
    lim_max = min(((q0 + t - 1) // CHUNK + 1) * CHUNK, l_valid)
    col = lax.broadcasted_iota(I32, (1, tk), 1)
    for j in range(bb):
        q = q_ref[j]
        m = jnp.full((rows, LANES), NEG, F32)
        l = jnp.zeros((rows, LANES), F32)
        acc = jnp.zeros((rows, c_lora), F32)
        for c in range(-(-lim_max // tk)):
            kc = kc_ref[j, c * tk:(c + 1) * tk, :]
            s = lax.dot_general(q, kc, _NT, preferred_element_type=F32)
            if (c + 1) * tk > lim_min:
                s = jnp.where(col + c * tk < lim, s, NEG)
            m, l, acc = _softmax_update(s, kc[:, 0:c_lora], m, l, acc)
        o_ref[j] = (acc * _lane_tile(1.0 / l, c_lora)).astype(o_ref.dtype)


def _mla_abs_attention(q_abs, kc, *, t, n_heads, c_lora, tk, q0, l_valid):
    b, rows, dk = q_abs.shape
    lp = kc.shape[1]
    bb = _pick(b, 4)
    return pl.pallas_call(
        functools.partial(_mla_abs_body, bb=bb, t=t, n_heads=n_heads, c_lora=c_lora, tk=tk, q0=q0, l_valid=l_valid),
        grid=(b // bb,),
        in_specs=[pl.BlockSpec((bb, rows, dk), lambda i: (i, 0, 0)), pl.BlockSpec((bb, lp, dk), lambda i: (i, 0, 0))],
        out_specs=pl.BlockSpec((bb, rows, c_lora), lambda i: (i, 0, 0)),
        out_shape=jax.ShapeDtypeStruct((b, rows, c_lora), BF16), name="mla_absorbed_attention",
        compiler_params=_cparams(("parallel",)),
    )(q_abs, kc)


def _dsa_body(qb_ref, qi_ref, wi_ref, kit_ref, kt_ref, v_ref, o_ref,
              keys_sc, half_sc, m_sc, acc_sc, s_sc, *, tq, tk, q0, l_valid, n_idx, n_kv, rep, topk):
    q_first = q0 + pl.program_id(1) * tq
    lim, _, lim_max = _row_limits(q_first, tq, l_valid)
    n_all = (lim_max + tk - 1) // tk
    col = lax.broadcasted_iota(I32, (1, tk), 1)
    w = wi_ref[...]

    def score_chunk(c, carry):
        start = pl.multiple_of(c * tk, tk)
        kic = kit_ref[c]
        s = jnp.zeros((tq, tk), F32)
        for h in range(n_idx):
            y = jnp.dot(qi_ref[:, h * LANES:(h + 1) * LANES], kic, preferred_element_type=F32)
            s = s + w[:, h:h + 1] * jnp.maximum(y, 0.0)
        s = jnp.where(s == 0.0, 0.0, s)
        bits = lax.bitcast_convert_type(s, I32)
        key = jnp.where(bits >= 0, bits, bits ^ jnp.int32(0x7FFFFFFF))
        key = jnp.where(col + start < lim, key, jnp.int32(INT_MIN))
        keys_sc[c] = key
        half_sc[c] = lax.shift_right_arithmetic(key, HALF_BITS).astype(I16)
        return carry

    lax.fori_loop(0, n_all, score_chunk, 0)

    def count_rows(indicator):
        def count_chunk(c, acc):
            ind = indicator(c, keys_sc[c])
            part = ind[:, 0:LANES]
            for j in range(1, tk // LANES):
                part = part + ind[:, j * LANES:(j + 1) * LANES]
            return acc + part

        acc = lax.fori_loop(0, n_all, count_chunk, jnp.zeros((tq, LANES), F32))
        return jnp.sum(acc, axis=1, keepdims=True)

    def count_ge(cand):
        return count_rows(lambda c, kk: jnp.where(kk >= cand, 1.0, 0.0))

    def count_half_ge(cand):
        c16 = _lane_tile(jnp.broadcast_to(cand, (tq, LANES)).astype(I16), tk)

        def count_chunk(c, acc):
            ind = jnp.where(half_sc[c] >= c16, jnp.ones((), BF16), jnp.zeros((), BF16))
            part = ind[:, 0:LANES]
            for j in range(1, tk // LANES):
                part = part + ind[:, j * LANES:(j + 1) * LANES]
            return acc + part

        acc = lax.fori_loop(0, n_all, count_chunk, jnp.zeros((tq, LANES), BF16))
        return jnp.sum(acc.astype(F32), axis=1, keepdims=True)

    def search_half(goal):
        def bit_pass(i, carry):
            t, n_at_t = carry
            cand = t + lax.shift_left(jnp.int32(1), jnp.int32(HALF_BITS - 1) - i)
            n = count_half_ge(cand)
            keep = n >= goal
            return jnp.where(keep, cand, t), jnp.where(keep, n, n_at_t)

        return lax.fori_loop(0, HALF_BITS, bit_pass,
                             (jnp.full((tq, 1), I16_MIN, I32), jnp.full((tq, 1), -1.0, F32)))

    assert keys_sc.shape[0] * (tk // LANES) <= BF16_EXACT_INT, "per-lane partial counts must stay exact in bf16"
    t_hi, n_hi = search_half(float(topk))
    n_above = jnp.where(t_hi < I16_MAX, count_half_ge(jnp.minimum(t_hi + 1, I16_MAX)), 0.0)

    def low_halves(c, carry):
        low = jnp.bitwise_and(keys_sc[c], 2 ** HALF_BITS - 1) + I16_MIN
        half_sc[c] = jnp.where(half_sc[c] == _lane_tile(jnp.broadcast_to(t_hi, (tq, LANES)).astype(I16), tk),
                               low.astype(I16), jnp.full((), I16_MIN, I16))
        return carry

    lax.fori_loop(0, n_all, low_halves, 0)
    t_lo, n_lo = search_half(topk - n_above)
    t = t_hi * 2 ** HALF_BITS + (t_lo - I16_MIN)
    n_at_t = jnp.where(n_lo >= 0, n_above + n_lo, jnp.maximum(n_hi, 0.0))
    thr = jnp.maximum(t, jnp.int32(INT_MIN + 1))

    @pl.when(jnp.max(n_at_t) > topk)
    def _():
        slots = topk - count_ge(thr + 1)

        def tied_before(bound):
            return count_rows(lambda c, kk: jnp.where(kk == thr, jnp.where(col + c * tk < bound, 1.0, 0.0), 0.0))

        def index_pass(i, j_last):
            cand = j_last + lax.shift_left(jnp.int32(1), jnp.int32(INDEX_BITS - 1) - i)
            return jnp.where(tied_before(cand) < slots, cand, j_last)

        j_last = lax.fori_loop(0, INDEX_BITS, index_pass, jnp.zeros((tq, 1), I32))

        def demote(c, carry):
            kk = keys_sc[c]
            keys_sc[c] = jnp.where(kk == thr, jnp.where(col + c * tk > j_last, thr - 1, kk), kk)
            return carry

        lax.fori_loop(0, n_all, demote, 0)

    rows = rep * tq
    tok = jnp.bitwise_and(lax.broadcasted_iota(I32, (rows, LANES), 0), tq - 1)
    eye = jnp.where(tok == lax.broadcasted_iota(I32, (rows, LANES), 1), 1.0, 0.0).astype(BF16)

    gb = m_sc.shape[0]

    def mask_bias(c):
        bias = jnp.where(keys_sc[c] >= thr, 0.0, NEG).astype(BF16)
        if tq < LANES:
            bias = jnp.concatenate([bias, jnp.zeros((LANES - tq, tk), BF16)], axis=0)
        return bias

    def scores(c, g, bias):
        qs = jnp.concatenate([qb_ref[:, (g * rep + r) * LANES:(g * rep + r + 1) * LANES]
                              for r in range(rep)], axis=0)
        return jnp.dot(jnp.concatenate([qs, eye], axis=1), jnp.concatenate([kt_ref[g, c], bias], axis=0),
                       preferred_element_type=F32)

    def attend(c, g0, nxt_c, nxt_g0):
        start = pl.multiple_of(c * tk, tk)
        bias = None if nxt_g0 is None else mask_bias(nxt_c)
        for j in range(gb):
            g = g0 + j
            _softmax_step(s_sc[j], v_ref[pl.ds(start, tk), g * LANES:(g + 1) * LANES], m_sc.at[j], acc_sc.at[j])
            if nxt_g0 is not None:
                s_sc[j] = scores(nxt_c, nxt_g0 + j, bias)

    bias0 = mask_bias(0)
    for j in range(gb):
        s_sc[j] = scores(0, j, bias0)
    for g0 in range(0, n_kv, gb):
        m_sc[...] = jnp.full(m_sc.shape, NEG, F32)
        acc_sc[...] = jnp.zeros(acc_sc.shape, F32)

        def attend_chunk(c, carry, g0=g0):
            attend(c, g0, c + 1, g0)
            return carry

        lax.fori_loop(0, n_all - 1, attend_chunk, 0)
        attend(n_all - 1, g0, 0, g0 + gb if g0 + gb < n_kv else None)
        for j in range(gb):
            o = _softmax_result(acc_sc.at[j])
            for r in range(rep):
                h = (g0 + j) * rep + r
                o_ref[:, h * LANES:(h + 1) * LANES] = o[r * tq:(r + 1) * tq].astype(o_ref.dtype)


def _dsa_attention(qb, qi, wi, ki, k, v, *, n_kv, rep, n_idx, tq, tk, q0, l_valid, topk):
    b, t, _ = qb.shape
    lp = k.shape[1]
    kvw = n_kv * LANES
    assert tq <= LANES and tq & (tq - 1) == 0, "the token one-hot needs a power-of-two block of at most 128 rows"
    kt = k.reshape(b, lp // tk, tk, n_kv, LANES).transpose(0, 3, 1, 4, 2)
    kit = ki.reshape(b, lp // tk, tk, LANES).transpose(0, 1, 3, 2)
    kv_mode = dict(pipeline_mode=pl.Buffered(1)) if t // tq > 2 else {}
    stat = pltpu.VMEM((min(DSA_GROUPS_PER_PASS, n_kv), rep * tq, LANES), F32)
    return pl.pallas_call(
        functools.partial(_dsa_body, tq=tq, tk=tk, q0=q0, l_valid=l_valid, n_idx=n_idx, n_kv=n_kv, rep=rep,
                          topk=topk),
        grid=(b, t // tq),
        in_specs=[pl.BlockSpec((None, tq, n_kv * rep * LANES), lambda bi, qb_: (bi, qb_, 0)),
                  pl.BlockSpec((None, tq, n_idx * LANES), lambda bi, qb_: (bi, qb_, 0)),
                  pl.BlockSpec((None, tq, LANES), lambda bi, qb_: (bi, qb_, 0)),
                  pl.BlockSpec((None, lp // tk, LANES, tk), lambda bi, qb_: (bi, 0, 0, 0), **kv_mode),
                  pl.BlockSpec((None, n_kv, lp // tk, LANES, tk), lambda bi, qb_: (bi, 0, 0, 0, 0), **kv_mode),
                  pl.BlockSpec((None, lp, kvw), lambda bi, qb_: (bi, 0, 0), **kv_mode)],
        out_specs=pl.BlockSpec((None, tq, n_kv * rep * LANES), lambda bi, qb_: (bi, qb_, 0)),
        out_shape=jax.ShapeDtypeStruct((b, t, n_kv * rep * LANES), BF16),
        scratch_shapes=[pltpu.VMEM((lp // tk, tq, tk), I32), pltpu.VMEM((lp // tk, tq, tk), I16), stat,
                        pltpu.VMEM((stat.shape[0], rep * tq, 2 * LANES), F32),
                        pltpu.VMEM((stat.shape[0], rep * tq, tk), F32)],
        name="dsa_attention",
        compiler_params=pltpu.CompilerParams(dimension_semantics=("parallel", "arbitrary"),
                                             vmem_limit_bytes=VMEM_LIMIT_DSA),
    )(qb, qi, wi, kit, kt, v)


def _pad_cols(w, width):
    return jnp.pad(w, ((0, 0), (0, width - w.shape[1])))


def _prep_weights(w_in, w_uq, w_uk, w_uv, w_out, w_up, w_down, dims):
    d, q_lora, c_lora, r_mla, n_heads, n_kv, n_idx = dims
    splits = (q_lora, c_lora, r_mla, n_heads * HEAD_DIM, n_kv * HEAD_DIM, n_kv * HEAD_DIM,
              n_idx * HEAD_DIM, HEAD_DIM, n_idx, d, d)
    offs = np.concatenate([[0], np.cumsum(splits)])
    sec = [w_in[:, int(offs[i]):int(offs[i + 1])] for i in range(len(splits))]
    small = jnp.concatenate([sec[1], _pad_cols(sec[2], LANES), sec[4], sec[5], sec[7],
                             _pad_cols(sec[8], LANES)], axis=1)
    small = _pad_cols(small, -(-small.shape[1] // 512) * 512)
    parts = [("q", sec[0]), ("small", small), ("qb", sec[3]), ("qi", sec[6]),
             ("gates", jnp.concatenate([sec[9], sec[10]], axis=1))]
    windows, off = {}, 0
    for i, (key, p) in enumerate(parts):
        windows[key] = (off, p.shape[1])
        parts[i] = (key, _pad_cols(p, -(-p.shape[1] // TN) * TN))
        off += parts[i][1].shape[1]
    wq = w_uq.reshape(q_lora, n_heads, HEAD_DIM + r_mla)
    wq = jnp.pad(wq, ((0, 0), (0, 0), (0, 2 * LANES - HEAD_DIM - r_mla)))
    return dict(
        w_in=jnp.concatenate([p for _, p in parts], axis=1).astype(BF16), win=windows,
        w_uq=wq.reshape(q_lora, n_heads * 2 * LANES).astype(BF16),
        w_uk=w_uk.reshape(c_lora, n_heads * HEAD_DIM).astype(BF16),
        w_uk_t=w_uk.reshape(c_lora, n_heads * HEAD_DIM).T.astype(BF16),
        w_uv=w_uv.reshape(c_lora, n_heads * HEAD_DIM).astype(BF16),
        w_out=w_out.astype(BF16), w_up=w_up.astype(BF16), w_down=w_down.astype(BF16))


def _layer(x, ada, past, q0, pw, g_norm1, g_q_lora, g_kv_lora, g_norm2, dims, tq_mla, tq_dsa, tk):
    d, q_lora, c_lora, r_mla, n_heads, n_kv, n_idx = dims
    b, t, _ = x.shape
    m = b * t
    xf = x.reshape(m, d)
    if ada.shape[0] == 1:
        mods = [ada[:, i * d:(i + 1) * d] for i in range(N_ADA)]
    else:
        mods = [jnp.repeat(ada[:, i * d:(i + 1) * d], t, axis=0) for i in range(N_ADA)]
    sh1, sc1, gt1, sh2, sc2, gt2 = mods
    per_row = ada.shape[0] != 1

    pos = q0 + jnp.arange(t, dtype=I32)
    tab_m = jnp.tile(_rope_table(pos, MLA_THETA, r_mla), (b, 1))
    tab_d = jnp.tile(_rope_table(pos, ROPE_THETA, DSA_ROT), (b, 1))

    h = _norm(xf, g_norm1, sc1, sh1, BF16)

    def tab_extra(tab, tm):
        return (tab, (tm, 3 * LANES), lambda i, j: (i, 0))

    tm = _pick(m, 512)
    qn = _proj("proj_qlat", h, pw["w_in"], _ep_rmsnorm, BF16, cols=pw["win"]["q"], tn=q_lora,
                     extras=[(g_q_lora.reshape(1, q_lora), (1, q_lora), lambda i, j: (0, 0))])
    z_small = _proj("proj_small", h, pw["w_in"], _ep_plain, F32, cols=pw["win"]["small"],
                    tn=TN if pw["win"]["small"][1] % TN == 0 else 512)
    q_dsa = _proj("proj_qdsa", h, pw["w_in"],
                  functools.partial(_ep_rope, half=DSA_ROT // 2, period=1, roped=(0,),
                                    scale=HEAD_DIM ** -0.5 * LOG2E),
                  BF16, cols=pw["win"]["qb"], extras=[tab_extra(tab_d, tm)])
    q_idx = _proj("proj_qidx", h, pw["w_in"],
                  functools.partial(_ep_rope, half=IDX_ROT // 2, period=1, roped=(0,), scale=1.0),
                  BF16, cols=pw["win"]["qi"], extras=[tab_extra(tab_d, tm)])
    gates = _proj("proj_gates", h, pw["w_in"], _ep_sigmoid, BF16, cols=pw["win"]["gates"])

    (ckv, ckv_b, krope, krope_b, k_new, k_new_b, v_new, v_new_b, ki_new, ki_new_b, wi) = _finish(
        z_small, g_kv_lora, tab_m, tab_d, c_lora=c_lora, r_mla=r_mla, n_kv=n_kv, idx_scale=n_idx ** -0.5)

    kvw = n_kv * HEAD_DIM
    if past is None:
        l_valid = t
        lp = t
        ckv_all, krope_all, k_all, v_all, ki_all = ckv_b, krope_b, k_new_b, v_new_b, ki_new_b
    else:
        p_ckv, p_krope, p_k, p_v, p_ki = past
        l_valid = p_ckv.shape[1] + t
        lp = -(-l_valid // tk) * tk

        def cat(p, new, width):
            p = p.reshape(b, p.shape[1], -1).astype(BF16)
            if p.shape[2] < width:
                p = jnp.pad(p, ((0, 0), (0, 0), (0, width - p.shape[2])))
            full = jnp.concatenate([p, new.reshape(b, t, width)], axis=1)
            return jnp.pad(full, ((0, 0), (0, lp - l_valid), (0, 0))).reshape(b * lp, width)

        ckv_all = cat(p_ckv, ckv_b, c_lora)
        krope_all = cat(p_krope, krope_b, LANES)
        k_all = cat(p_k, k_new_b, kvw)
        v_all = cat(p_v, v_new_b, kvw)
        ki_all = cat(p_ki, ki_new_b, LANES)

    out_b = _dsa_attention(q_dsa.reshape(b, t, -1), q_idx.reshape(b, t, -1), wi.reshape(b, t, LANES),
                           ki_all.reshape(b, lp, LANES), k_all.reshape(b, lp, kvw), v_all.reshape(b, lp, kvw),
                           n_kv=n_kv, rep=n_heads // n_kv, n_idx=n_idx, tq=min(tq_dsa, t), tk=min(tk, lp),
                           q0=q0, l_valid=l_valid, topk=min(TOPK_MAX, l_valid // 4))

    tms = _pick(m, 2 * TM)
    q_full = _proj("mla_q_up", qn, pw["w_uq"],
                   functools.partial(_ep_rope, half=r_mla // 2, period=2, roped=(1,),
                                     scale=(HEAD_DIM + r_mla) ** -0.5 * LOG2E),
                   BF16, tm=tms, tn=2 * TN, extras=[tab_extra(tab_m, tms)])
    if past is not None and t * n_heads <= 1024 and n_heads % 8 == 0:
        kc = jnp.concatenate([ckv_all.reshape(b, lp, c_lora), krope_all.reshape(b, lp, LANES)], axis=2)
        q_abs = _absorb_q(q_full, pw["w_uk"], n_heads=n_heads, c_lora=c_lora)
        o_lat = _mla_abs_attention(q_abs.reshape(b, t * n_heads, c_lora + LANES), kc, t=t, n_heads=n_heads,
                                   c_lora=c_lora, tk=min(tk, lp), q0=q0, l_valid=l_valid)
        merged = _absorb_o(o_lat.reshape(m, n_heads * c_lora), pw["w_uv"], gates, out_b.reshape(m, d),
                           n_heads=n_heads, c_lora=c_lora)
    else:
        tka = min(tk, lp)
        nkb = lp // tka
        tmh = _pick(n_heads * HEAD_DIM, 4 * TM)
        k_t = _matmul(pw["w_uk_t"], ckv_all.T, n=b * lp, col0=0, tm=tmh, tn=tka, tk=c_lora,
                      extras=[(krope_all.T, (LANES, tka), lambda i, j: (0, j))],
                      outs=[(jax.ShapeDtypeStruct((b, nkb, 2 * n_heads * HEAD_DIM, tka), BF16),
                             (None, None, 2 * tmh, tka), lambda i, j: (j // nkb, j % nkb, i, 0))],
                      epilogue=_ep_kfull_t, name="mla_k_up", order="mn")[0]
        tml = _pick(b * lp, 2 * TM)
        v_mla = _proj("mla_v_up", ckv_all, pw["w_uv"], _ep_plain, BF16, tm=tml, tn=2 * TN)
        merged = _mla_attention(q_full.reshape(b, t, -1), k_t, v_mla.reshape(b, lp, -1),
                                gates.reshape(b, t, 2 * d), out_b,
                                n_heads=n_heads, hb=min(MLA_HEADS_PER_STEP, n_heads), tq=min(tq_mla, t),
                                tk=min(tk, lp), q0=q0, l_valid=l_valid).reshape(m, d)

    def gate_extra(gate, tm_, tn_):
        if per_row:
            return (gate, (tm_, tn_), lambda i, j: (i, j))
        return (gate, (1, tn_), lambda i, j: (0, j))

    tn = _pick(d, 1024)
    x1 = _proj("out_proj", merged, pw["w_out"], _ep_residual, F32,
                     extras=[(xf, (tm, tn), lambda i, j: (i, j)), gate_extra(gt1, tm, tn)])
    h2 = _norm(x1, g_norm2, sc2, sh2, BF16)
    u = _proj("mlp_up", h2, pw["w_up"], _ep_relu2, BF16)
    tm2 = _pick(m, 1024)
    x2 = _proj("mlp_down", u, pw["w_down"], _ep_residual, F32, tm=tm2, tk=2048,
               extras=[(x1, (tm2, tn), lambda i, j: (i, j)), gate_extra(gt2, tm2, tn)])
    rows = (ckv.reshape(b, t, c_lora), krope.reshape(b, t, r_mla), k_new.reshape(b, t, n_kv, HEAD_DIM),
            v_new.reshape(b, t, n_kv, HEAD_DIM), ki_new.reshape(b, t, HEAD_DIM))
    return x2, rows


def kernel(x_prompt, x_sample, c_prompt, c_sample, cache_mla_ckv, cache_mla_krope, cache_dsa_k, cache_dsa_v, cache_idx_k, w_ada, b_ada, g_norm1, w_in, g_q_lora, w_uq, g_kv_lora, w_uk, w_uv, w_out, g_norm2, w_up, w_down, g_final):
    depth = w_in.shape[0]
    bp, tp, d = x_prompt.shape
    bs, ts, _ = x_sample.shape
    n_heads = d // HEAD_DIM
    dims = (d, w_uq.shape[1], cache_mla_ckv.shape[-1], cache_mla_krope.shape[-1], n_heads,
            cache_dsa_k.shape[3], n_heads)
    past_len = cache_mla_ckv.shape[2]

    c_all = jnp.concatenate([c_prompt, c_sample], axis=0)
    n_c = c_all.shape[0]
    c_all = jnp.pad(c_all, ((0, -(-n_c // 16) * 16 - n_c), (0, 0)))

    xp, xs = x_prompt, x_sample
    rows_p, rows_s = [], []
    for l in range(depth):
        ada = _ada(c_all, w_ada[l], b_ada[l])
        pw = _prep_weights(w_in[l], w_uq[l], w_uk[l], w_uv[l], w_out[l], w_up[l], w_down[l], dims)
        norms = (g_norm1[l], g_q_lora[l], g_kv_lora[l], g_norm2[l])
        xp2, rp = _layer(xp, ada[0:bp], None, 0, pw, *norms, dims, tq_mla=TQ_MLA, tq_dsa=TQ_DSA, tk=TK_PROMPT)
        past = (cache_mla_ckv[l], cache_mla_krope[l], cache_dsa_k[l], cache_dsa_v[l], cache_idx_k[l])
        xs2, rs = _layer(xs, ada[bp:bp + bs], past, past_len, pw, *norms, dims, tq_mla=TQ_MLA, tq_dsa=TQ_DSA,
                         tk=TK_SAMPLE)
        xp, xs = xp2.reshape(bp, tp, d), xs2.reshape(bs, ts, d)
        rows_p.append(rp)
        rows_s.append(rs)

    y_prompt = _norm(xp.reshape(bp * tp, d), g_final, None, None, F32).reshape(bp, tp, d)
    y_sample = _norm(xs.reshape(bs * ts, d), g_final, None, None, F32).reshape(bs, ts, d)
    stack = lambda rows, i: jnp.stack([r[i] for r in rows])
    return (y_prompt, y_sample,
            stack(rows_p, 0), stack(rows_p, 1), stack(rows_p, 2), stack(rows_p, 3), stack(rows_p, 4),
            stack(rows_s, 0), stack(rows_s, 1), stack(rows_s, 2), stack(rows_s, 3), stack(rows_s, 4))
```

```python
import functools

import numpy as np
import jax
import jax.numpy as jnp
from jax import lax
from jax.experimental import pallas as pl
from jax.experimental.pallas import tpu as pltpu

F32 = jnp.float32
BF16 = jnp.bfloat16
I32 = jnp.int32
I16 = jnp.int16
HALF_BITS = 16
I16_MIN, I16_MAX = -2 ** (HALF_BITS - 1), 2 ** (HALF_BITS - 1) - 1
BF16_EXACT_INT = 256

LANES = 128
CHUNK = 64
CHUNK_LOG2 = 6
EPS = 1e-6
NEG = -2.0 ** 100
INT_MIN = -2 ** 31
INDEX_BITS = 24
MLA_THETA = 10000.0
ROPE_THETA = 500000.0
HEAD_DIM = 128
DSA_ROT = HEAD_DIM // 4
IDX_ROT = HEAD_DIM // 4
TOPK_MAX = 256
N_ADA = 6
VMEM_LIMIT = 56 * 1024 * 1024
VMEM_LIMIT_DSA = 60 * 1024 * 1024
TQ_MLA, TQ_DSA = 512, 128
MLA_HEADS_PER_STEP = 4
DSA_GROUPS_PER_PASS = 4
IDX_HEADS_PER_DOT = 8
TK_PROMPT, TK_SAMPLE = 512, 384


def _cparams(sem):
    return pltpu.CompilerParams(dimension_semantics=sem, vmem_limit_bytes=VMEM_LIMIT)


def _pick(n, pref):
    if n <= pref:
        return n
    t = pref
    while n % t:
        t //= 2
    return t


def _rope_table(pos, theta, rot_dim):
    half = rot_dim // 2
    inv = theta ** (-jnp.arange(half, dtype=F32) / half)
    ang = pos.astype(F32)[:, None] * inv[None, :]
    cos, sin = jnp.cos(ang), jnp.sin(ang)
    n = pos.shape[0]
    ones = jnp.ones((n, LANES - rot_dim), F32)
    zeros = jnp.zeros((n, LANES - rot_dim), F32)
    zh = jnp.zeros((n, half), F32)
    a = jnp.concatenate([cos, cos, ones], axis=1)
    b = jnp.concatenate([-sin, zh, zeros], axis=1)
    c = jnp.concatenate([zh, sin, zeros], axis=1)
    return jnp.concatenate([a, b, c], axis=1)


def _rope_slab(x, tab, half):
    a = tab[:, 0:LANES]
    b = tab[:, LANES:2 * LANES]
    c = tab[:, 2 * LANES:3 * LANES]
    return x * a + pltpu.roll(x, LANES - half, 1) * b + pltpu.roll(x, half, 1) * c


def _mm_body(*refs, nk, n_extra, n_out, epilogue):
    a_ref, w_ref = refs[0], refs[1]
    extra = refs[2:2 + n_extra]
    outs = refs[2 + n_extra:2 + n_extra + n_out]
    a = a_ref[...].astype(BF16)
    w = w_ref[...].astype(BF16)
    part = jnp.dot(a, w, preferred_element_type=F32)
    if nk == 1:
        epilogue(part, extra, outs)
        return
    acc_ref = refs[2 + n_extra + n_out]
    k = pl.program_id(2)

    @pl.when(k == 0)
    def _():
        acc_ref[...] = part

    @pl.when(k > 0)
    def _():
        acc_ref[...] += part

    @pl.when(k == nk - 1)
    def _():
        epilogue(acc_ref[...], extra, outs)


def _matmul(a, w, *, n, col0, tm, tn, tk, extras, outs, epilogue, name, order="nm"):
    m, kdim = a.shape
    assert col0 % tn == 0
    jb = col0 // tn
    gi, gj, gk = m // tm, n // tn, kdim // tk
    if order == "nm":
        grid = (gj, gi, gk)
        wrap = lambda f: (lambda pj, pi, pk: f(pi, pj, pk))
    else:
        grid = (gi, gj, gk)
        wrap = lambda f: (lambda pi, pj, pk: f(pi, pj, pk))
    in_specs = [
        pl.BlockSpec((tm, tk), wrap(lambda i, j, k: (i, k))),
        pl.BlockSpec((tk, tn), wrap(lambda i, j, k: (k, j + jb))),
    ]
    args = [a, w]
    for arr, blk, fn in extras:
        in_specs.append(pl.BlockSpec(blk, wrap(lambda i, j, k, fn=fn: fn(i, j))))
        args.append(arr)
    out_specs = [pl.BlockSpec(blk, wrap(lambda i, j, k, fn=fn: fn(i, j))) for _, blk, fn in outs]
    out_shape = [s for s, _, _ in outs]
    scratch = [pltpu.VMEM((tm, tn), F32)] if gk > 1 else []
    body = functools.partial(_mm_body, nk=gk, n_extra=len(extras), n_out=len(outs), epilogue=epilogue)
    res = pl.pallas_call(
        body, grid=grid, in_specs=in_specs, out_specs=out_specs, out_shape=out_shape,
        scratch_shapes=scratch, name=name,
        compiler_params=_cparams(("parallel", "parallel", "arbitrary")),
    )(*args)
    return res


def _ep_plain(acc, extra, outs):
    outs[0][...] = acc.astype(outs[0].dtype)


def _ep_rmsnorm(acc, extra, outs):
    g = extra[0][...]
    y = acc * lax.rsqrt(jnp.mean(acc * acc, axis=-1, keepdims=True) + EPS) * g
    outs[0][...] = y.astype(outs[0].dtype)


def _ep_sigmoid(acc, extra, outs):
    outs[0][...] = jax.nn.sigmoid(acc).astype(outs[0].dtype)


def _ep_relu2(acc, extra, outs):
    r = jnp.maximum(acc, 0.0)
    outs[0][...] = (r * r).astype(outs[0].dtype)


def _ep_residual(acc, extra, outs):
    x = extra[0][...]
    gate = extra[1][...]
    outs[0][...] = x + gate * acc


def _ep_rope(acc, extra, outs, *, half, period, roped, scale):
    tab = extra[0][...]
    o = outs[0]
    for s in range(acc.shape[1] // LANES):
        x = acc[:, s * LANES:(s + 1) * LANES]
        if (s % period) in roped:
            x = _rope_slab(x, tab, half)
        if scale != 1.0:
            x = x * scale
        o[:, s * LANES:(s + 1) * LANES] = x.astype(o.dtype)


def _ep_kfull_t(acc, extra, outs):
    kr = extra[0][...]
    o = outs[0]
    for h in range(acc.shape[0] // LANES):
        o[(2 * h) * LANES:(2 * h + 1) * LANES, :] = acc[h * LANES:(h + 1) * LANES, :].astype(o.dtype)
        o[(2 * h + 1) * LANES:(2 * h + 2) * LANES, :] = kr


TM, TN, TK = 512, 1024, 4096


def _proj(name, a, w, ep, out_dtype, *, cols=None, tm=TM, tn=TN, tk=TK, extras=(), out_cols=None, order="nm"):
    m, k = a.shape
    col0, n = (0, w.shape[1]) if cols is None else cols
    tm, tn, tk = _pick(m, tm), _pick(n, tn), _pick(k, tk)
    oc = n if out_cols is None else out_cols
    outs = [(jax.ShapeDtypeStruct((m, oc), out_dtype), (tm, tn * oc // n), lambda i, j: (i, j))]
    return _matmul(a, w, n=n, col0=col0, tm=tm, tn=tn, tk=tk, extras=list(extras), outs=outs, epilogue=ep,
                   name=name, order=order)[0]


def _ada_body(c_ref, w_ref, b_ref, o_ref):
    c = c_ref[...]
    a = (c * jax.nn.sigmoid(c)).astype(BF16)
    o_ref[...] = jnp.dot(a, w_ref[...].astype(BF16), preferred_element_type=F32) + b_ref[...]


def _ada(c_all, w_ada, b_ada):
    r, d = c_all.shape
    n = w_ada.shape[1]
    tn = _pick(n, 512)
    return pl.pallas_call(
        _ada_body, grid=(n // tn,),
        in_specs=[pl.BlockSpec((r, d), lambda j: (0, 0)),
                  pl.BlockSpec((d, tn), lambda j: (0, j)),
                  pl.BlockSpec((1, tn), lambda j: (0, j))],
        out_specs=pl.BlockSpec((r, tn), lambda j: (0, j)),
        out_shape=jax.ShapeDtypeStruct((r, n), F32), name="adaln",
        compiler_params=_cparams(("parallel",)),
    )(c_all, w_ada, b_ada.reshape(1, n))


def _norm_body(*refs, modulate):
    x = refs[0][...]
    g = refs[1][...]
    y = x * lax.rsqrt(jnp.mean(x * x, axis=-1, keepdims=True) + EPS) * g
    if modulate:
        y = y * (1.0 + refs[2][...]) + refs[3][...]
        o = refs[4]
    else:
        o = refs[2]
    o[...] = y.astype(o.dtype)


def _norm(x, g, sc, sh, out_dtype):
    m, d = x.shape
    tm = _pick(m, TM)
    in_specs = [pl.BlockSpec((tm, d), lambda i: (i, 0)), pl.BlockSpec((1, d), lambda i: (0, 0))]
    args = [x, g.reshape(1, d)]
    if sc is not None:
        per_row = sc.shape[0] == m
        blk = (tm, d) if per_row else (1, d)
        fn = (lambda i: (i, 0)) if per_row else (lambda i: (0, 0))
        in_specs += [pl.BlockSpec(blk, fn), pl.BlockSpec(blk, fn)]
        args += [sc, sh]
    return pl.pallas_call(
        functools.partial(_norm_body, modulate=sc is not None), grid=(m // tm,),
        in_specs=in_specs, out_specs=pl.BlockSpec((tm, d), lambda i: (i, 0)),
        out_shape=jax.ShapeDtypeStruct((m, d), out_dtype), name="rmsnorm",
        compiler_params=_cparams(("parallel",)),
    )(*args)


def _finish_body(z_ref, g_ref, tm_ref, td_ref,
                 ckv_o, ckvb_o, kr_o, krb_o, k_o, kb_o, v_o, vb_o, ki_o, kib_o, wi_o, ckvt_o, krt_o,
                 *, c_lora, r_mla, n_kv, idx_scale):
    tab_m = tm_ref[...]
    tab_d = td_ref[...]
    off = 0
    ckv = z_ref[:, off:off + c_lora]
    ckv = ckv * lax.rsqrt(jnp.mean(ckv * ckv, axis=-1, keepdims=True) + EPS) * g_ref[...]
    ckv_o[...] = ckv
    ckvb_o[...] = ckv.astype(BF16)
    ckvt_o[...] = ckv.T.astype(BF16)
    off += c_lora
    kr = _rope_slab(z_ref[:, off:off + LANES], tab_m, r_mla // 2)
    kr_o[...] = kr[:, 0:r_mla]
    krb_o[...] = kr.astype(BF16)
    krt_o[...] = kr.T.astype(BF16)
    off += LANES
    for h in range(n_kv):
        kh = _rope_slab(z_ref[:, off + h * LANES:off + (h + 1) * LANES], tab_d, DSA_ROT // 2)
        k_o[:, h * LANES:(h + 1) * LANES] = kh
        kb_o[:, h * LANES:(h + 1) * LANES] = kh.astype(BF16)
    off += n_kv * LANES
    v = z_ref[:, off:off + n_kv * LANES]
    v_o[...] = v
    vb_o[...] = v.astype(BF16)
    off += n_kv * LANES
    ki = _rope_slab(z_ref[:, off:off + LANES], tab_d, IDX_ROT // 2)
    ki_o[...] = ki
    kib_o[...] = ki.astype(BF16)
    off += LANES
    wi_o[...] = z_ref[:, off:off + LANES] * idx_scale


def _finish(z, g_kv, tab_m, tab_d, *, c_lora, r_mla, n_kv, idx_scale):
    m, zc = z.shape
    tm = _pick(m, TM)
    kvw = n_kv * LANES
    row = lambda i: (i, 0)
    shapes = [(c_lora, F32), (c_lora, BF16), (r_mla, F32), (LANES, BF16), (kvw, F32), (kvw, BF16),
              (kvw, F32), (kvw, BF16), (LANES, F32), (LANES, BF16), (LANES, F32)]
    return pl.pallas_call(
        functools.partial(_finish_body, c_lora=c_lora, r_mla=r_mla, n_kv=n_kv, idx_scale=idx_scale),
        grid=(m // tm,),
        in_specs=[pl.BlockSpec((tm, zc), row), pl.BlockSpec((1, c_lora), lambda i: (0, 0)),
                  pl.BlockSpec((tm, 3 * LANES), row), pl.BlockSpec((tm, 3 * LANES), row)],
        out_specs=[pl.BlockSpec((tm, w), row) for w, _ in shapes]
        + [pl.BlockSpec((w, tm), lambda i: (0, i)) for w in (c_lora, LANES)],
        out_shape=[jax.ShapeDtypeStruct((m, w), dt) for w, dt in shapes]
        + [jax.ShapeDtypeStruct((w, m), BF16) for w in (c_lora, LANES)], name="finish_small",
        compiler_params=_cparams(("parallel",)),
    )(z, g_kv.reshape(1, c_lora), tab_m, tab_d)


def _row_limits(q_first, tq, l_valid):
    rows = q_first + lax.broadcasted_iota(I32, (tq, 1), 0)
    chunk_end = lambda p: (lax.shift_right_logical(p, CHUNK_LOG2) + 1) * CHUNK
    lim = jnp.minimum(chunk_end(rows), l_valid)
    lim_min = jnp.minimum(chunk_end(q_first), l_valid)
    lim_max = jnp.minimum(chunk_end(q_first + tq - 1), l_valid)
    return lim, lim_min, lim_max


LOG2E = 1.4426950408889634


def _lane_tile(x, width):
    return x if width == LANES else jnp.concatenate([x] * (width // LANES), axis=1)


def _softmax_update(s, v, m_prev, l_prev, acc_prev):
    m_new = jnp.maximum(m_prev, jnp.max(s, axis=1, keepdims=True))
    p = jnp.exp2(s - _lane_tile(m_new, s.shape[1]))
    alpha = jnp.exp2(m_prev - m_new)
    l_new = alpha * l_prev + jnp.sum(p, axis=1, keepdims=True)
    acc_new = _lane_tile(alpha, acc_prev.shape[1]) * acc_prev + jnp.dot(p.astype(BF16), v,
                                                                        preferred_element_type=F32)
    return m_new, l_new, acc_new


def _softmax_step(s, v, m_sc, acc_sc):
    sb = s.astype(BF16)
    m_prev = m_sc[...]
    m_new = jnp.maximum(m_prev, jnp.max(sb, axis=1, keepdims=True).astype(F32))
    p = jnp.exp2(sb - _lane_tile(m_new.astype(BF16), s.shape[1]))
    alpha = jnp.exp2(m_prev - m_new)
    v_ones = jnp.concatenate([v, jnp.ones(v.shape, BF16)], axis=1)
    acc_sc[...] = _lane_tile(alpha, 2 * LANES) * acc_sc[...] + jnp.dot(p, v_ones, preferred_element_type=F32)
    m_sc[...] = m_new


def _softmax_result(acc_sc):
    acc = acc_sc[...]
    return acc[:, 0:LANES] * (1.0 / acc[:, LANES:2 * LANES])


_NT = (((1,), (1,)), ((), ()))


def _gate_mix(out_a, ga, gb, out_b):
    return (ga.astype(F32) * out_a + gb.astype(F32) * out_b.astype(F32)).astype(BF16)


def _mla_body(q_ref, kt_ref, v_ref, ga_ref, gb_ref, ob_ref, o_ref, m_sc, acc_sc, s_sc,
              *, hb, tq, tk, q0, l_valid):
    q_first = q0 + pl.program_id(2) * tq
    lim, lim_min, lim_max = _row_limits(q_first, tq, l_valid)
    n_full = lim_min // tk
    n_all = (lim_max + tk - 1) // tk
    col = lax.broadcasted_iota(I32, (1, tk), 1)
    m_sc[...] = jnp.full(m_sc.shape, NEG, F32)
    acc_sc[...] = jnp.zeros(acc_sc.shape, F32)

    def scores(c, j):
        return jnp.dot(q_ref[:, j * 2 * LANES:(j + 1) * 2 * LANES], kt_ref[c, j * 2 * LANES:(j + 1) * 2 * LANES, :],
                       preferred_element_type=F32)

    for j in range(hb):
        s_sc[j] = scores(0, j)

    def step(c, masked, last):
        start = pl.multiple_of(c * tk, tk)
        for j in range(hb):
            s = s_sc[j]
            if masked:
                s = jnp.where(col + start < lim, s, NEG)
            _softmax_step(s, v_ref[pl.ds(start, tk), j * LANES:(j + 1) * LANES], m_sc.at[j], acc_sc.at[j])
            if not last:
                s_sc[j] = scores(c + 1, j)

    def full_step(c, carry):
        step(c, False, False)
        return carry

    def masked_step(c, carry):
        step(c, True, False)
        return carry

    n_rot = n_all - 1
    n_rot_full = jnp.minimum(n_full, n_rot)
    lax.fori_loop(0, n_rot_full, full_step, 0)
    lax.fori_loop(n_rot_full, n_rot, masked_step, 0)
    step(n_rot, True, True)
    for j in range(hb):
        cols = slice(j * LANES, (j + 1) * LANES)
        o_ref[:, cols] = _gate_mix(_softmax_result(acc_sc.at[j]), ga_ref[:, cols], gb_ref[:, cols], ob_ref[:, cols])


def _mla_attention(q_full, k_t, v, gates, out_b, *, n_heads, hb, tq, tk, q0, l_valid):
    b, t, _ = q_full.shape
    lp = v.shape[1]
    kv_mode = dict(pipeline_mode=pl.Buffered(1)) if t // tq > 2 else {}
    stat = pltpu.VMEM((hb, tq, LANES), F32)
    return pl.pallas_call(
        functools.partial(_mla_body, hb=hb, tq=tq, tk=tk, q0=q0, l_valid=l_valid),
        grid=(b, n_heads // hb, t // tq),
        in_specs=[pl.BlockSpec((None, tq, hb * 2 * LANES), lambda bi, h, qi: (bi, qi, h)),
                  pl.BlockSpec((None, lp // tk, hb * 2 * LANES, tk), lambda bi, h, qi: (bi, 0, h, 0), **kv_mode),
                  pl.BlockSpec((None, lp, hb * LANES), lambda bi, h, qi: (bi, 0, h)),
                  pl.BlockSpec((None, tq, hb * LANES), lambda bi, h, qi: (bi, qi, h)),
                  pl.BlockSpec((None, tq, hb * LANES), lambda bi, h, qi: (bi, qi, n_heads // hb + h)),
                  pl.BlockSpec((None, tq, hb * LANES), lambda bi, h, qi: (bi, qi, h))],
        out_specs=pl.BlockSpec((None, tq, hb * LANES), lambda bi, h, qi: (bi, qi, h)),
        out_shape=jax.ShapeDtypeStruct((b, t, n_heads * LANES), BF16),
        scratch_shapes=[stat, pltpu.VMEM((hb, tq, 2 * LANES), F32), pltpu.VMEM((hb, tq, tk), F32)],
        name="mla_attention",
        compiler_params=_cparams(("parallel", "parallel", "arbitrary")),
    )(q_full, k_t, v, gates, gates, out_b)


def _absorb_q_body(q_ref, w_ref, o_ref, *, c_lora):
    lat = lax.dot_general(q_ref[:, 0:LANES], w_ref[...], _NT, preferred_element_type=F32)
    o_ref[:, 0:c_lora] = lat.astype(o_ref.dtype)
    o_ref[:, c_lora:c_lora + LANES] = q_ref[:, LANES:2 * LANES]


def _absorb_q(q_full, w_uk, *, n_heads, c_lora):
    m = q_full.shape[0]
    return pl.pallas_call(
        functools.partial(_absorb_q_body, c_lora=c_lora), grid=(n_heads,),
        in_specs=[pl.BlockSpec((m, 2 * LANES), lambda h: (0, h)), pl.BlockSpec((c_lora, LANES), lambda h: (0, h))],
        out_specs=pl.BlockSpec((m, c_lora + LANES), lambda h: (0, h)),
        out_shape=jax.ShapeDtypeStruct((m, n_heads * (c_lora + LANES)), BF16), name="mla_absorb_q",
        compiler_params=_cparams(("parallel",)),
    )(q_full, w_uk)


def _absorb_o_body(o_lat_ref, w_ref, ga_ref, gb_ref, ob_ref, o_ref):
    out_a = jnp.dot(o_lat_ref[...], w_ref[...], preferred_element_type=F32)
    o_ref[...] = _gate_mix(out_a, ga_ref[...], gb_ref[...], ob_ref[...])


def _absorb_o(o_lat, w_uv, gates, out_b, *, n_heads, c_lora):
    m = o_lat.shape[0]
    head = lambda h: (0, h)
    return pl.pallas_call(
        _absorb_o_body, grid=(n_heads,),
        in_specs=[pl.BlockSpec((m, c_lora), head), pl.BlockSpec((c_lora, LANES), head),
                  pl.BlockSpec((m, LANES), head), pl.BlockSpec((m, LANES), lambda h: (0, n_heads + h)),
                  pl.BlockSpec((m, LANES), head)],
        out_specs=pl.BlockSpec((m, LANES), head),
        out_shape=jax.ShapeDtypeStruct((m, n_heads * LANES), BF16), name="mla_absorb_o",
        compiler_params=_cparams(("parallel",)),
    )(o_lat, w_uv, gates, gates, out_b)


def _mla_abs_body(q_ref, kc_ref, o_ref, *, bb, t, n_heads, c_lora, tk, q0, l_valid):
    rows = t * n_heads
    tok = lax.broadcasted_iota(I32, (t, n_heads, 1), 0).reshape(rows, 1)
    lim = jnp.minimum((lax.shift_right_logical(q0 + tok, CHUNK_LOG2) + 1) * CHUNK, l_valid)
    lim_min = min((q0 // CHUNK + 1) * CHUNK, l_valid)
    lim_max = min(((q0 + t - 1) // CHUNK + 1) * CHUNK, l_valid)
    col = lax.broadcasted_iota(I32, (1, tk), 1)
    for j in range(bb):
        q = q_ref[j]
        m = jnp.full((rows, LANES), NEG, F32)
        l = jnp.zeros((rows, LANES), F32)
        acc = jnp.zeros((rows, c_lora), F32)
        for c in range(-(-lim_max // tk)):
            kc = kc_ref[j, c * tk:(c + 1) * tk, :]
            s = lax.dot_general(q, kc, _NT, preferred_element_type=F32)
            if (c + 1) * tk > lim_min:
                s = jnp.where(col + c * tk < lim, s, NEG)
            m, l, acc = _softmax_update(s, kc[:, 0:c_lora], m, l, acc)
        o_ref[j] = (acc * _lane_tile(1.0 / l, c_lora)).astype(o_ref.dtype)


def _mla_abs_attention(q_abs, kc, *, t, n_heads, c_lora, tk, q0, l_valid):
    b, rows, dk = q_abs.shape
    lp = kc.shape[1]
    bb = _pick(b, 4)
    return pl.pallas_call(
        functools.partial(_mla_abs_body, bb=bb, t=t, n_heads=n_heads, c_lora=c_lora, tk=tk, q0=q0, l_valid=l_valid),
        grid=(b // bb,),
        in_specs=[pl.BlockSpec((bb, rows, dk), lambda i: (i, 0, 0)), pl.BlockSpec((bb, lp, dk), lambda i: (i, 0, 0))],
        out_specs=pl.BlockSpec((bb, rows, c_lora), lambda i: (i, 0, 0)),
        out_shape=jax.ShapeDtypeStruct((b, rows, c_lora), BF16), name="mla_absorbed_attention",
        compiler_params=_cparams(("parallel",)),
    )(q_abs, kc)


def _dsa_body(qb_ref, qi_ref, wi_ref, kit_ref, kt_ref, v_ref, o_ref,
              keys_sc, half_sc, m_sc, acc_sc, s_sc, *, tq, tk, q0, l_valid, n_idx, n_kv, rep, topk):
    q_first = q0 + pl.program_id(1) * tq
    lim, _, lim_max = _row_limits(q_first, tq, l_valid)
    n_all = (lim_max + tk - 1) // tk
    col = lax.broadcasted_iota(I32, (1, tk), 1)
    w = wi_ref[...]

    def score_chunk(c, carry):
        start = pl.multiple_of(c * tk, tk)
        kic = kit_ref[c]
        s = jnp.zeros((tq, tk), F32)
        for h0 in range(0, n_idx, IDX_HEADS_PER_DOT):
            hs = range(h0, min(h0 + IDX_HEADS_PER_DOT, n_idx))
            y = jnp.dot(jnp.concatenate([qi_ref[:, h * LANES:(h + 1) * LANES] for h in hs], axis=0), kic,
                        preferred_element_type=F32)
            for i, h in enumerate(hs):
                s = s + w[:, h:h + 1] * jnp.maximum(y[i * tq:(i + 1) * tq], 0.0)
        s = jnp.where(s == 0.0, 0.0, s)
        bits = lax.bitcast_convert_type(s, I32)
        key = jnp.where(bits >= 0, bits, bits ^ jnp.int32(0x7FFFFFFF))
        key = jnp.where(col + start < lim, key, jnp.int32(INT_MIN))
        keys_sc[c] = key
        half_sc[c] = lax.shift_right_arithmetic(key, HALF_BITS).astype(I16)
        return carry

    lax.fori_loop(0, n_all, score_chunk, 0)

    def count_rows(indicator):
        def count_chunk(c, acc):
            ind = indicator(c, keys_sc[c])
            part = ind[:, 0:LANES]
            for j in range(1, tk // LANES):
                part = part + ind[:, j * LANES:(j + 1) * LANES]
            return acc + part

        acc = lax.fori_loop(0, n_all, count_chunk, jnp.zeros((tq, LANES), F32))
        return jnp.sum(acc, axis=1, keepdims=True)

    def count_ge(cand):
        return count_rows(lambda c, kk: jnp.where(kk >= cand, 1.0, 0.0))

    def count_half_ge(cand):
        c16 = _lane_tile(jnp.broadcast_to(cand, (tq, LANES)).astype(I16), tk)

        def count_chunk(c, acc):
            ind = jnp.where(half_sc[c] >= c16, jnp.ones((), BF16), jnp.zeros((), BF16))
            part = ind[:, 0:LANES]
            for j in range(1, tk // LANES):
                part = part + ind[:, j * LANES:(j + 1) * LANES]
            return acc + part

        acc = lax.fori_loop(0, n_all, count_chunk, jnp.zeros((tq, LANES), BF16))
        return jnp.sum(acc.astype(F32), axis=1, keepdims=True)

    def search_half(goal):
        def bit_pass(i, carry):
            t, n_at_t = carry
            cand = t + lax.shift_left(jnp.int32(1), jnp.int32(HALF_BITS - 1) - i)
            n = count_half_ge(cand)
            keep = n >= goal
            return jnp.where(keep, cand, t), jnp.where(keep, n, n_at_t)

        return lax.fori_loop(0, HALF_BITS, bit_pass,
                             (jnp.full((tq, 1), I16_MIN, I32), jnp.full((tq, 1), -1.0, F32)))

    assert keys_sc.shape[0] * (tk // LANES) <= BF16_EXACT_INT, "per-lane partial counts must stay exact in bf16"
    t_hi, n_hi = search_half(float(topk))
    n_above = jnp.where(t_hi < I16_MAX, count_half_ge(jnp.minimum(t_hi + 1, I16_MAX)), 0.0)

    def low_halves(c, carry):
        low = jnp.bitwise_and(keys_sc[c], 2 ** HALF_BITS - 1) + I16_MIN
        half_sc[c] = jnp.where(half_sc[c] == _lane_tile(jnp.broadcast_to(t_hi, (tq, LANES)).astype(I16), tk),
                               low.astype(I16), jnp.full((), I16_MIN, I16))
        return carry

    lax.fori_loop(0, n_all, low_halves, 0)
    t_lo, n_lo = search_half(topk - n_above)
    t = t_hi * 2 ** HALF_BITS + (t_lo - I16_MIN)
    n_at_t = jnp.where(n_lo >= 0, n_above + n_lo, jnp.maximum(n_hi, 0.0))
    thr = jnp.maximum(t, jnp.int32(INT_MIN + 1))

    @pl.when(jnp.max(n_at_t) > topk)
    def _():
        slots = topk - count_ge(thr + 1)

        def tied_before(bound):
            return count_rows(lambda c, kk: jnp.where(kk == thr, jnp.where(col + c * tk < bound, 1.0, 0.0), 0.0))

        def index_pass(i, j_last):
            cand = j_last + lax.shift_left(jnp.int32(1), jnp.int32(INDEX_BITS - 1) - i)
            return jnp.where(tied_before(cand) < slots, cand, j_last)

        j_last = lax.fori_loop(0, INDEX_BITS, index_pass, jnp.zeros((tq, 1), I32))

        def demote(c, carry):
            kk = keys_sc[c]
            keys_sc[c] = jnp.where(kk == thr, jnp.where(col + c * tk > j_last, thr - 1, kk), kk)
            return carry

        lax.fori_loop(0, n_all, demote, 0)

    rows = rep * tq
    tok = jnp.bitwise_and(lax.broadcasted_iota(I32, (rows, LANES), 0), tq - 1)
    eye = jnp.where(tok == lax.broadcasted_iota(I32, (rows, LANES), 1), 1.0, 0.0).astype(BF16)

    gb = m_sc.shape[0]

    def mask_bias(c):
        bias = jnp.where(keys_sc[c] >= thr, 0.0, NEG).astype(BF16)
        if tq < LANES:
            bias = jnp.concatenate([bias, jnp.zeros((LANES - tq, tk), BF16)], axis=0)
        return bias

    def scores(c, g, bias):
        qs = jnp.concatenate([qb_ref[:, (g * rep + r) * LANES:(g * rep + r + 1) * LANES]
                              for r in range(rep)], axis=0)
        return jnp.dot(jnp.concatenate([qs, eye], axis=1), jnp.concatenate([kt_ref[g, c], bias], axis=0),
                       preferred_element_type=F32)

    def attend(c, g0, nxt_c, nxt_g0):
        start = pl.multiple_of(c * tk, tk)
        bias = None if nxt_g0 is None else mask_bias(nxt_c)
        for j in range(gb):
            g = g0 + j
            _softmax_step(s_sc[j], v_ref[pl.ds(start, tk), g * LANES:(g + 1) * LANES], m_sc.at[j], acc_sc.at[j])
            if nxt_g0 is not None:
                s_sc[j] = scores(nxt_c, nxt_g0 + j, bias)

    bias0 = mask_bias(0)
    for j in range(gb):
        s_sc[j] = scores(0, j, bias0)
    for g0 in range(0, n_kv, gb):
        m_sc[...] = jnp.full(m_sc.shape, NEG, F32)
        acc_sc[...] = jnp.zeros(acc_sc.shape, F32)

        def attend_chunk(c, carry, g0=g0):
            attend(c, g0, c + 1, g0)
            return carry

        lax.fori_loop(0, n_all - 1, attend_chunk, 0)
        attend(n_all - 1, g0, 0, g0 + gb if g0 + gb < n_kv else None)
        for j in range(gb):
            o = _softmax_result(acc_sc.at[j])
            for r in range(rep):
                h = (g0 + j) * rep + r
                o_ref[:, h * LANES:(h + 1) * LANES] = o[r * tq:(r + 1) * tq].astype(o_ref.dtype)


def _dsa_attention(qb, qi, wi, ki, k, v, *, n_kv, rep, n_idx, tq, tk, q0, l_valid, topk):
    b, t, _ = qb.shape
    lp = k.shape[1]
    kvw = n_kv * LANES
    assert tq <= LANES and tq & (tq - 1) == 0, "the token one-hot needs a power-of-two block of at most 128 rows"
    kt = k.reshape(b, lp // tk, tk, n_kv, LANES).transpose(0, 3, 1, 4, 2)
    kit = ki.reshape(b, lp // tk, tk, LANES).transpose(0, 1, 3, 2)
    kv_mode = dict(pipeline_mode=pl.Buffered(1)) if t // tq > 2 else {}
    stat = pltpu.VMEM((min(DSA_GROUPS_PER_PASS, n_kv), rep * tq, LANES), F32)
    return pl.pallas_call(
        functools.partial(_dsa_body, tq=tq, tk=tk, q0=q0, l_valid=l_valid, n_idx=n_idx, n_kv=n_kv, rep=rep,
                          topk=topk),
        grid=(b, t // tq),
        in_specs=[pl.BlockSpec((None, tq, n_kv * rep * LANES), lambda bi, qb_: (bi, qb_, 0)),
                  pl.BlockSpec((None, tq, n_idx * LANES), lambda bi, qb_: (bi, qb_, 0)),
                  pl.BlockSpec((None, tq, LANES), lambda bi, qb_: (bi, qb_, 0)),
                  pl.BlockSpec((None, lp // tk, LANES, tk), lambda bi, qb_: (bi, 0, 0, 0), **kv_mode),
                  pl.BlockSpec((None, n_kv, lp // tk, LANES, tk), lambda bi, qb_: (bi, 0, 0, 0, 0), **kv_mode),
                  pl.BlockSpec((None, lp, kvw), lambda bi, qb_: (bi, 0, 0), **kv_mode)],
        out_specs=pl.BlockSpec((None, tq, n_kv * rep * LANES), lambda bi, qb_: (bi, qb_, 0)),
        out_shape=jax.ShapeDtypeStruct((b, t, n_kv * rep * LANES), BF16),
        scratch_shapes=[pltpu.VMEM((lp // tk, tq, tk), I32), pltpu.VMEM((lp // tk, tq, tk), I16), stat,
                        pltpu.VMEM((stat.shape[0], rep * tq, 2 * LANES), F32),
                        pltpu.VMEM((stat.shape[0], rep * tq, tk), F32)],
        name="dsa_attention",
        compiler_params=pltpu.CompilerParams(dimension_semantics=("parallel", "arbitrary"),
                                             vmem_limit_bytes=VMEM_LIMIT_DSA),
    )(qb, qi, wi, kit, kt, v)


def _pad_cols(w, width):
    return jnp.pad(w, ((0, 0), (0, width - w.shape[1])))


def _prep_weights(w_in, w_uq, w_uk, w_uv, w_out, w_up, w_down, dims):
    d, q_lora, c_lora, r_mla, n_heads, n_kv, n_idx = dims
    splits = (q_lora, c_lora, r_mla, n_heads * HEAD_DIM, n_kv * HEAD_DIM, n_kv * HEAD_DIM,
              n_idx * HEAD_DIM, HEAD_DIM, n_idx, d, d)
    offs = np.concatenate([[0], np.cumsum(splits)])
    sec = [w_in[:, int(offs[i]):int(offs[i + 1])] for i in range(len(splits))]
    small = jnp.concatenate([sec[1], _pad_cols(sec[2], LANES), sec[4], sec[5], sec[7],
                             _pad_cols(sec[8], LANES)], axis=1)
    small = _pad_cols(small, -(-small.shape[1] // 512) * 512)
    parts = [("q", sec[0]), ("small", small), ("qb", sec[3]), ("qi", sec[6]),
             ("gates", jnp.concatenate([sec[9], sec[10]], axis=1))]
    windows, off = {}, 0
    for i, (key, p) in enumerate(parts):
        windows[key] = (off, p.shape[1])
        parts[i] = (key, _pad_cols(p, -(-p.shape[1] // TN) * TN))
        off += parts[i][1].shape[1]
    wq = w_uq.reshape(q_lora, n_heads, HEAD_DIM + r_mla)
    wq = jnp.pad(wq, ((0, 0), (0, 0), (0, 2 * LANES - HEAD_DIM - r_mla)))
    return dict(
        w_in=jnp.concatenate([p for _, p in parts], axis=1).astype(BF16), win=windows,
        w_uq=wq.reshape(q_lora, n_heads * 2 * LANES).astype(BF16),
        w_uk=w_uk.reshape(c_lora, n_heads * HEAD_DIM).astype(BF16),
        w_uk_t=w_uk.reshape(c_lora, n_heads * HEAD_DIM).T.astype(BF16),
        w_uv=w_uv.reshape(c_lora, n_heads * HEAD_DIM).astype(BF16),
        w_out=w_out.astype(BF16), w_up=w_up.astype(BF16), w_down=w_down.astype(BF16))


def _layer(x, ada, past, q0, pw, g_norm1, g_q_lora, g_kv_lora, g_norm2, dims, tq_mla, tq_dsa, tk):
    d, q_lora, c_lora, r_mla, n_heads, n_kv, n_idx = dims
    b, t, _ = x.shape
    m = b * t
    xf = x.reshape(m, d)
    if ada.shape[0] == 1:
        mods = [ada[:, i * d:(i + 1) * d] for i in range(N_ADA)]
    else:
        mods = [jnp.repeat(ada[:, i * d:(i + 1) * d], t, axis=0) for i in range(N_ADA)]
    sh1, sc1, gt1, sh2, sc2, gt2 = mods
    per_row = ada.shape[0] != 1

    pos = q0 + jnp.arange(t, dtype=I32)
    tab_m = jnp.tile(_rope_table(pos, MLA_THETA, r_mla), (b, 1))
    tab_d = jnp.tile(_rope_table(pos, ROPE_THETA, DSA_ROT), (b, 1))

    h = _norm(xf, g_norm1, sc1, sh1, BF16)

    def tab_extra(tab, tm):
        return (tab, (tm, 3 * LANES), lambda i, j: (i, 0))

    tm = _pick(m, 512)
    qn = _proj("proj_qlat", h, pw["w_in"], _ep_rmsnorm, BF16, cols=pw["win"]["q"], tn=q_lora,
                     extras=[(g_q_lora.reshape(1, q_lora), (1, q_lora), lambda i, j: (0, 0))])
    z_small = _proj("proj_small", h, pw["w_in"], _ep_plain, F32, cols=pw["win"]["small"],
                    tn=TN if pw["win"]["small"][1] % TN == 0 else 512)
    q_dsa = _proj("proj_qdsa", h, pw["w_in"],
                  functools.partial(_ep_rope, half=DSA_ROT // 2, period=1, roped=(0,),
                                    scale=HEAD_DIM ** -0.5 * LOG2E),
                  BF16, cols=pw["win"]["qb"], extras=[tab_extra(tab_d, tm)])
    q_idx = _proj("proj_qidx", h, pw["w_in"],
                  functools.partial(_ep_rope, half=IDX_ROT // 2, period=1, roped=(0,), scale=1.0),
                  BF16, cols=pw["win"]["qi"], extras=[tab_extra(tab_d, tm)])
    gates = _proj("proj_gates", h, pw["w_in"], _ep_sigmoid, BF16, cols=pw["win"]["gates"])

    (ckv, ckv_b, krope, krope_b, k_new, k_new_b, v_new, v_new_b, ki_new, ki_new_b, wi, ckv_t, krope_t) = _finish(
        z_small, g_kv_lora, tab_m, tab_d, c_lora=c_lora, r_mla=r_mla, n_kv=n_kv, idx_scale=n_idx ** -0.5)

    kvw = n_kv * HEAD_DIM
    if past is None:
        l_valid = t
        lp = t
        ckv_all, krope_all, k_all, v_all, ki_all = ckv_b, krope_b, k_new_b, v_new_b, ki_new_b
    else:
        p_ckv, p_krope, p_k, p_v, p_ki = past
        l_valid = p_ckv.shape[1] + t
        lp = -(-l_valid // tk) * tk

        def cat(p, new, width):
            p = p.reshape(b, p.shape[1], -1).astype(BF16)
            if p.shape[2] < width:
                p = jnp.pad(p, ((0, 0), (0, 0), (0, width - p.shape[2])))
            full = jnp.concatenate([p, new.reshape(b, t, width)], axis=1)
            return jnp.pad(full, ((0, 0), (0, lp - l_valid), (0, 0))).reshape(b * lp, width)

        ckv_all = cat(p_ckv, ckv_b, c_lora)
        krope_all = cat(p_krope, krope_b, LANES)
        k_all = cat(p_k, k_new_b, kvw)
        v_all = cat(p_v, v_new_b, kvw)
        ki_all = cat(p_ki, ki_new_b, LANES)

    out_b = _dsa_attention(q_dsa.reshape(b, t, -1), q_idx.reshape(b, t, -1), wi.reshape(b, t, LANES),
                           ki_all.reshape(b, lp, LANES), k_all.reshape(b, lp, kvw), v_all.reshape(b, lp, kvw),
                           n_kv=n_kv, rep=n_heads // n_kv, n_idx=n_idx, tq=min(tq_dsa, t), tk=min(tk, lp),
                           q0=q0, l_valid=l_valid, topk=min(TOPK_MAX, l_valid // 4))

    tms = _pick(m, 2 * TM)
    q_full = _proj("mla_q_up", qn, pw["w_uq"],
                   functools.partial(_ep_rope, half=r_mla // 2, period=2, roped=(1,),
                                     scale=(HEAD_DIM + r_mla) ** -0.5 * LOG2E),
                   BF16, tm=tms, tn=2 * TN, extras=[tab_extra(tab_m, tms)])
    if past is not None and t * n_heads <= 1024 and n_heads % 8 == 0:
        kc = jnp.concatenate([ckv_all.reshape(b, lp, c_lora), krope_all.reshape(b, lp, LANES)], axis=2)
        q_abs = _absorb_q(q_full, pw["w_uk"], n_heads=n_heads, c_lora=c_lora)
        o_lat = _mla_abs_attention(q_abs.reshape(b, t * n_heads, c_lora + LANES), kc, t=t, n_heads=n_heads,
                                   c_lora=c_lora, tk=min(tk, lp), q0=q0, l_valid=l_valid)
        merged = _absorb_o(o_lat.reshape(m, n_heads * c_lora), pw["w_uv"], gates, out_b.reshape(m, d),
                           n_heads=n_heads, c_lora=c_lora)
    else:
        tka = min(tk, lp)
        nkb = lp // tka
        tmh = _pick(n_heads * HEAD_DIM, 4 * TM)
        if past is not None:
            ckv_t, krope_t = ckv_all.T, krope_all.T
        k_t = _matmul(pw["w_uk_t"], ckv_t, n=b * lp, col0=0, tm=tmh, tn=tka, tk=c_lora,
                      extras=[(krope_t, (LANES, tka), lambda i, j: (0, j))],
                      outs=[(jax.ShapeDtypeStruct((b, nkb, 2 * n_heads * HEAD_DIM, tka), BF16),
                             (None, None, 2 * tmh, tka), lambda i, j: (j // nkb, j % nkb, i, 0))],
                      epilogue=_ep_kfull_t, name="mla_k_up", order="mn")[0]
        tml = _pick(b * lp, 2 * TM)
        v_mla = _proj("mla_v_up", ckv_all, pw["w_uv"], _ep_plain, BF16, tm=tml, tn=2 * TN)
        merged = _mla_attention(q_full.reshape(b, t, -1), k_t, v_mla.reshape(b, lp, -1),
                                gates.reshape(b, t, 2 * d), out_b,
                                n_heads=n_heads, hb=min(MLA_HEADS_PER_STEP, n_heads), tq=min(tq_mla, t),
                                tk=min(tk, lp), q0=q0, l_valid=l_valid).reshape(m, d)

    def gate_extra(gate, tm_, tn_):
        if per_row:
            return (gate, (tm_, tn_), lambda i, j: (i, j))
        return (gate, (1, tn_), lambda i, j: (0, j))

    tn = _pick(d, 1024)
    x1 = _proj("out_proj", merged, pw["w_out"], _ep_residual, F32,
                     extras=[(xf, (tm, tn), lambda i, j: (i, j)), gate_extra(gt1, tm, tn)])
    h2 = _norm(x1, g_norm2, sc2, sh2, BF16)
    u = _proj("mlp_up", h2, pw["w_up"], _ep_relu2, BF16)
    tm2 = _pick(m, 1024)
    x2 = _proj("mlp_down", u, pw["w_down"], _ep_residual, F32, tm=tm2, tk=2048,
               extras=[(x1, (tm2, tn), lambda i, j: (i, j)), gate_extra(gt2, tm2, tn)])
    rows = (ckv.reshape(b, t, c_lora), krope.reshape(b, t, r_mla), k_new.reshape(b, t, n_kv, HEAD_DIM),
            v_new.reshape(b, t, n_kv, HEAD_DIM), ki_new.reshape(b, t, HEAD_DIM))
    return x2, rows


def kernel(x_prompt, x_sample, c_prompt, c_sample, cache_mla_ckv, cache_mla_krope, cache_dsa_k, cache_dsa_v, cache_idx_k, w_ada, b_ada, g_norm1, w_in, g_q_lora, w_uq, g_kv_lora, w_uk, w_uv, w_out, g_norm2, w_up, w_down, g_final):
    depth = w_in.shape[0]
    bp, tp, d = x_prompt.shape
    bs, ts, _ = x_sample.shape
    n_heads = d // HEAD_DIM
    dims = (d, w_uq.shape[1], cache_mla_ckv.shape[-1], cache_mla_krope.shape[-1], n_heads,
            cache_dsa_k.shape[3], n_heads)
    past_len = cache_mla_ckv.shape[2]

    c_all = jnp.concatenate([c_prompt, c_sample], axis=0)
    n_c = c_all.shape[0]
    c_all = jnp.pad(c_all, ((0, -(-n_c // 16) * 16 - n_c), (0, 0)))

    xp, xs = x_prompt, x_sample
    rows_p, rows_s = [], []
    for l in range(depth):
        ada = _ada(c_all, w_ada[l], b_ada[l])
        pw = _prep_weights(w_in[l], w_uq[l], w_uk[l], w_uv[l], w_out[l], w_up[l], w_down[l], dims)
        norms = (g_norm1[l], g_q_lora[l], g_kv_lora[l], g_norm2[l])
        xp2, rp = _layer(xp, ada[0:bp], None, 0, pw, *norms, dims, tq_mla=TQ_MLA, tq_dsa=TQ_DSA, tk=TK_PROMPT)
        past = (cache_mla_ckv[l], cache_mla_krope[l], cache_dsa_k[l], cache_dsa_v[l], cache_idx_k[l])
        xs2, rs = _layer(xs, ada[bp:bp + bs], past, past_len, pw, *norms, dims, tq_mla=TQ_MLA, tq_dsa=TQ_DSA,
                         tk=TK_SAMPLE)
        xp, xs = xp2.reshape(bp, tp, d), xs2.reshape(bs, ts, d)
        rows_p.append(rp)
        rows_s.append(rs)

    y_prompt = _norm(xp.reshape(bp * tp, d), g_final, None, None, F32).reshape(bp, tp, d)
    y_sample = _norm(xs.reshape(bs * ts, d), g_final, None, None, F32).reshape(bs, ts, d)
    stack = lambda rows, i: jnp.stack([r[i] for r in rows])
    return (y_prompt, y_sample,
            stack(rows_p, 0), stack(rows_p, 1), stack(rows_p, 2), stack(rows_p, 3), stack(rows_p, 4),
            stack(rows_s, 0), stack(rows_s, 1), stack(rows_s, 2), stack(rows_s, 3), stack(rows_s, 4))
```
